```python
import jax, jax.numpy as jnp
from jax import lax
import numpy as np

D_MODEL = 1024
BATCH = 16
SEQ = 4096
DEPTH = 1
DEC_BATCH = 8
DEC_SEQ = 8192
PAST_LEN = 128

GRID_W = 64
N_Q_HEADS = 8
N_KV_HEADS = 2
GQA_GROUP = N_Q_HEADS // N_KV_HEADS
HEAD_DIM = 64
ATTN_WIDTH = N_Q_HEADS * HEAD_DIM
KV_WIDTH = N_KV_HEADS * HEAD_DIM
ROPE_THETA = 10000.0
Q_BLOCK = 128
M_HEADS = 4
M_HEAD_DIM = 128
M_WIDTH = M_HEADS * M_HEAD_DIM
M_CHUNK = 64
N_GATE_PRE = 4 * M_HEADS
D_FF = 2816
CONV_W = 3
EPS = 1e-6
IN_SIZES = (ATTN_WIDTH, KV_WIDTH, KV_WIDTH,
            M_WIDTH, M_WIDTH, M_WIDTH, M_WIDTH,
            N_GATE_PRE,
            D_MODEL, D_MODEL)
D_IN = ATTN_WIDTH + 2 * KV_WIDTH + 4 * M_WIDTH + N_GATE_PRE + 2 * D_MODEL

kernel_name = "hybrid_axial_gqa_mlstm_convffn_encoder"


def rmsnorm(x, g):
    xf = x.astype(jnp.float32)
    y = xf * lax.rsqrt(jnp.mean(xf * xf, axis=-1, keepdims=True) + EPS) * g.astype(jnp.float32)
    return y.astype(x.dtype)


def axial_angles(S):
    rows = S // GRID_W
    r = jnp.repeat(jnp.arange(rows, dtype=jnp.float32), GRID_W)
    c = jnp.tile(jnp.arange(GRID_W, dtype=jnp.float32), rows)
    n_pairs_axis = HEAD_DIM // 4
    freq = ROPE_THETA ** (-jnp.arange(n_pairs_axis, dtype=jnp.float32) / n_pairs_axis)
    ang = jnp.concatenate([r[:, None] * freq, c[:, None] * freq], axis=-1)
    return jnp.cos(ang), jnp.sin(ang)


def apply_rope(x, cos, sin):
    B, S, H, dh = x.shape
    xf = x.astype(jnp.float32).reshape(B, S, H, dh // 2, 2)
    x0, x1 = xf[..., 0], xf[..., 1]
    c, s = cos[None, :, None, :], sin[None, :, None, :]
    out = jnp.stack([x0 * c - x1 * s, x0 * s + x1 * c], axis=-1)
    return out.reshape(B, S, H, dh).astype(x.dtype)


def axial_gqa(q, k, v, q_norm, k_norm):
    B, S, _ = q.shape
    q = rmsnorm(q.reshape(B, S, N_Q_HEADS, HEAD_DIM), q_norm)
    k = rmsnorm(k.reshape(B, S, N_KV_HEADS, HEAD_DIM), k_norm)
    v = v.reshape(B, S, N_KV_HEADS, HEAD_DIM)
    cos, sin = axial_angles(S)
    q = apply_rope(q, cos, sin)
    k = apply_rope(k, cos, sin)
    q = q.reshape(B, S, N_KV_HEADS, GQA_GROUP, HEAD_DIM).transpose(0, 2, 3, 1, 4) * (HEAD_DIM ** -0.5)
    k = k.transpose(0, 2, 1, 3)
    v = v.transpose(0, 2, 1, 3)
    nb = S // Q_BLOCK
    qb = q.reshape(B, N_KV_HEADS, GQA_GROUP, nb, Q_BLOCK, HEAD_DIM).transpose(3, 0, 1, 2, 4, 5)

    def block(qblk):
        s = jnp.einsum('bkgqd,bksd->bkgqs', qblk, k).astype(jnp.float32)
        p = jax.nn.softmax(s, axis=-1).astype(v.dtype)
        return jnp.einsum('bkgqs,bksd->bkgqd', p, v)

    o = lax.map(block, qb)
    return o.transpose(1, 0, 4, 2, 3, 5).reshape(B, S, ATTN_WIDTH)


def mlstm_scan(q, k, v, li, lf):
    B, H, S, dk = q.shape
    dv = v.shape[-1]
    L = M_CHUNK
    N = S // L

    def chunks(a):
        return jnp.moveaxis(a.reshape((B, H, N, L) + a.shape[3:]), 2, 0)

    tri = jnp.tril(jnp.ones((L, L), dtype=bool))

    def step(carry, inp):
        C, n, m = carry
        qc, kc, vc, ic, fc = inp
        b = jnp.cumsum(fc, axis=-1)
        Dm = jnp.where(tri, b[..., :, None] - b[..., None, :] + ic[..., None, :], -jnp.inf)
        inter = b + m[..., None]
        mt = jnp.maximum(jnp.max(Dm, axis=-1), inter)
        W = jnp.exp(Dm - mt[..., None]) * jnp.einsum('bhtd,bhsd->bhts', qc, kc)
        dec = jnp.exp(inter - mt)
        num = jnp.einsum('bhts,bhse->bhte', W, vc) + dec[..., None] * jnp.einsum('bhtd,bhde->bhte', qc, C)
        den = jnp.sum(W, axis=-1) + dec * jnp.einsum('bhtd,bhd->bht', qc, n)
        h = num / jnp.maximum(jnp.abs(den), jnp.exp(-mt))[..., None]
        bL = b[..., -1]
        ws = bL[..., None] - b + ic
        m_new = jnp.maximum(bL + m, jnp.max(ws, axis=-1))
        a = jnp.exp(bL + m - m_new)
        w = jnp.exp(ws - m_new[..., None])
        C = a[..., None, None] * C + jnp.einsum('bhs,bhsd,bhse->bhde', w, kc, vc)
        n = a[..., None] * n + jnp.einsum('bhs,bhsd->bhd', w, kc)
        return (C, n, m_new), h

    init = (jnp.zeros((B, H, dk, dv), jnp.float32), jnp.zeros((B, H, dk), jnp.float32),
            jnp.zeros((B, H), jnp.float32))
    _, hs = lax.scan(step, init, (chunks(q), chunks(k), chunks(v), chunks(li), chunks(lf)))
    return jnp.moveaxis(hs, 0, 2).reshape(B, H, S, dv)


def bidir_mlstm(q, k, v, o, gate_pre, b_if, mlstm_norm):
    B, S, _ = q.shape

    def heads(a):
        return a.astype(jnp.float32).reshape(B, S, M_HEADS, M_HEAD_DIM).transpose(0, 2, 1, 3)

    qh, kh, vh = heads(q), heads(k) * (M_HEAD_DIM ** -0.5), heads(v)
    g = (gate_pre.astype(jnp.float32) + b_if.astype(jnp.float32)).reshape(B, S, 4, M_HEADS)
    g = g.transpose(2, 0, 3, 1)
    i_f, lf_f = g[0], jax.nn.log_sigmoid(g[1])
    i_b, lf_b = g[2], jax.nn.log_sigmoid(g[3])
    h_f = mlstm_scan(qh, kh, vh, i_f, lf_f)
    flip = lambda a: jnp.flip(a, axis=2)
    h_b = flip(mlstm_scan(flip(qh), flip(kh), flip(vh), flip(i_b), flip(lf_b)))
    h = (h_f + h_b).transpose(0, 2, 1, 3)
    h = rmsnorm(h, mlstm_norm.reshape(M_HEADS, M_HEAD_DIM)).reshape(B, S, M_WIDTH)
    return (h * jax.nn.sigmoid(o.astype(jnp.float32))).astype(q.dtype)


def dwconv_centred(g, w, b):
    S = g.shape[1]
    gp = jnp.pad(g, ((0, 0), (CONV_W // 2, CONV_W // 2), (0, 0)))
    out = gp[:, 0:S] * w[0]
    for j in range(1, CONV_W):
        out = out + gp[:, j:j + S] * w[j]
    return out + b


def block_layer(x, norm_mix_pre, w_in, b_if, q_norm, k_norm, mlstm_norm, w_attn_o, w_mlstm_o,
                w_out, norm_mix_post, norm_ffn_pre, w_up, conv_w, conv_b, w_down, norm_ffn_post):
    h = rmsnorm(x, norm_mix_pre)
    z = h @ w_in
    idx = [int(i) for i in np.cumsum(IN_SIZES)[:-1]]
    aq, ak, av, mq, mk, mv, mo, gate_pre, ga, gm = jnp.split(z, idx, axis=-1)
    a_out = axial_gqa(aq, ak, av, q_norm, k_norm) @ w_attn_o
    m_out = bidir_mlstm(mq, mk, mv, mo, gate_pre, b_if, mlstm_norm) @ w_mlstm_o
    merged = jax.nn.sigmoid(ga) * a_out + jax.nn.sigmoid(gm) * m_out
    x = x + rmsnorm(merged @ w_out, norm_mix_post)
    h = rmsnorm(x, norm_ffn_pre)
    u = h @ w_up
    gate, val = u[..., :D_FF], u[..., D_FF:]
    gate = dwconv_centred(gate, conv_w, conv_b)
    y = (jax.nn.gelu(gate, approximate=True) * val) @ w_down
    return x + rmsnorm(y, norm_ffn_post)


def setup_inputs(seed: int = 0) -> dict:
    key = jax.random.key(seed)
    ks = jax.random.split(key, 24)
    f32 = jnp.float32

    def w(k, shape, fan_in):
        return jax.random.normal(k, shape, f32) * (fan_in ** -0.5)

    def gain(k, shape):
        return 1.0 + 0.05 * jax.random.normal(k, shape, f32)

    kb = jax.random.split(ks[17], 4)
    fbias = jnp.linspace(3.0, 6.0, M_HEADS, dtype=f32)
    b_if = jnp.concatenate([
        0.1 * jax.random.normal(kb[0], (M_HEADS,), f32),
        fbias + 0.1 * jax.random.normal(kb[1], (M_HEADS,), f32),
        0.1 * jax.random.normal(kb[2], (M_HEADS,), f32),
        fbias + 0.1 * jax.random.normal(kb[3], (M_HEADS,), f32)])
    b_if = jnp.tile(b_if[None], (DEPTH, 1))
    return {
        "x_prompt": jax.random.normal(ks[0], (BATCH, SEQ, D_MODEL), f32),
        "x_sample": jax.random.normal(ks[1], (DEC_BATCH, DEC_SEQ, D_MODEL), f32),
        "norm_mix_pre": gain(ks[2], (DEPTH, D_MODEL)),
        "w_in": w(ks[3], (DEPTH, D_MODEL, D_IN), D_MODEL),
        "b_if": b_if,
        "q_norm": gain(ks[4], (DEPTH, HEAD_DIM)),
        "k_norm": gain(ks[5], (DEPTH, HEAD_DIM)),
        "mlstm_norm": gain(ks[6], (DEPTH, M_WIDTH)),
        "w_attn_o": w(ks[7], (DEPTH, ATTN_WIDTH, D_MODEL), ATTN_WIDTH),
        "w_mlstm_o": w(ks[8], (DEPTH, M_WIDTH, D_MODEL), M_WIDTH),
        "w_out": w(ks[9], (DEPTH, D_MODEL, D_MODEL), D_MODEL),
        "norm_mix_post": gain(ks[10], (DEPTH, D_MODEL)),
        "norm_ffn_pre": gain(ks[11], (DEPTH, D_MODEL)),
        "w_up": w(ks[12], (DEPTH, D_MODEL, 2 * D_FF), D_MODEL),
        "conv_w": w(ks[13], (DEPTH, CONV_W, D_FF), CONV_W),
        "conv_b": 0.01 * jax.random.normal(ks[14], (DEPTH, D_FF), f32),
        "w_down": w(ks[15], (DEPTH, D_FF, D_MODEL), D_FF),
        "norm_ffn_post": gain(ks[16], (DEPTH, D_MODEL)),
    }


def reference(x_prompt, x_sample, norm_mix_pre, w_in, b_if, q_norm, k_norm, mlstm_norm,
              w_attn_o, w_mlstm_o, w_out, norm_mix_post, norm_ffn_pre, w_up, conv_w, conv_b,
              w_down, norm_ffn_post):
    def trunk(x):
        for l in range(DEPTH):
            x = block_layer(x, norm_mix_pre[l], w_in[l], b_if[l], q_norm[l], k_norm[l],
                            mlstm_norm[l], w_attn_o[l], w_mlstm_o[l], w_out[l], norm_mix_post[l],
                            norm_ffn_pre[l], w_up[l], conv_w[l], conv_b[l], w_down[l],
                            norm_ffn_post[l])
        return x

    y_prompt = trunk(x_prompt)
    y_sample = trunk(x_sample)
    return (y_prompt, y_sample)
```

```python
import functools

import numpy as np
import jax
import jax.numpy as jnp
from jax import lax
from jax.experimental import pallas as pl
from jax.experimental.pallas import tpu as pltpu

D_MODEL = 1024
GRID_W = 64
N_Q_HEADS = 8
N_KV_HEADS = 2
GQA_GROUP = N_Q_HEADS // N_KV_HEADS
HEAD_DIM = 64
ATTN_WIDTH = N_Q_HEADS * HEAD_DIM
KV_WIDTH = N_KV_HEADS * HEAD_DIM
ROPE_THETA = 10000.0
M_HEADS = 4
M_HEAD_DIM = 128
M_WIDTH = M_HEADS * M_HEAD_DIM
N_GATE_PRE = 4 * M_HEADS
D_FF = 2816
CONV_W = 3
EPS = 1e-6

LANES = 128
MXU_DIM = 256
VMEM_LIMIT = 56 * 1024 * 1024

TOKEN_TILE = 512
ATTN_Q_TILE = 256
ATTN_KV_CHUNK = 512
V_ROWS = HEAD_DIM + 16
M_CHUNK = 128
FF_CHUNK = 256
HALO = 16

BF16 = jnp.bfloat16
F32 = jnp.float32


def _sigmoid(x):
    return 0.5 * jnp.tanh(0.5 * x) + 0.5


def _const_spec(shape):
    n = len(shape)
    return pl.BlockSpec(shape, lambda *_: (0,) * n, pipeline_mode=pl.Buffered(1))


def _in_proj_kernel(x_ref, g_ref, wa_ref, wm_ref, wg_ref, ws_ref, bd_ref,
                    tqa_ref, tqb_ref, tka_ref, tkb_ref,
                    q_ref, k_ref, v_ref, m_ref, gate_ref, sg_ref):
    x = x_ref[...]
    ms = jnp.mean(x * x, axis=-1, keepdims=True)
    h = (x * lax.rsqrt(ms + EPS) * g_ref[...]).astype(BF16)

    za = jnp.dot(h, wa_ref[...], preferred_element_type=F32)
    bd = bd_ref[...]
    lane = lax.broadcasted_iota(jnp.int32, (x.shape[0], LANES), 1)
    first_half = (lane % HEAD_DIM) < (HEAD_DIM // 2)

    def head_norm_rope(z, msq, ta, tb):
        sw = jnp.where(first_half, pltpu.roll(z, LANES - HEAD_DIM // 2, 1), pltpu.roll(z, HEAD_DIM // 2, 1))
        return lax.rsqrt(msq + EPS) * (z * ta + sw * tb)

    tqa, tqb = tqa_ref[...], tqb_ref[...]
    for half in range(ATTN_WIDTH // MXU_DIM):
        zq = za[:, half * MXU_DIM:(half + 1) * MXU_DIM]
        msq = jnp.dot((zq * zq).astype(BF16), bd, preferred_element_type=F32)
        for j in range(MXU_DIM // LANES):
            c0 = half * MXU_DIM + j * LANES
            out = head_norm_rope(zq[:, j * LANES:(j + 1) * LANES], msq[:, j * LANES:(j + 1) * LANES], tqa, tqb)
            q_ref[:, c0:c0 + LANES] = out.astype(BF16)
    zk = za[:, ATTN_WIDTH:ATTN_WIDTH + KV_WIDTH]
    msk = jnp.dot((zk * zk).astype(BF16), bd[:LANES, :LANES], preferred_element_type=F32)
    k_ref[...] = head_norm_rope(zk, msk, tka_ref[...], tkb_ref[...]).astype(BF16)
    v_ref[...] = za[:, ATTN_WIDTH + KV_WIDTH:].astype(BF16)

    zm = jnp.dot(h, wm_ref[...], preferred_element_type=F32)
    m_ref[:, 0:M_WIDTH] = zm[:, 0:M_WIDTH].astype(BF16)
    m_ref[:, M_WIDTH:2 * M_WIDTH] = (zm[:, M_WIDTH:2 * M_WIDTH] * (M_HEAD_DIM ** -0.5)).astype(BF16)
    m_ref[:, 2 * M_WIDTH:3 * M_WIDTH] = zm[:, 2 * M_WIDTH:3 * M_WIDTH].astype(BF16)
    m_ref[:, 3 * M_WIDTH:] = _sigmoid(zm[:, 3 * M_WIDTH:]).astype(BF16)

    zg = jnp.dot(h, wg_ref[...], preferred_element_type=F32)
    gate_ref[...] = zg[:, :N_GATE_PRE]

    zs = jnp.dot(h, ws_ref[...], preferred_element_type=F32)
    sg_ref[...] = _sigmoid(zs).astype(BF16)


def _in_proj(x2, p, S):
    T = x2.shape[0]
    tm = TOKEN_TILE
    spt = S // tm
    row = lambda i: (i, 0)
    pos = lambda i: (i % spt, 0)
    tab = pl.BlockSpec((tm, LANES), pos)
    return pl.pallas_call(
        _in_proj_kernel,
        grid=(T // tm,),
        in_specs=[
            pl.BlockSpec((tm, D_MODEL), row),
            _const_spec((1, D_MODEL)),
            _const_spec(p["wa"].shape), _const_spec(p["wm"].shape),
            _const_spec(p["wg"].shape), _const_spec(p["ws"].shape),
            _const_spec((MXU_DIM, MXU_DIM)),
            tab, tab, tab, tab,
        ],
        out_specs=[
            pl.BlockSpec((tm, ATTN_WIDTH), row),
            pl.BlockSpec((tm, KV_WIDTH), row),
            pl.BlockSpec((tm, KV_WIDTH), row),
            pl.BlockSpec((tm, 4 * M_WIDTH), row),
            pl.BlockSpec((tm, N_GATE_PRE), row),
            pl.BlockSpec((tm, 2 * D_MODEL), row),
        ],
        out_shape=[
            jax.ShapeDtypeStruct((T, ATTN_WIDTH), BF16),
            jax.ShapeDtypeStruct((T, KV_WIDTH), BF16),
            jax.ShapeDtypeStruct((T, KV_WIDTH), BF16),
            jax.ShapeDtypeStruct((T, 4 * M_WIDTH), BF16),
            jax.ShapeDtypeStruct((T, N_GATE_PRE), F32),
            jax.ShapeDtypeStruct((T, 2 * D_MODEL), BF16),
        ],
        compiler_params=pltpu.CompilerParams(
            dimension_semantics=("parallel",), vmem_limit_bytes=VMEM_LIMIT),
        name="in_proj",
    )(x2, p["g_mix_pre"], p["wa"], p["wm"], p["wg"], p["ws"], p["bd"],
      p["tqa"][S], p["tqb"][S], p["tka"][S], p["tkb"][S])


def _attn_kernel(q_ref, k_ref, vt_ref, o_ref, w_ref, acc_ref, *, n_chunks, tq, kc):
    lane = lax.broadcasted_iota(jnp.int32, (tq, LANES), 1)
    low = lane < HEAD_DIM
    zero = jnp.zeros((tq, LANES), BF16)
    for j in range(GQA_GROUP):
        qj = q_ref[:, j * LANES:(j + 1) * LANES]
        w_ref[(2 * j) * tq:(2 * j + 1) * tq, :] = jnp.where(low, qj, zero)
        w_ref[(2 * j + 1) * tq:(2 * j + 2) * tq, :] = jnp.where(low, zero, qj)
    acc_ref[...] = jnp.zeros(acc_ref.shape, F32)

    def body(c, m):
        kch = k_ref[pl.ds(pl.multiple_of(c * kc, kc), kc), :]
        s = lax.dot_general(kch, w_ref[...], (((1,), (1,)), ((), ())),
                            preferred_element_type=F32)
        mn = jnp.maximum(m, jnp.max(s, axis=0, keepdims=True))
        alpha = jnp.exp(m - mn)
        pr = jnp.exp(s - mn).astype(BF16)
        for slot in range(N_Q_HEADS):
            vt = vt_ref[slot % N_KV_HEADS, c]
            pv = jnp.dot(vt, pr[:, slot * tq:(slot + 1) * tq], preferred_element_type=F32)
            acc_ref[slot] = acc_ref[slot] * alpha[:, slot * tq:(slot + 1) * tq] + pv
        return mn

    lax.fori_loop(0, n_chunks, body, jnp.full((1, N_Q_HEADS * tq), -jnp.inf, F32))

    outs = []
    for slot in range(N_Q_HEADS):
        a = acc_ref[slot]
        outs.append(a[:HEAD_DIM] / a[HEAD_DIM:HEAD_DIM + 1])
    o_ref[...] = jnp.concatenate(outs, axis=0).T.astype(BF16)


def _attention(q, k, vt, B, S):
    T = q.shape[0]
    tq, kc = ATTN_Q_TILE, ATTN_KV_CHUNK
    nq, n_chunks = S // tq, S // kc
    kern = functools.partial(_attn_kernel, n_chunks=n_chunks, tq=tq, kc=kc)
    return pl.pallas_call(
        kern,
        grid=(B, nq),
        in_specs=[
            pl.BlockSpec((tq, ATTN_WIDTH), lambda b, i: (b * nq + i, 0)),
            pl.BlockSpec((S, KV_WIDTH), lambda b, i: (b, 0)),
            pl.BlockSpec((None, N_KV_HEADS, n_chunks, V_ROWS, kc), lambda b, i: (b, 0, 0, 0, 0)),
        ],
        out_specs=pl.BlockSpec((tq, ATTN_WIDTH), lambda b, i: (b * nq + i, 0)),
        out_shape=jax.ShapeDtypeStruct((T, ATTN_WIDTH), BF16),
        scratch_shapes=[
            pltpu.VMEM((N_Q_HEADS * tq, LANES), BF16),
            pltpu.VMEM((N_Q_HEADS, V_ROWS, tq), F32),
        ],
        compiler_params=pltpu.CompilerParams(
            dimension_semantics=("parallel", "arbitrary"), vmem_limit_bytes=VMEM_LIMIT),
        name="attn",
    )(q, k, vt)


def _mlstm_kernel(qf_ref, kf_ref, vf_ref, qb_ref, kb_ref, vb_ref,
                  gcf_ref, gcb_ref, grf_ref, grb_ref, bc_ref, br_ref,
                  hf_ref, hb_ref, c_ref, m_ref):
    L = qf_ref.shape[0]

    @pl.when(pl.program_id(1) == 0)
    def _():
        c_ref[...] = jnp.zeros(c_ref.shape, F32)
        m_ref[...] = jnp.zeros(m_ref.shape, F32)

    ri = lax.broadcasted_iota(jnp.int32, (L, L), 0)
    ci = lax.broadcasted_iota(jnp.int32, (L, L), 1)
    lower = ci <= ri
    upper = ci >= ri
    lower_f = lower.astype(F32)
    upper_f = upper.astype(F32)
    ones = jnp.ones((L, M_HEAD_DIM), BF16)
    hp = lax.Precision.HIGHEST

    for d, (q_ref, k_ref, v_ref, gc_ref, gr_ref, h_ref) in enumerate(
            ((qf_ref, kf_ref, vf_ref, gcf_ref, grf_ref, hf_ref),
             (qb_ref, kb_ref, vb_ref, gcb_ref, grb_ref, hb_ref))):
        gcol = gc_ref[...] + bc_ref[...]
        grow = gr_ref[...] + br_ref[...]
        lf_col = jax.nn.log_sigmoid(gcol)
        lf_row = jax.nn.log_sigmoid(grow)
        if d == 0:
            cum_col = jnp.dot(lower_f, lf_col, precision=hp, preferred_element_type=F32)
            cum_row = jnp.dot(lf_row, upper_f, precision=hp, preferred_element_type=F32)
            mask = lower
        else:
            cum_col = jnp.dot(upper_f, lf_col, precision=hp, preferred_element_type=F32)
            cum_row = jnp.dot(lf_row, lower_f, precision=hp, preferred_element_type=F32)
            mask = upper
        for hd in range(M_HEADS):
            st = d * M_HEADS + hd
            gi = 2 * d * M_HEADS + hd
            gf = gi + M_HEADS
            b_col = cum_col[:, gf:gf + 1]
            b_row = cum_row[gf:gf + 1, :]
            r_row = grow[gi:gi + 1, :] - b_row
            b_tot = b_row[:, L - 1:L] if d == 0 else b_row[:, 0:1]
            m_old = m_ref[st:st + 1, 0:1]

            q = q_ref[:, hd * M_HEAD_DIM:(hd + 1) * M_HEAD_DIM]
            k = k_ref[:, hd * M_HEAD_DIM:(hd + 1) * M_HEAD_DIM]
            v_ext = jnp.concatenate([v_ref[:, hd * M_HEAD_DIM:(hd + 1) * M_HEAD_DIM], ones], axis=1)
            c_old = c_ref[st]

            dm = jnp.where(mask, b_col + r_row, -jnp.inf)
            inter = b_col + m_old
            mt = jnp.maximum(jnp.max(dm, axis=1, keepdims=True), inter)
            qk = lax.dot_general(q, k, (((1,), (1,)), ((), ())), preferred_element_type=F32)
            w = (jnp.exp(dm - mt) * qk).astype(BF16)
            dec = jnp.exp(inter - mt)
            ext = (jnp.dot(w, v_ext, preferred_element_type=F32)
                   + dec * jnp.dot(q, c_old.astype(BF16), preferred_element_type=F32))
            num, den = ext[:, :M_HEAD_DIM], ext[:, M_HEAD_DIM:]
            h = num / jnp.maximum(jnp.abs(den), jnp.exp(-mt))
            h_ref[:, hd * M_HEAD_DIM:(hd + 1) * M_HEAD_DIM] = h.astype(h_ref.dtype)

            ws_row = b_tot + r_row
            m_new = jnp.maximum(b_tot + m_old, jnp.max(ws_row, axis=1, keepdims=True))
            a = jnp.exp(b_tot + m_old - m_new)
            w_row = jnp.exp(ws_row - m_new)
            kw = (k.astype(F32).T * w_row).astype(BF16)
            c_ref[st] = a * c_old + jnp.dot(kw, v_ext, preferred_element_type=F32)
            m_ref[st:st + 1, :] = jnp.broadcast_to(m_new, (1, LANES))


def _mlstm(m4, gate, gate_t, p, B, S):
    T = m4.shape[0]
    L = M_CHUNK
    nc = S // L
    fwd = lambda col: (lambda b, c: (b * nc + c, col))
    bwd = lambda col: (lambda b, c: (b * nc + nc - 1 - c, col))
    blk = lambda f: pl.BlockSpec((L, M_WIDTH), f)
    return pl.pallas_call(
        _mlstm_kernel,
        grid=(B, nc),
        in_specs=[
            blk(fwd(0)), blk(fwd(1)), blk(fwd(2)),
            blk(bwd(0)), blk(bwd(1)), blk(bwd(2)),
            pl.BlockSpec((L, N_GATE_PRE), fwd(0)),
            pl.BlockSpec((L, N_GATE_PRE), bwd(0)),
            pl.BlockSpec((N_GATE_PRE, L), lambda b, c: (0, b * nc + c)),
            pl.BlockSpec((N_GATE_PRE, L), lambda b, c: (0, b * nc + nc - 1 - c)),
            _const_spec((1, N_GATE_PRE)), _const_spec((N_GATE_PRE, 1)),
        ],
        out_specs=[blk(fwd(0)), blk(bwd(0))],
        out_shape=[jax.ShapeDtypeStruct((T, M_WIDTH), BF16)] * 2,
        scratch_shapes=[
            pltpu.VMEM((2 * M_HEADS, M_HEAD_DIM, 2 * M_HEAD_DIM), F32),
            pltpu.VMEM((2 * M_HEADS, LANES), F32),
        ],
        compiler_params=pltpu.CompilerParams(
            dimension_semantics=("parallel", "arbitrary"), vmem_limit_bytes=VMEM_LIMIT),
        name="mlstm",
    )(m4, m4, m4, m4, m4, m4, gate, gate, gate_t, gate_t, p["b_if_row"], p["b_if_col"])


def _merge_kernel(a_ref, hf_ref, hb_ref, so_ref, sg_ref, x_ref, gm_ref, wao_ref, wmo_ref, wout_ref, gp_ref, o_ref):
    hs = hf_ref[...].astype(F32) + hb_ref[...].astype(F32)
    parts = []
    for hd in range(M_HEADS):
        blk = hs[:, hd * M_HEAD_DIM:(hd + 1) * M_HEAD_DIM]
        ms = jnp.mean(blk * blk, axis=-1, keepdims=True)
        parts.append(blk * lax.rsqrt(ms + EPS))
    hn = jnp.concatenate(parts, axis=1) * gm_ref[...]
    hm = (hn * so_ref[...].astype(F32)).astype(BF16)
    a_out = jnp.dot(a_ref[...], wao_ref[...], preferred_element_type=F32)
    m_out = jnp.dot(hm, wmo_ref[...], preferred_element_type=F32)
    sg = sg_ref[...]
    merged = sg[:, :D_MODEL].astype(F32) * a_out + sg[:, D_MODEL:].astype(F32) * m_out
    y = jnp.dot(merged.astype(BF16), wout_ref[...], preferred_element_type=F32)
    ms = jnp.mean(y * y, axis=-1, keepdims=True)
    o_ref[...] = x_ref[...] + y * lax.rsqrt(ms + EPS) * gp_ref[...]


def _merge(a, hf, hb, m4, sg, x2, p):
    T = x2.shape[0]
    tm = TOKEN_TILE
    row = lambda i: (i, 0)
    return pl.pallas_call(
        _merge_kernel,
        grid=(T // tm,),
        in_specs=[
            pl.BlockSpec((tm, ATTN_WIDTH), row),
            pl.BlockSpec((tm, M_WIDTH), row),
            pl.BlockSpec((tm, M_WIDTH), row),
            pl.BlockSpec((tm, M_WIDTH), lambda i: (i, 3)),
            pl.BlockSpec((tm, 2 * D_MODEL), row),
            pl.BlockSpec((tm, D_MODEL), row),
            _const_spec((1, M_WIDTH)),
            _const_spec((ATTN_WIDTH, D_MODEL)), _const_spec((M_WIDTH, D_MODEL)),
            _const_spec((D_MODEL, D_MODEL)), _const_spec((1, D_MODEL)),
        ],
        out_specs=pl.BlockSpec((tm, D_MODEL), row),
        out_shape=jax.ShapeDtypeStruct((T, D_MODEL), F32),
        compiler_params=pltpu.CompilerParams(
            dimension_semantics=("parallel",), vmem_limit_bytes=VMEM_LIMIT),
        name="merge",
    )(a, hf, hb, m4, sg, x2, p["g_mlstm"], p["wao"], p["wmo"], p["wout"], p["g_mix_post"])


def _ffn_kernel(xp_ref, x_ref, xn_ref, g_ref, wg_ref, wv_ref, cw_ref, cb_ref, wd_ref, gp_ref, o_ref,
                h_ref, ge_ref, acc_ref, *, tiles_per_seq):
    tm = x_ref.shape[0]
    g = g_ref[...]

    def norm(x):
        ms = jnp.mean(x * x, axis=-1, keepdims=True)
        return (x * lax.rsqrt(ms + EPS) * g).astype(BF16)

    h_ref[0:HALO, :] = norm(xp_ref[...])
    h_ref[HALO:HALO + tm, :] = norm(x_ref[...])
    h_ref[HALO + tm:, :] = norm(xn_ref[...])
    pos = pl.program_id(0) % tiles_per_seq
    keep_prev = (pos != 0).astype(F32)
    keep_next = (pos != tiles_per_seq - 1).astype(F32)
    acc_ref[...] = jnp.zeros(acc_ref.shape, F32)

    def body(c, carry):
        ge_ref[...] = jnp.dot(h_ref[...], wg_ref[c], preferred_element_type=F32)
        ge_ref[HALO - 1:HALO, :] = ge_ref[HALO - 1:HALO, :] * keep_prev
        ge_ref[HALO + tm:HALO + tm + 1, :] = ge_ref[HALO + tm:HALO + tm + 1, :] * keep_next
        val = jnp.dot(h_ref[HALO:HALO + tm, :], wv_ref[c], preferred_element_type=F32)
        cw = cw_ref[c]
        conv = (ge_ref[HALO - 1:HALO - 1 + tm, :] * cw[0:1]
                + ge_ref[HALO:HALO + tm, :] * cw[1:2]
                + ge_ref[HALO + 1:HALO + 1 + tm, :] * cw[2:3]
                + cb_ref[c])
        act = (jax.nn.gelu(conv, approximate=True) * val).astype(BF16)
        acc_ref[...] += jnp.dot(act, wd_ref[c], preferred_element_type=F32)
        return carry

    lax.fori_loop(0, D_FF // FF_CHUNK, body, 0)
    y = acc_ref[...]
    ms = jnp.mean(y * y, axis=-1, keepdims=True)
    o_ref[...] = x_ref[...] + y * lax.rsqrt(ms + EPS) * gp_ref[...]


def _ffn(x1, p, S):
    T = x1.shape[0]
    tm = TOKEN_TILE
    hpt = tm // HALO
    n_halo = T // HALO
    nch = D_FF // FF_CHUNK
    kern = functools.partial(_ffn_kernel, tiles_per_seq=S // tm)
    return pl.pallas_call(
        kern,
        grid=(T // tm,),
        in_specs=[
            pl.BlockSpec((HALO, D_MODEL), lambda i: (jnp.maximum(i * hpt - 1, 0), 0)),
            pl.BlockSpec((tm, D_MODEL), lambda i: (i, 0)),
            pl.BlockSpec((HALO, D_MODEL), lambda i: (jnp.minimum((i + 1) * hpt, n_halo - 1), 0)),
            _const_spec((1, D_MODEL)),
            _const_spec((nch, D_MODEL, FF_CHUNK)), _const_spec((nch, D_MODEL, FF_CHUNK)),
            _const_spec((nch, 8, FF_CHUNK)), _const_spec((nch, 1, FF_CHUNK)),
            _const_spec((nch, FF_CHUNK, D_MODEL)), _const_spec((1, D_MODEL)),
        ],
        out_specs=pl.BlockSpec((tm, D_MODEL), lambda i: (i, 0)),
        out_shape=jax.ShapeDtypeStruct((T, D_MODEL), F32),
        scratch_shapes=[
            pltpu.VMEM((tm + 2 * HALO, D_MODEL), BF16),
            pltpu.VMEM((tm + 2 * HALO, FF_CHUNK), F32),
            pltpu.VMEM((tm, D_MODEL), F32),
        ],
        compiler_params=pltpu.CompilerParams(
            dimension_semantics=("parallel",), vmem_limit_bytes=VMEM_LIMIT),
        name="ffn",
    )(x1, x1, x1, p["g_ffn_pre"], p["wup_g"], p["wup_v"], p["conv_w"], p["conv_b"], p["wdown"], p["g_ffn_post"])


def _rope_tables(S, q_gain, k_gain):
    rows = S // GRID_W
    r = jnp.repeat(jnp.arange(rows, dtype=F32), GRID_W)
    c = jnp.tile(jnp.arange(GRID_W, dtype=F32), rows)
    n_pairs_axis = HEAD_DIM // 4
    freq = ROPE_THETA ** (-jnp.arange(n_pairs_axis, dtype=F32) / n_pairs_axis)
    ang = jnp.concatenate([r[:, None] * freq, c[:, None] * freq], axis=-1)
    cos, sin = jnp.cos(ang), jnp.sin(ang)
    cc = jnp.concatenate([cos, cos], axis=-1)
    ss = jnp.concatenate([-sin, sin], axis=-1)

    def tables(gain, scale):
        g_half = jnp.concatenate([gain[0::2], gain[1::2]])
        g_swap = jnp.concatenate([gain[1::2], gain[0::2]])
        ta = cc * g_half * scale
        tb = ss * g_swap * scale
        return jnp.tile(ta, (1, LANES // HEAD_DIM)), jnp.tile(tb, (1, LANES // HEAD_DIM))

    tqa, tqb = tables(q_gain, HEAD_DIM ** -0.5)
    tka, tkb = tables(k_gain, 1.0)
    return tqa, tqb, tka, tkb


def _prepare(norm_mix_pre, w_in, b_if, q_norm, k_norm, mlstm_norm, w_attn_o, w_mlstm_o, w_out,
             norm_mix_post, norm_ffn_pre, w_up, conv_w, conv_b, w_down, norm_ffn_post, seq_lens):
    half = np.concatenate([np.arange(0, HEAD_DIM, 2), np.arange(1, HEAD_DIM, 2)])
    slots = [h for j in range(GQA_GROUP) for h in (j, j + GQA_GROUP)]
    q_cols = np.concatenate([h * HEAD_DIM + half for h in slots])
    k_cols = ATTN_WIDTH + np.concatenate([h * HEAD_DIM + half for h in range(N_KV_HEADS)])
    v_cols = ATTN_WIDTH + KV_WIDTH + np.arange(KV_WIDTH)
    o_rows = np.concatenate([h * HEAD_DIM + np.arange(HEAD_DIM) for h in slots])
    m0 = ATTN_WIDTH + 2 * KV_WIDTH
    g0 = m0 + 4 * M_WIDTH
    s0 = g0 + N_GATE_PRE

    p = {}
    p["wa"] = w_in[:, np.concatenate([q_cols, k_cols, v_cols])].astype(BF16)
    p["wm"] = w_in[:, m0:g0].astype(BF16)
    p["wg"] = jnp.pad(w_in[:, g0:s0], ((0, 0), (0, LANES - N_GATE_PRE))).astype(BF16)
    p["ws"] = w_in[:, s0:].astype(BF16)
    blk = np.arange(MXU_DIM) // HEAD_DIM
    p["bd"] = jnp.asarray((blk[:, None] == blk[None, :]).astype(np.float32) / HEAD_DIM, BF16)
    p["g_mix_pre"] = norm_mix_pre.reshape(1, D_MODEL)
    p["b_if_row"] = b_if.reshape(1, N_GATE_PRE)
    p["b_if_col"] = b_if.reshape(N_GATE_PRE, 1)
    p["g_mlstm"] = mlstm_norm.reshape(1, M_WIDTH)
    p["wao"] = w_attn_o[o_rows].astype(BF16)
    p["wmo"] = w_mlstm_o.astype(BF16)
    p["wout"] = w_out.astype(BF16)
    p["g_mix_post"] = norm_mix_post.reshape(1, D_MODEL)
    p["g_ffn_pre"] = norm_ffn_pre.reshape(1, D_MODEL)
    nch = D_FF // FF_CHUNK
    chunked = lambda w: w.reshape(D_MODEL, nch, FF_CHUNK).transpose(1, 0, 2).astype(BF16)
    p["wup_g"] = chunked(w_up[:, :D_FF])
    p["wup_v"] = chunked(w_up[:, D_FF:])
    p["conv_w"] = jnp.pad(conv_w, ((0, 8 - CONV_W), (0, 0))).reshape(8, nch, FF_CHUNK).transpose(1, 0, 2)
    p["conv_b"] = conv_b.reshape(nch, 1, FF_CHUNK)
    p["wdown"] = w_down.reshape(nch, FF_CHUNK, D_MODEL).astype(BF16)
    p["g_ffn_post"] = norm_ffn_post.reshape(1, D_MODEL)
    p["tqa"], p["tqb"], p["tka"], p["tkb"] = {}, {}, {}, {}
    for S in seq_lens:
        p["tqa"][S], p["tqb"][S], p["tka"][S], p["tkb"][S] = _rope_tables(S, q_norm, k_norm)
    return p


def _trunk(x, p):
    B, S, _ = x.shape
    T = B * S
    x2 = x.reshape(T, D_MODEL)
    q, k, v, m4, gate, sg = _in_proj(x2, p, S)
    n_chunks = S // ATTN_KV_CHUNK
    vt = v.reshape(B, n_chunks, ATTN_KV_CHUNK, N_KV_HEADS, HEAD_DIM).transpose(0, 3, 1, 4, 2)
    vt = jnp.concatenate([vt, jnp.ones((B, N_KV_HEADS, n_chunks, V_ROWS - HEAD_DIM, ATTN_KV_CHUNK), BF16)], axis=3)
    a = _attention(q, k, vt, B, S)
    hf, hb = _mlstm(m4, gate, gate.T, p, B, S)
    x1 = _merge(a, hf, hb, m4, sg, x2, p)
    y = _ffn(x1, p, S)
    return y.reshape(B, S, D_MODEL)


def kernel(x_prompt, x_sample, norm_mix_pre, w_in, b_if, q_norm, k_norm, mlstm_norm, w_attn_o, w_mlstm_o,
           w_out, norm_mix_post, norm_ffn_pre, w_up, conv_w, conv_b, w_down, norm_ffn_post):
    depth = w_in.shape[0]
    seq_lens = sorted({x_prompt.shape[1], x_sample.shape[1]})
    layers = [
        _prepare(norm_mix_pre[l], w_in[l], b_if[l], q_norm[l], k_norm[l], mlstm_norm[l], w_attn_o[l],
                 w_mlstm_o[l], w_out[l], norm_mix_post[l], norm_ffn_pre[l], w_up[l], conv_w[l], conv_b[l],
                 w_down[l], norm_ffn_post[l], seq_lens)
        for l in range(depth)
    ]

    def trunk(x):
        for p in layers:
            x = _trunk(x, p)
        return x

    return (trunk(x_prompt), trunk(x_sample))
```

```python
import functools

import numpy as np
import jax
import jax.numpy as jnp
from jax import lax
from jax.experimental import pallas as pl
from jax.experimental.pallas import tpu as pltpu

D_MODEL = 1024
GRID_W = 64
N_Q_HEADS = 8
N_KV_HEADS = 2
GQA_GROUP = N_Q_HEADS // N_KV_HEADS
HEAD_DIM = 64
ATTN_WIDTH = N_Q_HEADS * HEAD_DIM
KV_WIDTH = N_KV_HEADS * HEAD_DIM
ROPE_THETA = 10000.0
M_HEADS = 4
M_HEAD_DIM = 128
M_WIDTH = M_HEADS * M_HEAD_DIM
N_GATE_PRE = 4 * M_HEADS
D_FF = 2816
CONV_W = 3
EPS = 1e-6

LANES = 128
MXU_DIM = 256
VMEM_LIMIT = 56 * 1024 * 1024

TOKEN_TILE = 512
ATTN_Q_TILE = 512
ATTN_KV_CHUNK = 512
V_ROWS = HEAD_DIM + 16
M_CHUNK = 128
FFN_TOKEN_TILE = 256
HALO = 16
LOG2E = 1.4426950408889634
MAX_UNSTABILISED_SCORE = 40.0

BF16 = jnp.bfloat16
F32 = jnp.float32


def _sigmoid(x):
    return 0.5 * jnp.tanh(0.5 * x) + 0.5


def _const_spec(shape):
    n = len(shape)
    return pl.BlockSpec(shape, lambda *_: (0,) * n, pipeline_mode=pl.Buffered(1))


def _in_proj_kernel(x_ref, g_ref, wa_ref, wm_ref, wg_ref, ws_ref, bd_ref,
                    tqa_ref, tqb_ref, tka_ref, tkb_ref,
                    q_ref, k_ref, v_ref, m_ref, gate_ref, sg_ref):
    x = x_ref[...]
    ms = jnp.mean(x * x, axis=-1, keepdims=True)
    h = (x * lax.rsqrt(ms + EPS) * g_ref[...]).astype(BF16)

    za = jnp.dot(h, wa_ref[...], preferred_element_type=F32)
    bd = bd_ref[...]
    lane = lax.broadcasted_iota(jnp.int32, (x.shape[0], LANES), 1)
    first_half = (lane % HEAD_DIM) < (HEAD_DIM // 2)

    def head_norm_rope(z, msq, ta, tb):
        sw = jnp.where(first_half, pltpu.roll(z, LANES - HEAD_DIM // 2, 1), pltpu.roll(z, HEAD_DIM // 2, 1))
        return lax.rsqrt(msq + EPS) * (z * ta + sw * tb)

    tqa, tqb = tqa_ref[...], tqb_ref[...]
    for half in range(ATTN_WIDTH // MXU_DIM):
        zq = za[:, half * MXU_DIM:(half + 1) * MXU_DIM]
        msq = jnp.dot((zq * zq).astype(BF16), bd, preferred_element_type=F32)
        for j in range(MXU_DIM // LANES):
            c0 = half * MXU_DIM + j * LANES
            out = head_norm_rope(zq[:, j * LANES:(j + 1) * LANES], msq[:, j * LANES:(j + 1) * LANES], tqa, tqb)
            q_ref[:, c0:c0 + LANES] = out.astype(BF16)
    zk = za[:, ATTN_WIDTH:ATTN_WIDTH + KV_WIDTH]
    msk = jnp.dot((zk * zk).astype(BF16), bd[:LANES, :LANES], preferred_element_type=F32)
    k_ref[...] = head_norm_rope(zk, msk, tka_ref[...], tkb_ref[...]).astype(BF16)
    v_ref[...] = za[:, ATTN_WIDTH + KV_WIDTH:].astype(BF16)

    zm = jnp.dot(h, wm_ref[...], preferred_element_type=F32)
    m_ref[:, 0:M_WIDTH] = zm[:, 0:M_WIDTH].astype(BF16)
    m_ref[:, M_WIDTH:2 * M_WIDTH] = (zm[:, M_WIDTH:2 * M_WIDTH] * (M_HEAD_DIM ** -0.5)).astype(BF16)
    m_ref[:, 2 * M_WIDTH:3 * M_WIDTH] = zm[:, 2 * M_WIDTH:3 * M_WIDTH].astype(BF16)
    m_ref[:, 3 * M_WIDTH:] = _sigmoid(zm[:, 3 * M_WIDTH:]).astype(BF16)

    zg = jnp.dot(h, wg_ref[...], preferred_element_type=F32)
    gate_ref[...] = zg[:, :N_GATE_PRE]

    zs = jnp.dot(h, ws_ref[...], preferred_element_type=F32)
    sg_ref[...] = _sigmoid(zs).astype(BF16)


def _in_proj(x2, p, S):
    T = x2.shape[0]
    tm = TOKEN_TILE
    spt = S // tm
    row = lambda i: (i, 0)
    pos = lambda i: (i % spt, 0)
    tab = pl.BlockSpec((tm, LANES), pos)
    return pl.pallas_call(
        _in_proj_kernel,
        grid=(T // tm,),
        in_specs=[
            pl.BlockSpec((tm, D_MODEL), row),
            _const_spec((1, D_MODEL)),
            _const_spec(p["wa"].shape), _const_spec(p["wm"].shape),
            _const_spec(p["wg"].shape), _const_spec(p["ws"].shape),
            _const_spec((MXU_DIM, MXU_DIM)),
            tab, tab, tab, tab,
        ],
        out_specs=[
            pl.BlockSpec((tm, ATTN_WIDTH), row),
            pl.BlockSpec((tm, KV_WIDTH), row),
            pl.BlockSpec((tm, KV_WIDTH), row),
            pl.BlockSpec((tm, 4 * M_WIDTH), row),
            pl.BlockSpec((tm, N_GATE_PRE), row),
            pl.BlockSpec((tm, 2 * D_MODEL), row),
        ],
        out_shape=[
            jax.ShapeDtypeStruct((T, ATTN_WIDTH), BF16),
            jax.ShapeDtypeStruct((T, KV_WIDTH), BF16),
            jax.ShapeDtypeStruct((T, KV_WIDTH), BF16),
            jax.ShapeDtypeStruct((T, 4 * M_WIDTH), BF16),
            jax.ShapeDtypeStruct((T, N_GATE_PRE), F32),
            jax.ShapeDtypeStruct((T, 2 * D_MODEL), BF16),
        ],
        compiler_params=pltpu.CompilerParams(
            dimension_semantics=("parallel",), vmem_limit_bytes=VMEM_LIMIT),
        name="in_proj",
    )(x2, p["g_mix_pre"], p["wa"], p["wm"], p["wg"], p["ws"], p["bd"],
      p["tqa"][S], p["tqb"][S], p["tka"][S], p["tkb"][S])


def _attn_kernel(q_ref, k_ref, vt_ref, o_ref, w_ref, p_ref, acc_ref, *, n_chunks, tq, kc, stabilise):
    lane = lax.broadcasted_iota(jnp.int32, (tq, LANES), 1)
    low = lane < HEAD_DIM
    zero = jnp.zeros((tq, LANES), BF16)
    for j in range(GQA_GROUP):
        qj = q_ref[:, j * LANES:(j + 1) * LANES]
        w_ref[(2 * j) * tq:(2 * j + 1) * tq, :] = jnp.where(low, qj, zero)
        w_ref[(2 * j + 1) * tq:(2 * j + 2) * tq, :] = jnp.where(low, zero, qj)
    acc_ref[...] = jnp.zeros(acc_ref.shape, F32)

    def probs(c, m):
        kch = k_ref[pl.ds(pl.multiple_of(c * kc, kc), kc), :]
        s = lax.dot_general(kch, w_ref[...], (((1,), (1,)), ((), ())),
                            preferred_element_type=F32)
        if not stabilise:
            p_ref[c % 2] = jnp.exp2(s).astype(BF16)
            return m, m
        mn = jnp.maximum(m, jnp.max(s, axis=0, keepdims=True))
        p_ref[c % 2] = jnp.exp2(s - mn).astype(BF16)
        return mn, jnp.exp2(m - mn)

    def accumulate(c, alpha):
        for slot in range(N_Q_HEADS):
            cols = slice(slot * tq, (slot + 1) * tq)
            pv = jnp.dot(vt_ref[slot % N_KV_HEADS, c], p_ref[c % 2, :, cols], preferred_element_type=F32)
            if stabilise:
                acc_ref[slot] = acc_ref[slot] * alpha[:, cols] + pv
            else:
                acc_ref[slot] += pv

    width = N_Q_HEADS * tq if stabilise else LANES
    state = probs(0, jnp.full((1, width), -jnp.inf, F32))

    def body(c, st):
        nxt = probs(c + 1, st[0])
        accumulate(c, st[1])
        return nxt

    state = lax.fori_loop(0, n_chunks - 1, body, state)
    accumulate(n_chunks - 1, state[1])

    outs = []
    for slot in range(N_Q_HEADS):
        a = acc_ref[slot]
        outs.append(a[:HEAD_DIM] / a[HEAD_DIM:HEAD_DIM + 1])
    o_ref[...] = jnp.concatenate(outs, axis=0).T.astype(BF16)


def _attention(q, k, vt, B, S, stabilise):
    T = q.shape[0]
    tq, kc = ATTN_Q_TILE, ATTN_KV_CHUNK
    nq, n_chunks = S // tq, S // kc
    kern = functools.partial(_attn_kernel, n_chunks=n_chunks, tq=tq, kc=kc, stabilise=stabilise)
    return pl.pallas_call(
        kern,
        grid=(B, nq),
        in_specs=[
            pl.BlockSpec((tq, ATTN_WIDTH), lambda b, i: (b * nq + i, 0)),
            pl.BlockSpec((S, KV_WIDTH), lambda b, i: (b, 0)),
            pl.BlockSpec((None, N_KV_HEADS, n_chunks, V_ROWS, kc), lambda b, i: (b, 0, 0, 0, 0)),
        ],
        out_specs=pl.BlockSpec((tq, ATTN_WIDTH), lambda b, i: (b * nq + i, 0)),
        out_shape=jax.ShapeDtypeStruct((T, ATTN_WIDTH), BF16),
        scratch_shapes=[
            pltpu.VMEM((N_Q_HEADS * tq, LANES), BF16),
            pltpu.VMEM((2, kc, N_Q_HEADS * tq), BF16),
            pltpu.VMEM((N_Q_HEADS, V_ROWS, tq), F32),
        ],
        compiler_params=pltpu.CompilerParams(
            dimension_semantics=("parallel", "arbitrary"), vmem_limit_bytes=VMEM_LIMIT),
        name="attn",
    )(q, k, vt)


def _mlstm_kernel(qf_ref, kf_ref, vf_ref, qb_ref, kb_ref, vb_ref,
                  gcf_ref, gcb_ref, grf_ref, grb_ref, bc_ref, br_ref,
                  hf_ref, hb_ref, c_ref, m_ref):
    L = qf_ref.shape[0]

    @pl.when(pl.program_id(1) == 0)
    def _():
        c_ref[...] = jnp.zeros(c_ref.shape, F32)
        m_ref[...] = jnp.zeros(m_ref.shape, F32)

    ri = lax.broadcasted_iota(jnp.int32, (L, L), 0)
    ci = lax.broadcasted_iota(jnp.int32, (L, L), 1)
    lower = ci <= ri
    upper = ci >= ri
    lower_f = lower.astype(F32)
    upper_f = upper.astype(F32)
    ones = jnp.ones((L, M_HEAD_DIM), BF16)
    hp = lax.Precision.HIGHEST

    for d, (q_ref, k_ref, v_ref, gc_ref, gr_ref, h_ref) in enumerate(
            ((qf_ref, kf_ref, vf_ref, gcf_ref, grf_ref, hf_ref),
             (qb_ref, kb_ref, vb_ref, gcb_ref, grb_ref, hb_ref))):
        gcol = gc_ref[...] + bc_ref[...]
        grow = gr_ref[...] + br_ref[...]
        lf_col = jax.nn.log_sigmoid(gcol)
        lf_row = jax.nn.log_sigmoid(grow)
        if d == 0:
            cum_col = jnp.dot(lower_f, lf_col, precision=hp, preferred_element_type=F32)
            cum_row = jnp.dot(lf_row, upper_f, precision=hp, preferred_element_type=F32)
            mask = lower
        else:
            cum_col = jnp.dot(upper_f, lf_col, precision=hp, preferred_element_type=F32)
            cum_row = jnp.dot(lf_row, lower_f, precision=hp, preferred_element_type=F32)
            mask = upper
        for hd in range(M_HEADS):
            st = d * M_HEADS + hd
            gi = 2 * d * M_HEADS + hd
            gf = gi + M_HEADS
            b_col = cum_col[:, gf:gf + 1]
            b_row = cum_row[gf:gf + 1, :]
            r_row = grow[gi:gi + 1, :] - b_row
            b_tot = b_row[:, L - 1:L] if d == 0 else b_row[:, 0:1]
            m_old = m_ref[st:st + 1, 0:1]

            q = q_ref[:, hd * M_HEAD_DIM:(hd + 1) * M_HEAD_DIM]
            k = k_ref[:, hd * M_HEAD_DIM:(hd + 1) * M_HEAD_DIM]
            v_ext = jnp.concatenate([v_ref[:, hd * M_HEAD_DIM:(hd + 1) * M_HEAD_DIM], ones], axis=1)
            c_old = c_ref[st]

            dm = jnp.where(mask, b_col + r_row, -jnp.inf)
            inter = b_col + m_old
            mt = jnp.maximum(jnp.max(dm, axis=1, keepdims=True), inter)
            qk = lax.dot_general(q, k, (((1,), (1,)), ((), ())), preferred_element_type=F32)
            w = (jnp.exp(dm - mt) * qk).astype(BF16)
            dec = jnp.exp(inter - mt)
            ext = (jnp.dot(w, v_ext, preferred_element_type=F32)
                   + dec * jnp.dot(q, c_old.astype(BF16), preferred_element_type=F32))
            num, den = ext[:, :M_HEAD_DIM], ext[:, M_HEAD_DIM:]
            h = num / jnp.maximum(jnp.abs(den), jnp.exp(-mt))
            h_ref[:, hd * M_HEAD_DIM:(hd + 1) * M_HEAD_DIM] = h.astype(h_ref.dtype)

            ws_row = b_tot + r_row
            m_new = jnp.maximum(b_tot + m_old, jnp.max(ws_row, axis=1, keepdims=True))
            a = jnp.exp(b_tot + m_old - m_new)
            w_row = jnp.exp(ws_row - m_new)
            kw = (k.astype(F32).T * w_row).astype(BF16)
            c_ref[st] = a * c_old + jnp.dot(kw, v_ext, preferred_element_type=F32)
            m_ref[st:st + 1, :] = jnp.broadcast_to(m_new, (1, LANES))


def _mlstm(m4, gate, gate_t, p, B, S):
    T = m4.shape[0]
    L = M_CHUNK
    nc = S // L
    fwd = lambda col: (lambda b, c: (b * nc + c, col))
    bwd = lambda col: (lambda b, c: (b * nc + nc - 1 - c, col))
    blk = lambda f: pl.BlockSpec((L, M_WIDTH), f)
    return pl.pallas_call(
        _mlstm_kernel,
        grid=(B, nc),
        in_specs=[
            blk(fwd(0)), blk(fwd(1)), blk(fwd(2)),
            blk(bwd(0)), blk(bwd(1)), blk(bwd(2)),
            pl.BlockSpec((L, N_GATE_PRE), fwd(0)),
            pl.BlockSpec((L, N_GATE_PRE), bwd(0)),
            pl.BlockSpec((N_GATE_PRE, L), lambda b, c: (0, b * nc + c)),
            pl.BlockSpec((N_GATE_PRE, L), lambda b, c: (0, b * nc + nc - 1 - c)),
            _const_spec((1, N_GATE_PRE)), _const_spec((N_GATE_PRE, 1)),
        ],
        out_specs=[blk(fwd(0)), blk(bwd(0))],
        out_shape=[jax.ShapeDtypeStruct((T, M_WIDTH), BF16)] * 2,
        scratch_shapes=[
            pltpu.VMEM((2 * M_HEADS, M_HEAD_DIM, 2 * M_HEAD_DIM), F32),
            pltpu.VMEM((2 * M_HEADS, LANES), F32),
        ],
        compiler_params=pltpu.CompilerParams(
            dimension_semantics=("parallel", "arbitrary"), vmem_limit_bytes=VMEM_LIMIT),
        name="mlstm",
    )(m4, m4, m4, m4, m4, m4, gate, gate, gate_t, gate_t, p["b_if_row"], p["b_if_col"])


def _merge_kernel(a_ref, hf_ref, hb_ref, so_ref, sg_ref, x_ref, gm_ref, wao_ref, wmo_ref, wout_ref, gp_ref, o_ref):
    hs = hf_ref[...].astype(F32) + hb_ref[...].astype(F32)
    parts = []
    for hd in range(M_HEADS):
        blk = hs[:, hd * M_HEAD_DIM:(hd + 1) * M_HEAD_DIM]
        ms = jnp.mean(blk * blk, axis=-1, keepdims=True)
        parts.append(blk * lax.rsqrt(ms + EPS))
    hn = jnp.concatenate(parts, axis=1) * gm_ref[...]
    hm = (hn * so_ref[...].astype(F32)).astype(BF16)
    a_out = jnp.dot(a_ref[...], wao_ref[...], preferred_element_type=F32)
    m_out = jnp.dot(hm, wmo_ref[...], preferred_element_type=F32)
    sg = sg_ref[...]
    merged = sg[:, :D_MODEL].astype(F32) * a_out + sg[:, D_MODEL:].astype(F32) * m_out
    y = jnp.dot(merged.astype(BF16), wout_ref[...], preferred_element_type=F32)
    ms = jnp.mean(y * y, axis=-1, keepdims=True)
    o_ref[...] = x_ref[...] + y * lax.rsqrt(ms + EPS) * gp_ref[...]


def _merge(a, hf, hb, m4, sg, x2, p):
    T = x2.shape[0]
    tm = TOKEN_TILE
    row = lambda i: (i, 0)
    return pl.pallas_call(
        _merge_kernel,
        grid=(T // tm,),
        in_specs=[
            pl.BlockSpec((tm, ATTN_WIDTH), row),
            pl.BlockSpec((tm, M_WIDTH), row),
            pl.BlockSpec((tm, M_WIDTH), row),
            pl.BlockSpec((tm, M_WIDTH), lambda i: (i, 3)),
            pl.BlockSpec((tm, 2 * D_MODEL), row),
            pl.BlockSpec((tm, D_MODEL), row),
            _const_spec((1, M_WIDTH)),
            _const_spec((ATTN_WIDTH, D_MODEL)), _const_spec((M_WIDTH, D_MODEL)),
            _const_spec((D_MODEL, D_MODEL)), _const_spec((1, D_MODEL)),
        ],
        out_specs=pl.BlockSpec((tm, D_MODEL), row),
        out_shape=jax.ShapeDtypeStruct((T, D_MODEL), F32),
        compiler_params=pltpu.CompilerParams(
            dimension_semantics=("parallel",), vmem_limit_bytes=VMEM_LIMIT),
        name="merge",
    )(a, hf, hb, m4, sg, x2, p["g_mlstm"], p["wao"], p["wmo"], p["wout"], p["g_mix_post"])


def _ffn_kernel(xp_ref, x_ref, xn_ref, g_ref, wg_ref, wv_ref, cw_ref, cb_ref, wd_ref, gp_ref, o_ref,
                h_ref, ge_ref, *, tiles_per_seq):
    tm = x_ref.shape[0]
    g = g_ref[...]

    def norm(x):
        ms = jnp.mean(x * x, axis=-1, keepdims=True)
        return (x * lax.rsqrt(ms + EPS) * g).astype(BF16)

    h_ref[0:HALO, :] = norm(xp_ref[...])
    h_ref[HALO:HALO + tm, :] = norm(x_ref[...])
    h_ref[HALO + tm:, :] = norm(xn_ref[...])
    pos = pl.program_id(0) % tiles_per_seq
    keep_prev = (pos != 0).astype(F32)
    keep_next = (pos != tiles_per_seq - 1).astype(F32)

    ge_ref[...] = jnp.dot(h_ref[...], wg_ref[...], preferred_element_type=F32)
    ge_ref[HALO - 1:HALO, :] = ge_ref[HALO - 1:HALO, :] * keep_prev
    ge_ref[HALO + tm:HALO + tm + 1, :] = ge_ref[HALO + tm:HALO + tm + 1, :] * keep_next
    val = jnp.dot(h_ref[HALO:HALO + tm, :], wv_ref[...], preferred_element_type=F32)
    cw = cw_ref[...]
    conv = (ge_ref[HALO - 1:HALO - 1 + tm, :] * cw[0:1]
            + ge_ref[HALO:HALO + tm, :] * cw[1:2]
            + ge_ref[HALO + 1:HALO + 1 + tm, :] * cw[2:3]
            + cb_ref[...])
    act = (jax.nn.gelu(conv, approximate=True) * val).astype(BF16)
    y = jnp.dot(act, wd_ref[...], preferred_element_type=F32)
    ms = jnp.mean(y * y, axis=-1, keepdims=True)
    o_ref[...] = x_ref[...] + y * lax.rsqrt(ms + EPS) * gp_ref[...]


def _ffn(x1, p, S):
    T = x1.shape[0]
    tm = FFN_TOKEN_TILE
    hpt = tm // HALO
    n_halo = T // HALO
    kern = functools.partial(_ffn_kernel, tiles_per_seq=S // tm)
    return pl.pallas_call(
        kern,
        grid=(T // tm,),
        in_specs=[
            pl.BlockSpec((HALO, D_MODEL), lambda i: (jnp.maximum(i * hpt - 1, 0), 0)),
            pl.BlockSpec((tm, D_MODEL), lambda i: (i, 0)),
            pl.BlockSpec((HALO, D_MODEL), lambda i: (jnp.minimum((i + 1) * hpt, n_halo - 1), 0)),
            _const_spec((1, D_MODEL)),
            _const_spec((D_MODEL, D_FF)), _const_spec((D_MODEL, D_FF)),
            _const_spec((8, D_FF)), _const_spec((1, D_FF)),
            _const_spec((D_FF, D_MODEL)), _const_spec((1, D_MODEL)),
        ],
        out_specs=pl.BlockSpec((tm, D_MODEL), lambda i: (i, 0)),
        out_shape=jax.ShapeDtypeStruct((T, D_MODEL), F32),
        scratch_shapes=[
            pltpu.VMEM((tm + 2 * HALO, D_MODEL), BF16),
            pltpu.VMEM((tm + 2 * HALO, D_FF), F32),
        ],
        compiler_params=pltpu.CompilerParams(
            dimension_semantics=("parallel",), vmem_limit_bytes=VMEM_LIMIT),
        name="ffn",
    )(x1, x1, x1, p["g_ffn_pre"], p["wup_g"], p["wup_v"], p["conv_w"], p["conv_b"], p["wdown"], p["g_ffn_post"])


def _rope_tables(S, q_gain, k_gain):
    rows = S // GRID_W
    r = jnp.repeat(jnp.arange(rows, dtype=F32), GRID_W)
    c = jnp.tile(jnp.arange(GRID_W, dtype=F32), rows)
    n_pairs_axis = HEAD_DIM // 4
    freq = ROPE_THETA ** (-jnp.arange(n_pairs_axis, dtype=F32) / n_pairs_axis)
    ang = jnp.concatenate([r[:, None] * freq, c[:, None] * freq], axis=-1)
    cos, sin = jnp.cos(ang), jnp.sin(ang)
    cc = jnp.concatenate([cos, cos], axis=-1)
    ss = jnp.concatenate([-sin, sin], axis=-1)

    def tables(gain, scale):
        g_half = jnp.concatenate([gain[0::2], gain[1::2]])
        g_swap = jnp.concatenate([gain[1::2], gain[0::2]])
        ta = cc * g_half * scale
        tb = ss * g_swap * scale
        return jnp.tile(ta, (1, LANES // HEAD_DIM)), jnp.tile(tb, (1, LANES // HEAD_DIM))

    tqa, tqb = tables(q_gain, LOG2E * HEAD_DIM ** -0.5)
    tka, tkb = tables(k_gain, 1.0)
    return tqa, tqb, tka, tkb


def _prepare(norm_mix_pre, w_in, b_if, q_norm, k_norm, mlstm_norm, w_attn_o, w_mlstm_o, w_out,
             norm_mix_post, norm_ffn_pre, w_up, conv_w, conv_b, w_down, norm_ffn_post, seq_lens):
    half = np.concatenate([np.arange(0, HEAD_DIM, 2), np.arange(1, HEAD_DIM, 2)])
    slots = [h for j in range(GQA_GROUP) for h in (j, j + GQA_GROUP)]
    q_cols = np.concatenate([h * HEAD_DIM + half for h in slots])
    k_cols = ATTN_WIDTH + np.concatenate([h * HEAD_DIM + half for h in range(N_KV_HEADS)])
    v_cols = ATTN_WIDTH + KV_WIDTH + np.arange(KV_WIDTH)
    o_rows = np.concatenate([h * HEAD_DIM + np.arange(HEAD_DIM) for h in slots])
    m0 = ATTN_WIDTH + 2 * KV_WIDTH
    g0 = m0 + 4 * M_WIDTH
    s0 = g0 + N_GATE_PRE

    p = {}
    p["wa"] = w_in[:, np.concatenate([q_cols, k_cols, v_cols])].astype(BF16)
    p["wm"] = w_in[:, m0:g0].astype(BF16)
    p["wg"] = jnp.pad(w_in[:, g0:s0], ((0, 0), (0, LANES - N_GATE_PRE))).astype(BF16)
    p["ws"] = w_in[:, s0:].astype(BF16)
    blk = np.arange(MXU_DIM) // HEAD_DIM
    p["bd"] = jnp.asarray((blk[:, None] == blk[None, :]).astype(np.float32) / HEAD_DIM, BF16)
    p["g_mix_pre"] = norm_mix_pre.reshape(1, D_MODEL)
    p["b_if_row"] = b_if.reshape(1, N_GATE_PRE)
    p["b_if_col"] = b_if.reshape(N_GATE_PRE, 1)
    p["g_mlstm"] = mlstm_norm.reshape(1, M_WIDTH)
    p["wao"] = w_attn_o[o_rows].astype(BF16)
    p["wmo"] = w_mlstm_o.astype(BF16)
    p["wout"] = w_out.astype(BF16)
    p["g_mix_post"] = norm_mix_post.reshape(1, D_MODEL)
    p["g_ffn_pre"] = norm_ffn_pre.reshape(1, D_MODEL)
    p["wup_g"] = w_up[:, :D_FF].astype(BF16)
    p["wup_v"] = w_up[:, D_FF:].astype(BF16)
    p["conv_w"] = jnp.pad(conv_w, ((0, 8 - CONV_W), (0, 0)))
    p["conv_b"] = conv_b.reshape(1, D_FF)
    p["wdown"] = w_down.astype(BF16)
    p["g_ffn_post"] = norm_ffn_post.reshape(1, D_MODEL)
    p["score_bound"] = (1.02 * LOG2E * HEAD_DIM ** 0.5) * jnp.max(jnp.abs(q_norm)) * jnp.max(jnp.abs(k_norm))
    p["tqa"], p["tqb"], p["tka"], p["tkb"] = {}, {}, {}, {}
    for S in seq_lens:
        p["tqa"][S], p["tqb"][S], p["tka"][S], p["tkb"][S] = _rope_tables(S, q_norm, k_norm)
    return p


def _trunk(x, p):
    B, S, _ = x.shape
    T = B * S
    x2 = x.reshape(T, D_MODEL)
    q, k, v, m4, gate, sg = _in_proj(x2, p, S)
    n_chunks = S // ATTN_KV_CHUNK
    vt = v.reshape(B, n_chunks, ATTN_KV_CHUNK, N_KV_HEADS, HEAD_DIM).transpose(0, 3, 1, 4, 2)
    vt = jnp.concatenate([vt, jnp.ones((B, N_KV_HEADS, n_chunks, V_ROWS - HEAD_DIM, ATTN_KV_CHUNK), BF16)], axis=3)
    a = lax.cond(p["score_bound"] <= MAX_UNSTABILISED_SCORE,
                 lambda: _attention(q, k, vt, B, S, False),
                 lambda: _attention(q, k, vt, B, S, True))
    hf, hb = _mlstm(m4, gate, gate.T, p, B, S)
    x1 = _merge(a, hf, hb, m4, sg, x2, p)
    y = _ffn(x1, p, S)
    return y.reshape(B, S, D_MODEL)


def kernel(x_prompt, x_sample, norm_mix_pre, w_in, b_if, q_norm, k_norm, mlstm_norm, w_attn_o, w_mlstm_o,
           w_out, norm_mix_post, norm_ffn_pre, w_up, conv_w, conv_b, w_down, norm_ffn_post):
    depth = w_in.shape[0]
    seq_lens = sorted({x_prompt.shape[1], x_sample.shape[1]})
    layers = [
        _prepare(norm_mix_pre[l], w_in[l], b_if[l], q_norm[l], k_norm[l], mlstm_norm[l], w_attn_o[l],
                 w_mlstm_o[l], w_out[l], norm_mix_post[l], norm_ffn_pre[l], w_up[l], conv_w[l], conv_b[l],
                 w_down[l], norm_ffn_post[l], seq_lens)
        for l in range(depth)
    ]

    def trunk(x):
        for p in layers:
            x = _trunk(x, p)
        return x

    return (trunk(x_prompt), trunk(x_sample))
```

```python
import functools

import numpy as np
import jax
import jax.numpy as jnp
from jax import lax
from jax.experimental import pallas as pl
from jax.experimental.pallas import tpu as pltpu

D_MODEL = 1024
GRID_W = 64
N_Q_HEADS = 8
N_KV_HEADS = 2
GQA_GROUP = N_Q_HEADS // N_KV_HEADS
HEAD_DIM = 64
ATTN_WIDTH = N_Q_HEADS * HEAD_DIM
KV_WIDTH = N_KV_HEADS * HEAD_DIM
ROPE_THETA = 10000.0
M_HEADS = 4
M_HEAD_DIM = 128
M_WIDTH = M_HEADS * M_HEAD_DIM
N_GATE_PRE = 4 * M_HEADS
D_FF = 2816
CONV_W = 3
EPS = 1e-6

LANES = 128
MXU_DIM = 256
VMEM_LIMIT = 56 * 1024 * 1024

TOKEN_TILE = 512
ATTN_Q_TILE = 512
ATTN_KV_CHUNK = 512
V_ROWS = HEAD_DIM + 16
M_CHUNK = 128
FFN_TOKEN_TILE = 256
HALO = 16
LOG2E = 1.4426950408889634
MAX_UNSTABILISED_SCORE = 40.0

BF16 = jnp.bfloat16
F32 = jnp.float32


def _sigmoid(x):
    return 0.5 * jnp.tanh(0.5 * x) + 0.5


def _const_spec(shape):
    n = len(shape)
    return pl.BlockSpec(shape, lambda *_: (0,) * n, pipeline_mode=pl.Buffered(1))


def _in_proj_kernel(x_ref, g_ref, wa_ref, wm_ref, wg_ref, ws_ref, bd_ref,
                    tqa_ref, tqb_ref, tka_ref, tkb_ref,
                    q_ref, k_ref, v_ref, m_ref, gate_ref, sg_ref):
    x = x_ref[...]
    ms = jnp.mean(x * x, axis=-1, keepdims=True)
    h = (x * lax.rsqrt(ms + EPS) * g_ref[...]).astype(BF16)

    za = jnp.dot(h, wa_ref[...], preferred_element_type=F32)
    bd = bd_ref[...]
    lane = lax.broadcasted_iota(jnp.int32, (x.shape[0], LANES), 1)
    first_half = (lane % HEAD_DIM) < (HEAD_DIM // 2)

    def head_norm_rope(z, msq, ta, tb):
        sw = jnp.where(first_half, pltpu.roll(z, LANES - HEAD_DIM // 2, 1), pltpu.roll(z, HEAD_DIM // 2, 1))
        return lax.rsqrt(msq + EPS) * (z * ta + sw * tb)

    tqa, tqb = tqa_ref[...], tqb_ref[...]
    for half in range(ATTN_WIDTH // MXU_DIM):
        zq = za[:, half * MXU_DIM:(half + 1) * MXU_DIM]
        msq = jnp.dot((zq * zq).astype(BF16), bd, preferred_element_type=F32)
        for j in range(MXU_DIM // LANES):
            c0 = half * MXU_DIM + j * LANES
            out = head_norm_rope(zq[:, j * LANES:(j + 1) * LANES], msq[:, j * LANES:(j + 1) * LANES], tqa, tqb)
            q_ref[:, c0:c0 + LANES] = out.astype(BF16)
    zk = za[:, ATTN_WIDTH:ATTN_WIDTH + KV_WIDTH]
    msk = jnp.dot((zk * zk).astype(BF16), bd[:LANES, :LANES], preferred_element_type=F32)
    k_ref[...] = head_norm_rope(zk, msk, tka_ref[...], tkb_ref[...]).astype(BF16)
    v_ref[...] = za[:, ATTN_WIDTH + KV_WIDTH:].astype(BF16)

    zm = jnp.dot(h, wm_ref[...], preferred_element_type=F32)
    m_ref[:, 0:M_WIDTH] = zm[:, 0:M_WIDTH].astype(BF16)
    m_ref[:, M_WIDTH:2 * M_WIDTH] = (zm[:, M_WIDTH:2 * M_WIDTH] * (M_HEAD_DIM ** -0.5)).astype(BF16)
    m_ref[:, 2 * M_WIDTH:3 * M_WIDTH] = zm[:, 2 * M_WIDTH:3 * M_WIDTH].astype(BF16)
    m_ref[:, 3 * M_WIDTH:] = _sigmoid(zm[:, 3 * M_WIDTH:]).astype(BF16)

    zg = jnp.dot(h, wg_ref[...], preferred_element_type=F32)
    gate_ref[...] = zg[:, :N_GATE_PRE]

    zs = jnp.dot(h, ws_ref[...], preferred_element_type=F32)
    sg_ref[...] = _sigmoid(zs).astype(BF16)


def _in_proj(x2, p, S):
    T = x2.shape[0]
    tm = TOKEN_TILE
    spt = S // tm
    row = lambda i: (i, 0)
    pos = lambda i: (i % spt, 0)
    tab = pl.BlockSpec((tm, LANES), pos)
    return pl.pallas_call(
        _in_proj_kernel,
        grid=(T // tm,),
        in_specs=[
            pl.BlockSpec((tm, D_MODEL), row),
            _const_spec((1, D_MODEL)),
            _const_spec(p["wa"].shape), _const_spec(p["wm"].shape),
            _const_spec(p["wg"].shape), _const_spec(p["ws"].shape),
            _const_spec((MXU_DIM, MXU_DIM)),
            tab, tab, tab, tab,
        ],
        out_specs=[
            pl.BlockSpec((tm, ATTN_WIDTH), row),
            pl.BlockSpec((tm, KV_WIDTH), row),
            pl.BlockSpec((tm, KV_WIDTH), row),
            pl.BlockSpec((tm, 4 * M_WIDTH), row),
            pl.BlockSpec((tm, N_GATE_PRE), row),
            pl.BlockSpec((tm, 2 * D_MODEL), row),
        ],
        out_shape=[
            jax.ShapeDtypeStruct((T, ATTN_WIDTH), BF16),
            jax.ShapeDtypeStruct((T, KV_WIDTH), BF16),
            jax.ShapeDtypeStruct((T, KV_WIDTH), BF16),
            jax.ShapeDtypeStruct((T, 4 * M_WIDTH), BF16),
            jax.ShapeDtypeStruct((T, N_GATE_PRE), F32),
            jax.ShapeDtypeStruct((T, 2 * D_MODEL), BF16),
        ],
        compiler_params=pltpu.CompilerParams(
            dimension_semantics=("parallel",), vmem_limit_bytes=VMEM_LIMIT),
        name="in_proj",
    )(x2, p["g_mix_pre"], p["wa"], p["wm"], p["wg"], p["ws"], p["bd"],
      p["tqa"][S], p["tqb"][S], p["tka"][S], p["tkb"][S])


def _attn_kernel(q_ref, k_ref, vt_ref, o_ref, w_ref, p0_ref, p1_ref, acc_ref, *, n_chunks, tq, kc, stabilise):
    lane = lax.broadcasted_iota(jnp.int32, (tq, LANES), 1)
    low = lane < HEAD_DIM
    zero = jnp.zeros((tq, LANES), BF16)
    for j in range(GQA_GROUP):
        qj = q_ref[:, j * LANES:(j + 1) * LANES]
        w_ref[(2 * j) * tq:(2 * j + 1) * tq, :] = jnp.where(low, qj, zero)
        w_ref[(2 * j + 1) * tq:(2 * j + 2) * tq, :] = jnp.where(low, zero, qj)
    acc_ref[...] = jnp.zeros(acc_ref.shape, F32)

    def step(c, ms, src=None, dst=None):
        new_ms = []
        if dst is not None:
            start = 0 if src is None else pl.multiple_of((c + 1) * kc, kc)
            kch = k_ref[pl.ds(start, kc), :]
        for slot in range(N_Q_HEADS):
            cols = slice(slot * tq, (slot + 1) * tq)
            if dst is not None:
                s = lax.dot_general(kch, w_ref[cols, :], (((1,), (1,)), ((), ())),
                                    preferred_element_type=F32)
                if stabilise:
                    m = ms[slot][0]
                    mn = jnp.maximum(m, jnp.max(s, axis=0, keepdims=True))
                    new_ms.append((mn, jnp.exp2(m - mn)))
                    s = s - mn
                dst[:, cols] = jnp.exp2(s).astype(BF16)
            if src is not None:
                pv = jnp.dot(vt_ref[slot % N_KV_HEADS, c], src[:, cols], preferred_element_type=F32)
                if stabilise:
                    acc_ref[slot] = acc_ref[slot] * ms[slot][1] + pv
                else:
                    acc_ref[slot] += pv
        return ms if dst is None else tuple(new_ms)

    def pair(j, ms):
        ms = step(2 * j, ms, src=p0_ref, dst=p1_ref)
        return step(2 * j + 1, ms, src=p1_ref, dst=p0_ref)

    neg_inf = jnp.full((1, tq), -jnp.inf, F32)
    state = tuple((neg_inf, neg_inf) for _ in range(N_Q_HEADS)) if stabilise else ()
    state = step(-1, state, dst=p0_ref)
    state = lax.fori_loop(0, n_chunks // 2 - 1, pair, state)
    state = step(n_chunks - 2, state, src=p0_ref, dst=p1_ref)
    step(n_chunks - 1, state, src=p1_ref)

    outs = []
    for slot in range(N_Q_HEADS):
        a = acc_ref[slot]
        outs.append(a[:HEAD_DIM] / a[HEAD_DIM:HEAD_DIM + 1])
    o_ref[...] = jnp.concatenate(outs, axis=0).T.astype(BF16)


def _attention(q, k, vt, B, S, stabilise):
    T = q.shape[0]
    tq, kc = ATTN_Q_TILE, ATTN_KV_CHUNK
    nq, n_chunks = S // tq, S // kc
    kern = functools.partial(_attn_kernel, n_chunks=n_chunks, tq=tq, kc=kc, stabilise=stabilise)
    return pl.pallas_call(
        kern,
        grid=(B, nq),
        in_specs=[
            pl.BlockSpec((tq, ATTN_WIDTH), lambda b, i: (b * nq + i, 0)),
            pl.BlockSpec((S, KV_WIDTH), lambda b, i: (b, 0)),
            pl.BlockSpec((None, N_KV_HEADS, n_chunks, V_ROWS, kc), lambda b, i: (b, 0, 0, 0, 0)),
        ],
        out_specs=pl.BlockSpec((tq, ATTN_WIDTH), lambda b, i: (b * nq + i, 0)),
        out_shape=jax.ShapeDtypeStruct((T, ATTN_WIDTH), BF16),
        scratch_shapes=[
            pltpu.VMEM((N_Q_HEADS * tq, LANES), BF16),
            pltpu.VMEM((kc, N_Q_HEADS * tq), BF16),
            pltpu.VMEM((kc, N_Q_HEADS * tq), BF16),
            pltpu.VMEM((N_Q_HEADS, V_ROWS, tq), F32),
        ],
        compiler_params=pltpu.CompilerParams(
            dimension_semantics=("parallel", "arbitrary"), vmem_limit_bytes=VMEM_LIMIT),
        name="attn",
    )(q, k, vt)


def _mlstm_kernel(qf_ref, kf_ref, vf_ref, qb_ref, kb_ref, vb_ref,
                  gcf_ref, gcb_ref, grf_ref, grb_ref, bc_ref, br_ref,
                  hf_ref, hb_ref, c_ref, m_ref):
    L = qf_ref.shape[0]

    @pl.when(pl.program_id(1) == 0)
    def _():
        c_ref[...] = jnp.zeros(c_ref.shape, F32)
        m_ref[...] = jnp.zeros(m_ref.shape, F32)

    ri = lax.broadcasted_iota(jnp.int32, (L, L), 0)
    ci = lax.broadcasted_iota(jnp.int32, (L, L), 1)
    lower = ci <= ri
    upper = ci >= ri
    lower_f = lower.astype(F32)
    upper_f = upper.astype(F32)
    ones = jnp.ones((L, M_HEAD_DIM), BF16)
    hp = lax.Precision.HIGHEST

    chains = []
    for d, (q_ref, k_ref, v_ref, gc_ref, gr_ref, h_ref) in enumerate(
            ((qf_ref, kf_ref, vf_ref, gcf_ref, grf_ref, hf_ref),
             (qb_ref, kb_ref, vb_ref, gcb_ref, grb_ref, hb_ref))):
        gcol = gc_ref[...] + bc_ref[...]
        grow = gr_ref[...] + br_ref[...]
        lf_col = jax.nn.log_sigmoid(gcol)
        lf_row = jax.nn.log_sigmoid(grow)
        if d == 0:
            cum_col = jnp.dot(lower_f, lf_col, precision=hp, preferred_element_type=F32)
            cum_row = jnp.dot(lf_row, upper_f, precision=hp, preferred_element_type=F32)
            mask = lower
        else:
            cum_col = jnp.dot(upper_f, lf_col, precision=hp, preferred_element_type=F32)
            cum_row = jnp.dot(lf_row, lower_f, precision=hp, preferred_element_type=F32)
            mask = upper
        for hd in range(M_HEADS):
            gi = 2 * d * M_HEADS + hd
            gf = gi + M_HEADS
            cols = slice(hd * M_HEAD_DIM, (hd + 1) * M_HEAD_DIM)
            b_row = cum_row[gf:gf + 1, :]
            ch = dict(st=d * M_HEADS + hd, mask=mask, h_ref=h_ref, cols=cols,
                      b_col=cum_col[:, gf:gf + 1],
                      r_row=grow[gi:gi + 1, :] - b_row,
                      b_tot=b_row[:, L - 1:L] if d == 0 else b_row[:, 0:1],
                      q=q_ref[:, cols], k=k_ref[:, cols],
                      v_ext=jnp.concatenate([v_ref[:, cols], ones], axis=1))
            ch["m_old"] = m_ref[ch["st"]:ch["st"] + 1, 0:1]
            ch["c_old"] = c_ref[ch["st"]]
            chains.append(ch)

    for ch in chains:
        ch["qk"] = lax.dot_general(ch["q"], ch["k"], (((1,), (1,)), ((), ())), preferred_element_type=F32)
        ch["qc"] = jnp.dot(ch["q"], ch["c_old"].astype(BF16), preferred_element_type=F32)

    for ch in chains:
        r = jnp.where(ch["mask"], ch["r_row"], -jnp.inf)
        m_col = jnp.maximum(jnp.max(r, axis=1, keepdims=True), ch["m_old"])
        ch["w"] = (jnp.exp(r - m_col) * ch["qk"]).astype(BF16)
        ch["dec"] = jnp.exp(ch["m_old"] - m_col)
        ch["floor"] = jnp.exp(-(ch["b_col"] + m_col))

    for ch in chains:
        ext = jnp.dot(ch["w"], ch["v_ext"], preferred_element_type=F32) + ch["dec"] * ch["qc"]
        num, den = ext[:, :M_HEAD_DIM], ext[:, M_HEAD_DIM:]
        h = num / jnp.maximum(jnp.abs(den), ch["floor"])
        ch["h_ref"][:, ch["cols"]] = h.astype(ch["h_ref"].dtype)

    for ch in chains:
        ws_row = ch["b_tot"] + ch["r_row"]
        m_new = jnp.maximum(ch["b_tot"] + ch["m_old"], jnp.max(ws_row, axis=1, keepdims=True))
        ch["a"] = jnp.exp(ch["b_tot"] + ch["m_old"] - m_new)
        w_row = jnp.exp(ws_row - m_new)
        ch["kw"] = (ch["k"].astype(F32).T * w_row).astype(BF16)
        ch["m_new"] = m_new

    for ch in chains:
        st = ch["st"]
        c_ref[st] = ch["a"] * ch["c_old"] + jnp.dot(ch["kw"], ch["v_ext"], preferred_element_type=F32)
        m_ref[st:st + 1, :] = jnp.broadcast_to(ch["m_new"], (1, LANES))


def _mlstm(m4, gate, gate_t, p, B, S):
    T = m4.shape[0]
    L = M_CHUNK
    nc = S // L
    fwd = lambda col: (lambda b, c: (b * nc + c, col))
    bwd = lambda col: (lambda b, c: (b * nc + nc - 1 - c, col))
    blk = lambda f: pl.BlockSpec((L, M_WIDTH), f)
    return pl.pallas_call(
        _mlstm_kernel,
        grid=(B, nc),
        in_specs=[
            blk(fwd(0)), blk(fwd(1)), blk(fwd(2)),
            blk(bwd(0)), blk(bwd(1)), blk(bwd(2)),
            pl.BlockSpec((L, N_GATE_PRE), fwd(0)),
            pl.BlockSpec((L, N_GATE_PRE), bwd(0)),
            pl.BlockSpec((N_GATE_PRE, L), lambda b, c: (0, b * nc + c)),
            pl.BlockSpec((N_GATE_PRE, L), lambda b, c: (0, b * nc + nc - 1 - c)),
            _const_spec((1, N_GATE_PRE)), _const_spec((N_GATE_PRE, 1)),
        ],
        out_specs=[blk(fwd(0)), blk(bwd(0))],
        out_shape=[jax.ShapeDtypeStruct((T, M_WIDTH), BF16)] * 2,
        scratch_shapes=[
            pltpu.VMEM((2 * M_HEADS, M_HEAD_DIM, 2 * M_HEAD_DIM), F32),
            pltpu.VMEM((2 * M_HEADS, LANES), F32),
        ],
        compiler_params=pltpu.CompilerParams(
            dimension_semantics=("parallel", "arbitrary"), vmem_limit_bytes=VMEM_LIMIT),
        name="mlstm",
    )(m4, m4, m4, m4, m4, m4, gate, gate, gate_t, gate_t, p["b_if_row"], p["b_if_col"])


def _merge_kernel(a_ref, hf_ref, hb_ref, so_ref, sg_ref, x_ref, gm_ref, wao_ref, wmo_ref, wout_ref, gp_ref, o_ref):
    hs = hf_ref[...].astype(F32) + hb_ref[...].astype(F32)
    parts = []
    for hd in range(M_HEADS):
        blk = hs[:, hd * M_HEAD_DIM:(hd + 1) * M_HEAD_DIM]
        ms = jnp.mean(blk * blk, axis=-1, keepdims=True)
        parts.append(blk * lax.rsqrt(ms + EPS))
    hn = jnp.concatenate(parts, axis=1) * gm_ref[...]
    hm = (hn * so_ref[...].astype(F32)).astype(BF16)
    a_out = jnp.dot(a_ref[...], wao_ref[...], preferred_element_type=F32)
    m_out = jnp.dot(hm, wmo_ref[...], preferred_element_type=F32)
    sg = sg_ref[...]
    merged = sg[:, :D_MODEL].astype(F32) * a_out + sg[:, D_MODEL:].astype(F32) * m_out
    y = jnp.dot(merged.astype(BF16), wout_ref[...], preferred_element_type=F32)
    ms = jnp.mean(y * y, axis=-1, keepdims=True)
    o_ref[...] = x_ref[...] + y * lax.rsqrt(ms + EPS) * gp_ref[...]


def _merge(a, hf, hb, m4, sg, x2, p):
    T = x2.shape[0]
    tm = TOKEN_TILE
    row = lambda i: (i, 0)
    return pl.pallas_call(
        _merge_kernel,
        grid=(T // tm,),
        in_specs=[
            pl.BlockSpec((tm, ATTN_WIDTH), row),
            pl.BlockSpec((tm, M_WIDTH), row),
            pl.BlockSpec((tm, M_WIDTH), row),
            pl.BlockSpec((tm, M_WIDTH), lambda i: (i, 3)),
            pl.BlockSpec((tm, 2 * D_MODEL), row),
            pl.BlockSpec((tm, D_MODEL), row),
            _const_spec((1, M_WIDTH)),
            _const_spec((ATTN_WIDTH, D_MODEL)), _const_spec((M_WIDTH, D_MODEL)),
            _const_spec((D_MODEL, D_MODEL)), _const_spec((1, D_MODEL)),
        ],
        out_specs=pl.BlockSpec((tm, D_MODEL), row),
        out_shape=jax.ShapeDtypeStruct((T, D_MODEL), F32),
        compiler_params=pltpu.CompilerParams(
            dimension_semantics=("parallel",), vmem_limit_bytes=VMEM_LIMIT),
        name="merge",
    )(a, hf, hb, m4, sg, x2, p["g_mlstm"], p["wao"], p["wmo"], p["wout"], p["g_mix_post"])


def _ffn_kernel(xp_ref, x_ref, xn_ref, g_ref, wg_ref, wv_ref, cw_ref, cb_ref, wd_ref, gp_ref, o_ref,
                h_ref, ge_ref, *, tiles_per_seq):
    tm = x_ref.shape[0]
    g = g_ref[...]

    def norm(x):
        ms = jnp.mean(x * x, axis=-1, keepdims=True)
        return (x * lax.rsqrt(ms + EPS) * g).astype(BF16)

    h_ref[0:HALO, :] = norm(xp_ref[...])
    h_ref[HALO:HALO + tm, :] = norm(x_ref[...])
    h_ref[HALO + tm:, :] = norm(xn_ref[...])
    pos = pl.program_id(0) % tiles_per_seq
    keep_prev = (pos != 0).astype(F32)
    keep_next = (pos != tiles_per_seq - 1).astype(F32)

    ge_ref[...] = jnp.dot(h_ref[...], wg_ref[...], preferred_element_type=F32)
    ge_ref[HALO - 1:HALO, :] = ge_ref[HALO - 1:HALO, :] * keep_prev
    ge_ref[HALO + tm:HALO + tm + 1, :] = ge_ref[HALO + tm:HALO + tm + 1, :] * keep_next
    val = jnp.dot(h_ref[HALO:HALO + tm, :], wv_ref[...], preferred_element_type=F32)
    cw = cw_ref[...]
    conv = (ge_ref[HALO - 1:HALO - 1 + tm, :] * cw[0:1]
            + ge_ref[HALO:HALO + tm, :] * cw[1:2]
            + ge_ref[HALO + 1:HALO + 1 + tm, :] * cw[2:3]
            + cb_ref[...])
    act = (jax.nn.gelu(conv, approximate=True) * val).astype(BF16)
    y = jnp.dot(act, wd_ref[...], preferred_element_type=F32)
    ms = jnp.mean(y * y, axis=-1, keepdims=True)
    o_ref[...] = x_ref[...] + y * lax.rsqrt(ms + EPS) * gp_ref[...]


def _ffn(x1, p, S):
    T = x1.shape[0]
    tm = FFN_TOKEN_TILE
    hpt = tm // HALO
    n_halo = T // HALO
    kern = functools.partial(_ffn_kernel, tiles_per_seq=S // tm)
    return pl.pallas_call(
        kern,
        grid=(T // tm,),
        in_specs=[
            pl.BlockSpec((HALO, D_MODEL), lambda i: (jnp.maximum(i * hpt - 1, 0), 0)),
            pl.BlockSpec((tm, D_MODEL), lambda i: (i, 0)),
            pl.BlockSpec((HALO, D_MODEL), lambda i: (jnp.minimum((i + 1) * hpt, n_halo - 1), 0)),
            _const_spec((1, D_MODEL)),
            _const_spec((D_MODEL, D_FF)), _const_spec((D_MODEL, D_FF)),
            _const_spec((8, D_FF)), _const_spec((1, D_FF)),
            _const_spec((D_FF, D_MODEL)), _const_spec((1, D_MODEL)),
        ],
        out_specs=pl.BlockSpec((tm, D_MODEL), lambda i: (i, 0)),
        out_shape=jax.ShapeDtypeStruct((T, D_MODEL), F32),
        scratch_shapes=[
            pltpu.VMEM((tm + 2 * HALO, D_MODEL), BF16),
            pltpu.VMEM((tm + 2 * HALO, D_FF), F32),
        ],
        compiler_params=pltpu.CompilerParams(
            dimension_semantics=("parallel",), vmem_limit_bytes=VMEM_LIMIT),
        name="ffn",
    )(x1, x1, x1, p["g_ffn_pre"], p["wup_g"], p["wup_v"], p["conv_w"], p["conv_b"], p["wdown"], p["g_ffn_post"])


def _rope_tables(S, q_gain, k_gain):
    rows = S // GRID_W
    r = jnp.repeat(jnp.arange(rows, dtype=F32), GRID_W)
    c = jnp.tile(jnp.arange(GRID_W, dtype=F32), rows)
    n_pairs_axis = HEAD_DIM // 4
    freq = ROPE_THETA ** (-jnp.arange(n_pairs_axis, dtype=F32) / n_pairs_axis)
    ang = jnp.concatenate([r[:, None] * freq, c[:, None] * freq], axis=-1)
    cos, sin = jnp.cos(ang), jnp.sin(ang)
    cc = jnp.concatenate([cos, cos], axis=-1)
    ss = jnp.concatenate([-sin, sin], axis=-1)

    def tables(gain, scale):
        g_half = jnp.concatenate([gain[0::2], gain[1::2]])
        g_swap = jnp.concatenate([gain[1::2], gain[0::2]])
        ta = cc * g_half * scale
        tb = ss * g_swap * scale
        return jnp.tile(ta, (1, LANES // HEAD_DIM)), jnp.tile(tb, (1, LANES // HEAD_DIM))

    tqa, tqb = tables(q_gain, LOG2E * HEAD_DIM ** -0.5)
    tka, tkb = tables(k_gain, 1.0)
    return tqa, tqb, tka, tkb


def _prepare(norm_mix_pre, w_in, b_if, q_norm, k_norm, mlstm_norm, w_attn_o, w_mlstm_o, w_out,
             norm_mix_post, norm_ffn_pre, w_up, conv_w, conv_b, w_down, norm_ffn_post, seq_lens):
    half = np.concatenate([np.arange(0, HEAD_DIM, 2), np.arange(1, HEAD_DIM, 2)])
    slots = [h for j in range(GQA_GROUP) for h in (j, j + GQA_GROUP)]
    q_cols = np.concatenate([h * HEAD_DIM + half for h in slots])
    k_cols = ATTN_WIDTH + np.concatenate([h * HEAD_DIM + half for h in range(N_KV_HEADS)])
    v_cols = ATTN_WIDTH + KV_WIDTH + np.arange(KV_WIDTH)
    o_rows = np.concatenate([h * HEAD_DIM + np.arange(HEAD_DIM) for h in slots])
    m0 = ATTN_WIDTH + 2 * KV_WIDTH
    g0 = m0 + 4 * M_WIDTH
    s0 = g0 + N_GATE_PRE

    p = {}
    p["wa"] = w_in[:, np.concatenate([q_cols, k_cols, v_cols])].astype(BF16)
    p["wm"] = w_in[:, m0:g0].astype(BF16)
    p["wg"] = jnp.pad(w_in[:, g0:s0], ((0, 0), (0, LANES - N_GATE_PRE))).astype(BF16)
    p["ws"] = w_in[:, s0:].astype(BF16)
    blk = np.arange(MXU_DIM) // HEAD_DIM
    p["bd"] = jnp.asarray((blk[:, None] == blk[None, :]).astype(np.float32) / HEAD_DIM, BF16)
    p["g_mix_pre"] = norm_mix_pre.reshape(1, D_MODEL)
    p["b_if_row"] = b_if.reshape(1, N_GATE_PRE)
    p["b_if_col"] = b_if.reshape(N_GATE_PRE, 1)
    p["g_mlstm"] = mlstm_norm.reshape(1, M_WIDTH)
    p["wao"] = w_attn_o[o_rows].astype(BF16)
    p["wmo"] = w_mlstm_o.astype(BF16)
    p["wout"] = w_out.astype(BF16)
    p["g_mix_post"] = norm_mix_post.reshape(1, D_MODEL)
    p["g_ffn_pre"] = norm_ffn_pre.reshape(1, D_MODEL)
    p["wup_g"] = w_up[:, :D_FF].astype(BF16)
    p["wup_v"] = w_up[:, D_FF:].astype(BF16)
    p["conv_w"] = jnp.pad(conv_w, ((0, 8 - CONV_W), (0, 0)))
    p["conv_b"] = conv_b.reshape(1, D_FF)
    p["wdown"] = w_down.astype(BF16)
    p["g_ffn_post"] = norm_ffn_post.reshape(1, D_MODEL)
    p["score_bound"] = (1.02 * LOG2E * HEAD_DIM ** 0.5) * jnp.max(jnp.abs(q_norm)) * jnp.max(jnp.abs(k_norm))
    p["tqa"], p["tqb"], p["tka"], p["tkb"] = {}, {}, {}, {}
    for S in seq_lens:
        p["tqa"][S], p["tqb"][S], p["tka"][S], p["tkb"][S] = _rope_tables(S, q_norm, k_norm)
    return p


def _trunk(x, p):
    B, S, _ = x.shape
    T = B * S
    x2 = x.reshape(T, D_MODEL)
    q, k, v, m4, gate, sg = _in_proj(x2, p, S)
    n_chunks = S // ATTN_KV_CHUNK
    vt = v.reshape(B, n_chunks, ATTN_KV_CHUNK, N_KV_HEADS, HEAD_DIM).transpose(0, 3, 1, 4, 2)
    vt = jnp.concatenate([vt, jnp.ones((B, N_KV_HEADS, n_chunks, V_ROWS - HEAD_DIM, ATTN_KV_CHUNK), BF16)], axis=3)
    a = lax.cond(p["score_bound"] <= MAX_UNSTABILISED_SCORE,
                 lambda: _attention(q, k, vt, B, S, False),
                 lambda: _attention(q, k, vt, B, S, True))
    hf, hb = _mlstm(m4, gate, gate.T, p, B, S)
    x1 = _merge(a, hf, hb, m4, sg, x2, p)
    y = _ffn(x1, p, S)
    return y.reshape(B, S, D_MODEL)


def kernel(x_prompt, x_sample, norm_mix_pre, w_in, b_if, q_norm, k_norm, mlstm_norm, w_attn_o, w_mlstm_o,
           w_out, norm_mix_post, norm_ffn_pre, w_up, conv_w, conv_b, w_down, norm_ffn_post):
    depth = w_in.shape[0]
    seq_lens = sorted({x_prompt.shape[1], x_sample.shape[1]})
    layers = [
        _prepare(norm_mix_pre[l], w_in[l], b_if[l], q_norm[l], k_norm[l], mlstm_norm[l], w_attn_o[l],
                 w_mlstm_o[l], w_out[l], norm_mix_post[l], norm_ffn_pre[l], w_up[l], conv_w[l], conv_b[l],
                 w_down[l], norm_ffn_post[l], seq_lens)
        for l in range(depth)
    ]

    def trunk(x):
        for p in layers:
            x = _trunk(x, p)
        return x

    return (trunk(x_prompt), trunk(x_sample))
```

```python
import functools

import numpy as np
import jax
import jax.numpy as jnp
from jax import lax
from jax.experimental import pallas as pl
from jax.experimental.pallas import tpu as pltpu

D_MODEL = 1024
GRID_W = 64
N_Q_HEADS = 8
N_KV_HEADS = 2
GQA_GROUP = N_Q_HEADS // N_KV_HEADS
HEAD_DIM = 64
ATTN_WIDTH = N_Q_HEADS * HEAD_DIM
KV_WIDTH = N_KV_HEADS * HEAD_DIM
ROPE_THETA = 10000.0
M_HEADS = 4
M_HEAD_DIM = 128
M_WIDTH = M_HEADS * M_HEAD_DIM
N_GATE_PRE = 4 * M_HEADS
D_FF = 2816
CONV_W = 3
EPS = 1e-6

LANES = 128
MXU_DIM = 256
VMEM_LIMIT = 56 * 1024 * 1024

TOKEN_TILE = 512
ATTN_Q_TILE = 512
ATTN_KV_CHUNK = 512
M_CHUNK = 128
FFN_TOKEN_TILE = 256
HALO = 16
LOG2E = 1.4426950408889634
MAX_UNSTABILISED_SCORE = 40.0

BF16 = jnp.bfloat16
F32 = jnp.float32


def _sigmoid(x):
    return 0.5 * jnp.tanh(0.5 * x) + 0.5


def _const_spec(shape):
    n = len(shape)
    return pl.BlockSpec(shape, lambda *_: (0,) * n, pipeline_mode=pl.Buffered(1))


def _in_proj_kernel(x_ref, g_ref, wa_ref, wm_ref, wg_ref, ws_ref, bd_ref,
                    tqa_ref, tqb_ref, tka_ref, tkb_ref,
                    q_ref, k_ref, v_ref, m_ref, gate_ref, sg_ref):
    x = x_ref[...]
    ms = jnp.mean(x * x, axis=-1, keepdims=True)
    h = (x * lax.rsqrt(ms + EPS) * g_ref[...]).astype(BF16)

    za = jnp.dot(h, wa_ref[...], preferred_element_type=F32)
    bd = bd_ref[...]
    lane = lax.broadcasted_iota(jnp.int32, (x.shape[0], LANES), 1)
    first_half = (lane % HEAD_DIM) < (HEAD_DIM // 2)

    def head_norm_rope(z, msq, ta, tb):
        sw = jnp.where(first_half, pltpu.roll(z, LANES - HEAD_DIM // 2, 1), pltpu.roll(z, HEAD_DIM // 2, 1))
        return lax.rsqrt(msq + EPS) * (z * ta + sw * tb)

    tqa, tqb = tqa_ref[...], tqb_ref[...]
    for half in range(ATTN_WIDTH // MXU_DIM):
        zq = za[:, half * MXU_DIM:(half + 1) * MXU_DIM]
        msq = jnp.dot((zq * zq).astype(BF16), bd, preferred_element_type=F32)
        for j in range(MXU_DIM // LANES):
            c0 = half * MXU_DIM + j * LANES
            out = head_norm_rope(zq[:, j * LANES:(j + 1) * LANES], msq[:, j * LANES:(j + 1) * LANES], tqa, tqb)
            q_ref[:, c0:c0 + LANES] = out.astype(BF16)
    zk = za[:, ATTN_WIDTH:ATTN_WIDTH + KV_WIDTH]
    msk = jnp.dot((zk * zk).astype(BF16), bd[:LANES, :LANES], preferred_element_type=F32)
    k_ref[...] = head_norm_rope(zk, msk, tka_ref[...], tkb_ref[...]).astype(BF16)
    v_ref[...] = za[:, ATTN_WIDTH + KV_WIDTH:].astype(BF16)

    zm = jnp.dot(h, wm_ref[...], preferred_element_type=F32)
    m_ref[:, 0:M_WIDTH] = zm[:, 0:M_WIDTH].astype(BF16)
    m_ref[:, M_WIDTH:2 * M_WIDTH] = (zm[:, M_WIDTH:2 * M_WIDTH] * (M_HEAD_DIM ** -0.5)).astype(BF16)
    m_ref[:, 2 * M_WIDTH:3 * M_WIDTH] = zm[:, 2 * M_WIDTH:3 * M_WIDTH].astype(BF16)
    m_ref[:, 3 * M_WIDTH:] = _sigmoid(zm[:, 3 * M_WIDTH:]).astype(BF16)

    zg = jnp.dot(h, wg_ref[...], preferred_element_type=F32)
    gate_ref[...] = zg[:, :N_GATE_PRE]

    zs = jnp.dot(h, ws_ref[...], preferred_element_type=F32)
    sg_ref[...] = _sigmoid(zs).astype(BF16)


def _in_proj(x2, p, S):
    T = x2.shape[0]
    tm = TOKEN_TILE
    spt = S // tm
    row = lambda i: (i, 0)
    pos = lambda i: (i % spt, 0)
    tab = pl.BlockSpec((tm, LANES), pos)
    return pl.pallas_call(
        _in_proj_kernel,
        grid=(T // tm,),
        in_specs=[
            pl.BlockSpec((tm, D_MODEL), row),
            _const_spec((1, D_MODEL)),
            _const_spec(p["wa"].shape), _const_spec(p["wm"].shape),
            _const_spec(p["wg"].shape), _const_spec(p["ws"].shape),
            _const_spec((MXU_DIM, MXU_DIM)),
            tab, tab, tab, tab,
        ],
        out_specs=[
            pl.BlockSpec((tm, ATTN_WIDTH), row),
            pl.BlockSpec((tm, KV_WIDTH), row),
            pl.BlockSpec((tm, KV_WIDTH), row),
            pl.BlockSpec((tm, 4 * M_WIDTH), row),
            pl.BlockSpec((tm, N_GATE_PRE), row),
            pl.BlockSpec((tm, 2 * D_MODEL), row),
        ],
        out_shape=[
            jax.ShapeDtypeStruct((T, ATTN_WIDTH), BF16),
            jax.ShapeDtypeStruct((T, KV_WIDTH), BF16),
            jax.ShapeDtypeStruct((T, KV_WIDTH), BF16),
            jax.ShapeDtypeStruct((T, 4 * M_WIDTH), BF16),
            jax.ShapeDtypeStruct((T, N_GATE_PRE), F32),
            jax.ShapeDtypeStruct((T, 2 * D_MODEL), BF16),
        ],
        compiler_params=pltpu.CompilerParams(
            dimension_semantics=("parallel",), vmem_limit_bytes=VMEM_LIMIT),
        name="in_proj",
    )(x2, p["g_mix_pre"], p["wa"], p["wm"], p["wg"], p["ws"], p["bd"],
      p["tqa"][S], p["tqb"][S], p["tka"][S], p["tkb"][S])


def _attn_kernel(q_ref, k_ref, vt_ref, o_ref, w_ref, p0_ref, p1_ref, acc_ref, l_ref, *, n_chunks, tq, kc, stabilise):
    row = lax.broadcasted_iota(jnp.int32, (LANES, tq), 0)
    low = row < HEAD_DIM
    for j in range(GQA_GROUP):
        qt = q_ref[:, j * LANES:(j + 1) * LANES].astype(F32).T
        w_ref[:, (2 * j) * tq:(2 * j + 1) * tq] = jnp.where(low, qt, 0.0).astype(BF16)
        w_ref[:, (2 * j + 1) * tq:(2 * j + 2) * tq] = jnp.where(low, 0.0, qt).astype(BF16)
    acc_ref[...] = jnp.zeros(acc_ref.shape, F32)
    l_ref[...] = jnp.zeros(l_ref.shape, F32)

    def step(c, ms, src=None, dst=None):
        new_ms = []
        if dst is not None:
            start = 0 if src is None else pl.multiple_of((c + 1) * kc, kc)
            kch = k_ref[pl.ds(start, kc), :]
        for slot in range(N_Q_HEADS):
            cols = slice(slot * tq, (slot + 1) * tq)
            if dst is not None:
                s = jnp.dot(kch, w_ref[:, cols], preferred_element_type=F32)
                if stabilise:
                    m = ms[slot][0]
                    mn = jnp.maximum(m, jnp.max(s, axis=0, keepdims=True))
                    new_ms.append((mn, jnp.exp2(m - mn)))
                    s = s - mn
                    l_ref[slot] = l_ref[slot] * new_ms[-1][1]
                pr = jnp.exp2(s)
                dst[:, cols] = pr.astype(BF16)
                l_ref[slot] += jnp.sum(pr.reshape(kc // 8, 8, tq), axis=0)
            if src is not None:
                pv = jnp.dot(vt_ref[slot % N_KV_HEADS, c], src[:, cols], preferred_element_type=F32)
                if stabilise:
                    acc_ref[slot] = acc_ref[slot] * ms[slot][1] + pv
                else:
                    acc_ref[slot] += pv
        return ms if dst is None else tuple(new_ms)

    def pair(j, ms):
        ms = step(2 * j, ms, src=p0_ref, dst=p1_ref)
        return step(2 * j + 1, ms, src=p1_ref, dst=p0_ref)

    neg_inf = jnp.full((1, tq), -jnp.inf, F32)
    state = tuple((neg_inf, neg_inf) for _ in range(N_Q_HEADS)) if stabilise else ()
    state = step(-1, state, dst=p0_ref)
    state = lax.fori_loop(0, n_chunks // 2 - 1, pair, state)
    state = step(n_chunks - 2, state, src=p0_ref, dst=p1_ref)
    step(n_chunks - 1, state, src=p1_ref)

    outs = []
    for slot in range(N_Q_HEADS):
        outs.append(acc_ref[slot] / jnp.sum(l_ref[slot], axis=0, keepdims=True))
    o_ref[...] = jnp.concatenate(outs, axis=0).T.astype(BF16)


def _attention(q, k, vt, B, S, stabilise):
    T = q.shape[0]
    tq, kc = ATTN_Q_TILE, ATTN_KV_CHUNK
    nq, n_chunks = S // tq, S // kc
    kern = functools.partial(_attn_kernel, n_chunks=n_chunks, tq=tq, kc=kc, stabilise=stabilise)
    return pl.pallas_call(
        kern,
        grid=(B, nq),
        in_specs=[
            pl.BlockSpec((tq, ATTN_WIDTH), lambda b, i: (b * nq + i, 0)),
            pl.BlockSpec((S, KV_WIDTH), lambda b, i: (b, 0)),
            pl.BlockSpec((None, N_KV_HEADS, n_chunks, HEAD_DIM, kc), lambda b, i: (b, 0, 0, 0, 0)),
        ],
        out_specs=pl.BlockSpec((tq, ATTN_WIDTH), lambda b, i: (b * nq + i, 0)),
        out_shape=jax.ShapeDtypeStruct((T, ATTN_WIDTH), BF16),
        scratch_shapes=[
            pltpu.VMEM((LANES, N_Q_HEADS * tq), BF16),
            pltpu.VMEM((kc, N_Q_HEADS * tq), BF16),
            pltpu.VMEM((kc, N_Q_HEADS * tq), BF16),
            pltpu.VMEM((N_Q_HEADS, HEAD_DIM, tq), F32),
            pltpu.VMEM((N_Q_HEADS, 8, tq), F32),
        ],
        compiler_params=pltpu.CompilerParams(
            dimension_semantics=("parallel", "arbitrary"), vmem_limit_bytes=VMEM_LIMIT),
        name="attn",
    )(q, k, vt)


def _mlstm_kernel(qf_ref, kf_ref, vf_ref, qb_ref, kb_ref, vb_ref,
                  gcf_ref, gcb_ref, grf_ref, grb_ref, bc_ref, br_ref,
                  hf_ref, hb_ref, c_ref, m_ref):
    L = qf_ref.shape[0]

    @pl.when(pl.program_id(1) == 0)
    def _():
        c_ref[...] = jnp.zeros(c_ref.shape, F32)
        m_ref[...] = jnp.zeros(m_ref.shape, F32)

    ri = lax.broadcasted_iota(jnp.int32, (L, L), 0)
    ci = lax.broadcasted_iota(jnp.int32, (L, L), 1)
    lower = ci <= ri
    upper = ci >= ri
    lower_f = lower.astype(F32)
    upper_f = upper.astype(F32)
    ones = jnp.ones((L, M_HEAD_DIM), BF16)
    hp = lax.Precision.HIGHEST

    chains = []
    for d, (q_ref, k_ref, v_ref, gc_ref, gr_ref, h_ref) in enumerate(
            ((qf_ref, kf_ref, vf_ref, gcf_ref, grf_ref, hf_ref),
             (qb_ref, kb_ref, vb_ref, gcb_ref, grb_ref, hb_ref))):
        gcol = gc_ref[...] + bc_ref[...]
        grow = gr_ref[...] + br_ref[...]
        lf_col = jax.nn.log_sigmoid(gcol)
        lf_row = jax.nn.log_sigmoid(grow)
        if d == 0:
            cum_col = jnp.dot(lower_f, lf_col, precision=hp, preferred_element_type=F32)
            cum_row = jnp.dot(lf_row, upper_f, precision=hp, preferred_element_type=F32)
            mask = lower
        else:
            cum_col = jnp.dot(upper_f, lf_col, precision=hp, preferred_element_type=F32)
            cum_row = jnp.dot(lf_row, lower_f, precision=hp, preferred_element_type=F32)
            mask = upper
        for hd in range(M_HEADS):
            gi = 2 * d * M_HEADS + hd
            gf = gi + M_HEADS
            cols = slice(hd * M_HEAD_DIM, (hd + 1) * M_HEAD_DIM)
            b_row = cum_row[gf:gf + 1, :]
            ch = dict(st=d * M_HEADS + hd, mask=mask, h_ref=h_ref, cols=cols,
                      b_col=cum_col[:, gf:gf + 1],
                      r_row=grow[gi:gi + 1, :] - b_row,
                      b_tot=b_row[:, L - 1:L] if d == 0 else b_row[:, 0:1],
                      q=q_ref[:, cols], k=k_ref[:, cols],
                      v_ext=jnp.concatenate([v_ref[:, cols], ones], axis=1))
            ch["m_old"] = m_ref[ch["st"]:ch["st"] + 1, 0:1]
            ch["c_old"] = c_ref[ch["st"]]
            chains.append(ch)

    for ch in chains:
        ch["qk"] = lax.dot_general(ch["q"], ch["k"], (((1,), (1,)), ((), ())), preferred_element_type=F32)
        ch["qc"] = jnp.dot(ch["q"], ch["c_old"].astype(BF16), preferred_element_type=F32)

    for ch in chains:
        r = jnp.where(ch["mask"], ch["r_row"], -jnp.inf)
        m_col = jnp.maximum(jnp.max(r, axis=1, keepdims=True), ch["m_old"])
        ch["w"] = (jnp.exp(r - m_col) * ch["qk"]).astype(BF16)
        ch["dec"] = jnp.exp(ch["m_old"] - m_col)
        ch["floor"] = jnp.exp(-(ch["b_col"] + m_col))

    for ch in chains:
        ext = jnp.dot(ch["w"], ch["v_ext"], preferred_element_type=F32) + ch["dec"] * ch["qc"]
        num, den = ext[:, :M_HEAD_DIM], ext[:, M_HEAD_DIM:]
        h = num / jnp.maximum(jnp.abs(den), ch["floor"])
        ch["h_ref"][:, ch["cols"]] = h.astype(ch["h_ref"].dtype)

    for ch in chains:
        ws_row = ch["b_tot"] + ch["r_row"]
        m_new = jnp.maximum(ch["b_tot"] + ch["m_old"], jnp.max(ws_row, axis=1, keepdims=True))
        ch["a"] = jnp.exp(ch["b_tot"] + ch["m_old"] - m_new)
        w_row = jnp.exp(ws_row - m_new)
        ch["kw"] = (ch["k"].astype(F32).T * w_row).astype(BF16)
        ch["m_new"] = m_new

    for ch in chains:
        st = ch["st"]
        c_ref[st] = ch["a"] * ch["c_old"] + jnp.dot(ch["kw"], ch["v_ext"], preferred_element_type=F32)
        m_ref[st:st + 1, :] = jnp.broadcast_to(ch["m_new"], (1, LANES))


def _mlstm(m4, gate, gate_t, p, B, S):
    T = m4.shape[0]
    L = M_CHUNK
    nc = S // L
    fwd = lambda col: (lambda b, c: (b * nc + c, col))
    bwd = lambda col: (lambda b, c: (b * nc + nc - 1 - c, col))
    blk = lambda f: pl.BlockSpec((L, M_WIDTH), f)
    return pl.pallas_call(
        _mlstm_kernel,
        grid=(B, nc),
        in_specs=[
            blk(fwd(0)), blk(fwd(1)), blk(fwd(2)),
            blk(bwd(0)), blk(bwd(1)), blk(bwd(2)),
            pl.BlockSpec((L, N_GATE_PRE), fwd(0)),
            pl.BlockSpec((L, N_GATE_PRE), bwd(0)),
            pl.BlockSpec((N_GATE_PRE, L), lambda b, c: (0, b * nc + c)),
            pl.BlockSpec((N_GATE_PRE, L), lambda b, c: (0, b * nc + nc - 1 - c)),
            _const_spec((1, N_GATE_PRE)), _const_spec((N_GATE_PRE, 1)),
        ],
        out_specs=[blk(fwd(0)), blk(bwd(0))],
        out_shape=[jax.ShapeDtypeStruct((T, M_WIDTH), BF16)] * 2,
        scratch_shapes=[
            pltpu.VMEM((2 * M_HEADS, M_HEAD_DIM, 2 * M_HEAD_DIM), F32),
            pltpu.VMEM((2 * M_HEADS, LANES), F32),
        ],
        compiler_params=pltpu.CompilerParams(
            dimension_semantics=("parallel", "arbitrary"), vmem_limit_bytes=VMEM_LIMIT),
        name="mlstm",
    )(m4, m4, m4, m4, m4, m4, gate, gate, gate_t, gate_t, p["b_if_row"], p["b_if_col"])


def _merge_kernel(a_ref, hf_ref, hb_ref, so_ref, sg_ref, x_ref, gm_ref, wao_ref, wmo_ref, wout_ref, gp_ref, o_ref):
    hs = hf_ref[...].astype(F32) + hb_ref[...].astype(F32)
    parts = []
    for hd in range(M_HEADS):
        blk = hs[:, hd * M_HEAD_DIM:(hd + 1) * M_HEAD_DIM]
        ms = jnp.mean(blk * blk, axis=-1, keepdims=True)
        parts.append(blk * lax.rsqrt(ms + EPS))
    hn = jnp.concatenate(parts, axis=1) * gm_ref[...]
    hm = (hn * so_ref[...].astype(F32)).astype(BF16)
    a_out = jnp.dot(a_ref[...], wao_ref[...], preferred_element_type=F32)
    m_out = jnp.dot(hm, wmo_ref[...], preferred_element_type=F32)
    sg = sg_ref[...]
    merged = sg[:, :D_MODEL].astype(F32) * a_out + sg[:, D_MODEL:].astype(F32) * m_out
    y = jnp.dot(merged.astype(BF16), wout_ref[...], preferred_element_type=F32)
    ms = jnp.mean(y * y, axis=-1, keepdims=True)
    o_ref[...] = x_ref[...] + y * lax.rsqrt(ms + EPS) * gp_ref[...]


def _merge(a, hf, hb, m4, sg, x2, p):
    T = x2.shape[0]
    tm = TOKEN_TILE
    row = lambda i: (i, 0)
    return pl.pallas_call(
        _merge_kernel,
        grid=(T // tm,),
        in_specs=[
            pl.BlockSpec((tm, ATTN_WIDTH), row),
            pl.BlockSpec((tm, M_WIDTH), row),
            pl.BlockSpec((tm, M_WIDTH), row),
            pl.BlockSpec((tm, M_WIDTH), lambda i: (i, 3)),
            pl.BlockSpec((tm, 2 * D_MODEL), row),
            pl.BlockSpec((tm, D_MODEL), row),
            _const_spec((1, M_WIDTH)),
            _const_spec((ATTN_WIDTH, D_MODEL)), _const_spec((M_WIDTH, D_MODEL)),
            _const_spec((D_MODEL, D_MODEL)), _const_spec((1, D_MODEL)),
        ],
        out_specs=pl.BlockSpec((tm, D_MODEL), row),
        out_shape=jax.ShapeDtypeStruct((T, D_MODEL), F32),
        compiler_params=pltpu.CompilerParams(
            dimension_semantics=("parallel",), vmem_limit_bytes=VMEM_LIMIT),
        name="merge",
    )(a, hf, hb, m4, sg, x2, p["g_mlstm"], p["wao"], p["wmo"], p["wout"], p["g_mix_post"])


def _ffn_kernel(xp_ref, x_ref, xn_ref, g_ref, wg_ref, wv_ref, cw_ref, cb_ref, wd_ref, gp_ref, o_ref,
                h_ref, ge_ref, *, tiles_per_seq):
    tm = x_ref.shape[0]
    g = g_ref[...]

    def norm(x):
        ms = jnp.mean(x * x, axis=-1, keepdims=True)
        return (x * lax.rsqrt(ms + EPS) * g).astype(BF16)

    h_ref[0:HALO, :] = norm(xp_ref[...])
    h_ref[HALO:HALO + tm, :] = norm(x_ref[...])
    h_ref[HALO + tm:, :] = norm(xn_ref[...])
    pos = pl.program_id(0) % tiles_per_seq
    keep_prev = (pos != 0).astype(F32)
    keep_next = (pos != tiles_per_seq - 1).astype(F32)

    ge_ref[...] = jnp.dot(h_ref[...], wg_ref[...], preferred_element_type=F32)
    ge_ref[HALO - 1:HALO, :] = ge_ref[HALO - 1:HALO, :] * keep_prev
    ge_ref[HALO + tm:HALO + tm + 1, :] = ge_ref[HALO + tm:HALO + tm + 1, :] * keep_next
    val = jnp.dot(h_ref[HALO:HALO + tm, :], wv_ref[...], preferred_element_type=F32)
    cw = cw_ref[...]
    conv = (ge_ref[HALO - 1:HALO - 1 + tm, :] * cw[0:1]
            + ge_ref[HALO:HALO + tm, :] * cw[1:2]
            + ge_ref[HALO + 1:HALO + 1 + tm, :] * cw[2:3]
            + cb_ref[...])
    act = (jax.nn.gelu(conv, approximate=True) * val).astype(BF16)
    y = jnp.dot(act, wd_ref[...], preferred_element_type=F32)
    ms = jnp.mean(y * y, axis=-1, keepdims=True)
    o_ref[...] = x_ref[...] + y * lax.rsqrt(ms + EPS) * gp_ref[...]


def _ffn(x1, p, S):
    T = x1.shape[0]
    tm = FFN_TOKEN_TILE
    hpt = tm // HALO
    n_halo = T // HALO
    kern = functools.partial(_ffn_kernel, tiles_per_seq=S // tm)
    return pl.pallas_call(
        kern,
        grid=(T // tm,),
        in_specs=[
            pl.BlockSpec((HALO, D_MODEL), lambda i: (jnp.maximum(i * hpt - 1, 0), 0)),
            pl.BlockSpec((tm, D_MODEL), lambda i: (i, 0)),
            pl.BlockSpec((HALO, D_MODEL), lambda i: (jnp.minimum((i + 1) * hpt, n_halo - 1), 0)),
            _const_spec((1, D_MODEL)),
            _const_spec((D_MODEL, D_FF)), _const_spec((D_MODEL, D_FF)),
            _const_spec((8, D_FF)), _const_spec((1, D_FF)),
            _const_spec((D_FF, D_MODEL)), _const_spec((1, D_MODEL)),
        ],
        out_specs=pl.BlockSpec((tm, D_MODEL), lambda i: (i, 0)),
        out_shape=jax.ShapeDtypeStruct((T, D_MODEL), F32),
        scratch_shapes=[
            pltpu.VMEM((tm + 2 * HALO, D_MODEL), BF16),
            pltpu.VMEM((tm + 2 * HALO, D_FF), F32),
        ],
        compiler_params=pltpu.CompilerParams(
            dimension_semantics=("parallel",), vmem_limit_bytes=VMEM_LIMIT),
        name="ffn",
    )(x1, x1, x1, p["g_ffn_pre"], p["wup_g"], p["wup_v"], p["conv_w"], p["conv_b"], p["wdown"], p["g_ffn_post"])


def _rope_tables(S, q_gain, k_gain):
    rows = S // GRID_W
    r = jnp.repeat(jnp.arange(rows, dtype=F32), GRID_W)
    c = jnp.tile(jnp.arange(GRID_W, dtype=F32), rows)
    n_pairs_axis = HEAD_DIM // 4
    freq = ROPE_THETA ** (-jnp.arange(n_pairs_axis, dtype=F32) / n_pairs_axis)
    ang = jnp.concatenate([r[:, None] * freq, c[:, None] * freq], axis=-1)
    cos, sin = jnp.cos(ang), jnp.sin(ang)
    cc = jnp.concatenate([cos, cos], axis=-1)
    ss = jnp.concatenate([-sin, sin], axis=-1)

    def tables(gain, scale):
        g_half = jnp.concatenate([gain[0::2], gain[1::2]])
        g_swap = jnp.concatenate([gain[1::2], gain[0::2]])
        ta = cc * g_half * scale
        tb = ss * g_swap * scale
        return jnp.tile(ta, (1, LANES // HEAD_DIM)), jnp.tile(tb, (1, LANES // HEAD_DIM))

    tqa, tqb = tables(q_gain, LOG2E * HEAD_DIM ** -0.5)
    tka, tkb = tables(k_gain, 1.0)
    return tqa, tqb, tka, tkb


def _prepare(norm_mix_pre, w_in, b_if, q_norm, k_norm, mlstm_norm, w_attn_o, w_mlstm_o, w_out,
             norm_mix_post, norm_ffn_pre, w_up, conv_w, conv_b, w_down, norm_ffn_post, seq_lens):
    half = np.concatenate([np.arange(0, HEAD_DIM, 2), np.arange(1, HEAD_DIM, 2)])
    slots = [h for j in range(GQA_GROUP) for h in (j, j + GQA_GROUP)]
    q_cols = np.concatenate([h * HEAD_DIM + half for h in slots])
    k_cols = ATTN_WIDTH + np.concatenate([h * HEAD_DIM + half for h in range(N_KV_HEADS)])
    v_cols = ATTN_WIDTH + KV_WIDTH + np.arange(KV_WIDTH)
    o_rows = np.concatenate([h * HEAD_DIM + np.arange(HEAD_DIM) for h in slots])
    m0 = ATTN_WIDTH + 2 * KV_WIDTH
    g0 = m0 + 4 * M_WIDTH
    s0 = g0 + N_GATE_PRE

    p = {}
    p["wa"] = w_in[:, np.concatenate([q_cols, k_cols, v_cols])].astype(BF16)
    p["wm"] = w_in[:, m0:g0].astype(BF16)
    p["wg"] = jnp.pad(w_in[:, g0:s0], ((0, 0), (0, LANES - N_GATE_PRE))).astype(BF16)
    p["ws"] = w_in[:, s0:].astype(BF16)
    blk = np.arange(MXU_DIM) // HEAD_DIM
    p["bd"] = jnp.asarray((blk[:, None] == blk[None, :]).astype(np.float32) / HEAD_DIM, BF16)
    p["g_mix_pre"] = norm_mix_pre.reshape(1, D_MODEL)
    p["b_if_row"] = b_if.reshape(1, N_GATE_PRE)
    p["b_if_col"] = b_if.reshape(N_GATE_PRE, 1)
    p["g_mlstm"] = mlstm_norm.reshape(1, M_WIDTH)
    p["wao"] = w_attn_o[o_rows].astype(BF16)
    p["wmo"] = w_mlstm_o.astype(BF16)
    p["wout"] = w_out.astype(BF16)
    p["g_mix_post"] = norm_mix_post.reshape(1, D_MODEL)
    p["g_ffn_pre"] = norm_ffn_pre.reshape(1, D_MODEL)
    p["wup_g"] = w_up[:, :D_FF].astype(BF16)
    p["wup_v"] = w_up[:, D_FF:].astype(BF16)
    p["conv_w"] = jnp.pad(conv_w, ((0, 8 - CONV_W), (0, 0)))
    p["conv_b"] = conv_b.reshape(1, D_FF)
    p["wdown"] = w_down.astype(BF16)
    p["g_ffn_post"] = norm_ffn_post.reshape(1, D_MODEL)
    p["score_bound"] = (1.02 * LOG2E * HEAD_DIM ** 0.5) * jnp.max(jnp.abs(q_norm)) * jnp.max(jnp.abs(k_norm))
    p["tqa"], p["tqb"], p["tka"], p["tkb"] = {}, {}, {}, {}
    for S in seq_lens:
        p["tqa"][S], p["tqb"][S], p["tka"][S], p["tkb"][S] = _rope_tables(S, q_norm, k_norm)
    return p


def _trunk(x, p):
    B, S, _ = x.shape
    T = B * S
    x2 = x.reshape(T, D_MODEL)
    q, k, v, m4, gate, sg = _in_proj(x2, p, S)
    n_chunks = S // ATTN_KV_CHUNK
    vt = v.reshape(B, n_chunks, ATTN_KV_CHUNK, N_KV_HEADS, HEAD_DIM).transpose(0, 3, 1, 4, 2)
    a = lax.cond(p["score_bound"] <= MAX_UNSTABILISED_SCORE,
                 lambda: _attention(q, k, vt, B, S, False),
                 lambda: _attention(q, k, vt, B, S, True))
    hf, hb = _mlstm(m4, gate, gate.T, p, B, S)
    x1 = _merge(a, hf, hb, m4, sg, x2, p)
    y = _ffn(x1, p, S)
    return y.reshape(B, S, D_MODEL)


def kernel(x_prompt, x_sample, norm_mix_pre, w_in, b_if, q_norm, k_norm, mlstm_norm, w_attn_o, w_mlstm_o,
           w_out, norm_mix_post, norm_ffn_pre, w_up, conv_w, conv_b, w_down, norm_ffn_post):
    depth = w_in.shape[0]
    seq_lens = sorted({x_prompt.shape[1], x_sample.shape[1]})
    layers = [
        _prepare(norm_mix_pre[l], w_in[l], b_if[l], q_norm[l], k_norm[l], mlstm_norm[l], w_attn_o[l],
                 w_mlstm_o[l], w_out[l], norm_mix_post[l], norm_ffn_pre[l], w_up[l], conv_w[l], conv_b[l],
                 w_down[l], norm_ffn_post[l], seq_lens)
        for l in range(depth)
    ]

    def trunk(x):
        for p in layers:
            x = _trunk(x, p)
        return x

    return (trunk(x_prompt), trunk(x_sample))
```

```python
import functools

import numpy as np
import jax
import jax.numpy as jnp
from jax import lax
from jax.experimental import pallas as pl
from jax.experimental.pallas import tpu as pltpu

D_MODEL = 1024
GRID_W = 64
N_Q_HEADS = 8
N_KV_HEADS = 2
GQA_GROUP = N_Q_HEADS // N_KV_HEADS
HEAD_DIM = 64
ATTN_WIDTH = N_Q_HEADS * HEAD_DIM
KV_WIDTH = N_KV_HEADS * HEAD_DIM
ROPE_THETA = 10000.0
M_HEADS = 4
M_HEAD_DIM = 128
M_WIDTH = M_HEADS * M_HEAD_DIM
N_GATE_PRE = 4 * M_HEADS
D_FF = 2816
CONV_W = 3
EPS = 1e-6

LANES = 128
MXU_DIM = 256
VMEM_LIMIT = 56 * 1024 * 1024

TOKEN_TILE = 512
ATTN_Q_TILE = 512
ATTN_KV_CHUNK = 512
M_CHUNK = 128
FFN_TOKEN_TILE = 256
HALO = 16
LOG2E = 1.4426950408889634
MAX_UNSTABILISED_SCORE = 40.0

BF16 = jnp.bfloat16
F32 = jnp.float32


def _sigmoid(x):
    return 0.5 * jnp.tanh(0.5 * x) + 0.5


def _const_spec(shape):
    n = len(shape)
    return pl.BlockSpec(shape, lambda *_: (0,) * n, pipeline_mode=pl.Buffered(1))


def _in_proj_kernel(x_ref, g_ref, wa_ref, wm_ref, wg_ref, ws_ref, bd_ref,
                    tqa_ref, tqb_ref, tka_ref, tkb_ref,
                    q_ref, k_ref, v_ref, m_ref, gate_ref, sg_ref):
    x = x_ref[...]
    ms = jnp.mean(x * x, axis=-1, keepdims=True)
    h = (x * lax.rsqrt(ms + EPS) * g_ref[...]).astype(BF16)

    za = jnp.dot(h, wa_ref[...], preferred_element_type=F32)
    bd = bd_ref[...]
    lane = lax.broadcasted_iota(jnp.int32, (x.shape[0], LANES), 1)
    first_half = (lane % HEAD_DIM) < (HEAD_DIM // 2)

    def head_norm_rope(z, msq, ta, tb):
        sw = jnp.where(first_half, pltpu.roll(z, LANES - HEAD_DIM // 2, 1), pltpu.roll(z, HEAD_DIM // 2, 1))
        return lax.rsqrt(msq + EPS) * (z * ta + sw * tb)

    tqa, tqb = tqa_ref[...], tqb_ref[...]
    for half in range(ATTN_WIDTH // MXU_DIM):
        zq = za[:, half * MXU_DIM:(half + 1) * MXU_DIM]
        msq = jnp.dot((zq * zq).astype(BF16), bd, preferred_element_type=F32)
        for j in range(MXU_DIM // LANES):
            c0 = half * MXU_DIM + j * LANES
            out = head_norm_rope(zq[:, j * LANES:(j + 1) * LANES], msq[:, j * LANES:(j + 1) * LANES], tqa, tqb)
            q_ref[:, c0:c0 + LANES] = out.astype(BF16)
    zk = za[:, ATTN_WIDTH:ATTN_WIDTH + KV_WIDTH]
    msk = jnp.dot((zk * zk).astype(BF16), bd[:LANES, :LANES], preferred_element_type=F32)
    k_ref[...] = head_norm_rope(zk, msk, tka_ref[...], tkb_ref[...]).astype(BF16)
    v_ref[...] = za[:, ATTN_WIDTH + KV_WIDTH:].astype(BF16)

    zm = jnp.dot(h, wm_ref[...], preferred_element_type=F32)
    m_ref[:, 0:M_WIDTH] = zm[:, 0:M_WIDTH].astype(BF16)
    m_ref[:, M_WIDTH:2 * M_WIDTH] = (zm[:, M_WIDTH:2 * M_WIDTH] * (M_HEAD_DIM ** -0.5)).astype(BF16)
    m_ref[:, 2 * M_WIDTH:3 * M_WIDTH] = zm[:, 2 * M_WIDTH:3 * M_WIDTH].astype(BF16)
    m_ref[:, 3 * M_WIDTH:] = _sigmoid(zm[:, 3 * M_WIDTH:]).astype(BF16)

    zg = jnp.dot(h, wg_ref[...], preferred_element_type=F32)
    gate_ref[...] = zg[:, :N_GATE_PRE]

    zs = jnp.dot(h, ws_ref[...], preferred_element_type=F32)
    sg_ref[...] = _sigmoid(zs).astype(BF16)


def _in_proj(x2, p, S):
    T = x2.shape[0]
    tm = TOKEN_TILE
    spt = S // tm
    row = lambda i: (i, 0)
    pos = lambda i: (i % spt, 0)
    tab = pl.BlockSpec((tm, LANES), pos)
    return pl.pallas_call(
        _in_proj_kernel,
        grid=(T // tm,),
        in_specs=[
            pl.BlockSpec((tm, D_MODEL), row),
            _const_spec((1, D_MODEL)),
            _const_spec(p["wa"].shape), _const_spec(p["wm"].shape),
            _const_spec(p["wg"].shape), _const_spec(p["ws"].shape),
            _const_spec((MXU_DIM, MXU_DIM)),
            tab, tab, tab, tab,
        ],
        out_specs=[
            pl.BlockSpec((tm, ATTN_WIDTH), row),
            pl.BlockSpec((tm, KV_WIDTH), row),
            pl.BlockSpec((tm, KV_WIDTH), row),
            pl.BlockSpec((tm, 4 * M_WIDTH), row),
            pl.BlockSpec((tm, N_GATE_PRE), row),
            pl.BlockSpec((tm, 2 * D_MODEL), row),
        ],
        out_shape=[
            jax.ShapeDtypeStruct((T, ATTN_WIDTH), BF16),
            jax.ShapeDtypeStruct((T, KV_WIDTH), BF16),
            jax.ShapeDtypeStruct((T, KV_WIDTH), BF16),
            jax.ShapeDtypeStruct((T, 4 * M_WIDTH), BF16),
            jax.ShapeDtypeStruct((T, N_GATE_PRE), F32),
            jax.ShapeDtypeStruct((T, 2 * D_MODEL), BF16),
        ],
        compiler_params=pltpu.CompilerParams(
            dimension_semantics=("parallel",), vmem_limit_bytes=VMEM_LIMIT),
        name="in_proj",
    )(x2, p["g_mix_pre"], p["wa"], p["wm"], p["wg"], p["ws"], p["bd"],
      p["tqa"][S], p["tqb"][S], p["tka"][S], p["tkb"][S])


def _attn_kernel(q_ref, k_ref, vt_ref, o_ref, w_ref, p0_ref, p1_ref, acc_ref, l_ref, *, n_chunks, tq, kc, stabilise):
    row = lax.broadcasted_iota(jnp.int32, (LANES, tq), 0)
    low = row < HEAD_DIM
    for j in range(GQA_GROUP):
        qt = q_ref[:, j * LANES:(j + 1) * LANES].astype(F32).T
        w_ref[:, (2 * j) * tq:(2 * j + 1) * tq] = jnp.where(low, qt, 0.0).astype(BF16)
        w_ref[:, (2 * j + 1) * tq:(2 * j + 2) * tq] = jnp.where(low, 0.0, qt).astype(BF16)
    acc_ref[...] = jnp.zeros(acc_ref.shape, F32)
    l_ref[...] = jnp.zeros(l_ref.shape, F32)

    def step(c, ms, src=None, dst=None):
        new_ms = []
        if dst is not None:
            start = 0 if src is None else pl.multiple_of((c + 1) * kc, kc)
            kch = k_ref[pl.ds(start, kc), :]
        for slot in range(N_Q_HEADS):
            cols = slice(slot * tq, (slot + 1) * tq)
            if dst is not None:
                s = jnp.dot(kch, w_ref[:, cols], preferred_element_type=F32)
                if stabilise:
                    m = ms[slot][0]
                    mn = jnp.maximum(m, jnp.max(s, axis=0, keepdims=True))
                    new_ms.append((mn, jnp.exp2(m - mn)))
                    s = s - mn
                    l_ref[slot] = l_ref[slot] * new_ms[-1][1]
                pr = jnp.exp2(s)
                dst[:, cols] = pr.astype(BF16)
                l_ref[slot] += jnp.sum(pr.reshape(kc // 8, 8, tq), axis=0)
            if src is not None:
                pv = jnp.dot(vt_ref[slot % N_KV_HEADS, c], src[:, cols], preferred_element_type=F32)
                if stabilise:
                    acc_ref[slot] = acc_ref[slot] * ms[slot][1] + pv
                else:
                    acc_ref[slot] += pv
        return ms if dst is None else tuple(new_ms)

    def pair(j, ms):
        ms = step(2 * j, ms, src=p0_ref, dst=p1_ref)
        return step(2 * j + 1, ms, src=p1_ref, dst=p0_ref)

    neg_inf = jnp.full((1, tq), -jnp.inf, F32)
    state = tuple((neg_inf, neg_inf) for _ in range(N_Q_HEADS)) if stabilise else ()
    state = step(-1, state, dst=p0_ref)
    state = lax.fori_loop(0, n_chunks // 2 - 1, pair, state)
    state = step(n_chunks - 2, state, src=p0_ref, dst=p1_ref)
    step(n_chunks - 1, state, src=p1_ref)

    outs = []
    for slot in range(N_Q_HEADS):
        outs.append(acc_ref[slot] / jnp.sum(l_ref[slot], axis=0, keepdims=True))
    o_ref[...] = jnp.concatenate(outs, axis=0).T.astype(BF16)


def _attention(q, k, vt, B, S, stabilise):
    T = q.shape[0]
    tq, kc = ATTN_Q_TILE, ATTN_KV_CHUNK
    nq, n_chunks = S // tq, S // kc
    kern = functools.partial(_attn_kernel, n_chunks=n_chunks, tq=tq, kc=kc, stabilise=stabilise)
    return pl.pallas_call(
        kern,
        grid=(B, nq),
        in_specs=[
            pl.BlockSpec((tq, ATTN_WIDTH), lambda b, i: (b * nq + i, 0)),
            pl.BlockSpec((S, KV_WIDTH), lambda b, i: (b, 0)),
            pl.BlockSpec((None, N_KV_HEADS, n_chunks, HEAD_DIM, kc), lambda b, i: (b, 0, 0, 0, 0)),
        ],
        out_specs=pl.BlockSpec((tq, ATTN_WIDTH), lambda b, i: (b * nq + i, 0)),
        out_shape=jax.ShapeDtypeStruct((T, ATTN_WIDTH), BF16),
        scratch_shapes=[
            pltpu.VMEM((LANES, N_Q_HEADS * tq), BF16),
            pltpu.VMEM((kc, N_Q_HEADS * tq), BF16),
            pltpu.VMEM((kc, N_Q_HEADS * tq), BF16),
            pltpu.VMEM((N_Q_HEADS, HEAD_DIM, tq), F32),
            pltpu.VMEM((N_Q_HEADS, 8, tq), F32),
        ],
        compiler_params=pltpu.CompilerParams(
            dimension_semantics=("parallel", "arbitrary"), vmem_limit_bytes=VMEM_LIMIT),
        name="attn",
    )(q, k, vt)


def _mlstm_kernel(qf_ref, kf_ref, vf_ref, qb_ref, kb_ref, vb_ref,
                  gcf_ref, gcb_ref, grf_ref, grb_ref, bc_ref, br_ref,
                  hf_ref, hb_ref, c_ref, m_ref):
    L = qf_ref.shape[0]

    @pl.when(pl.program_id(1) == 0)
    def _():
        c_ref[...] = jnp.zeros(c_ref.shape, F32)
        m_ref[...] = jnp.zeros(m_ref.shape, F32)

    ri = lax.broadcasted_iota(jnp.int32, (L, L), 0)
    ci = lax.broadcasted_iota(jnp.int32, (L, L), 1)
    lower = ci <= ri
    upper = ci >= ri
    lower_f = lower.astype(F32)
    upper_f = upper.astype(F32)
    ones = jnp.ones((L, M_HEAD_DIM), BF16)
    hp = lax.Precision.HIGHEST

    dirs = ((qf_ref, kf_ref, vf_ref, gcf_ref, grf_ref, hf_ref, lower),
            (qb_ref, kb_ref, vb_ref, gcb_ref, grb_ref, hb_ref, upper))
    chains = []
    for d, (q_ref, k_ref, v_ref, _, _, h_ref, mask) in enumerate(dirs):
        for hd in range(M_HEADS):
            cols = slice(hd * M_HEAD_DIM, (hd + 1) * M_HEAD_DIM)
            st = d * M_HEADS + hd
            ch = dict(st=st, d=d, hd=hd, mask=mask, h_ref=h_ref, cols=cols, q=q_ref[:, cols], k=k_ref[:, cols],
                      v_ext=jnp.concatenate([v_ref[:, cols], ones], axis=1),
                      m_old=m_ref[st:st + 1, 0:1],
                      c_old=c_ref[st])
            ch["qk"] = lax.dot_general(ch["q"], ch["k"], (((1,), (1,)), ((), ())), preferred_element_type=F32)
            ch["qc"] = jnp.dot(ch["q"], ch["c_old"].astype(BF16), preferred_element_type=F32)
            chains.append(ch)

    for d, (_, _, _, gc_ref, gr_ref, _, _) in enumerate(dirs):
        gcol = gc_ref[...] + bc_ref[...]
        grow = gr_ref[...] + br_ref[...]
        lf_col = jax.nn.log_sigmoid(gcol)
        lf_row = jax.nn.log_sigmoid(grow)
        if d == 0:
            cum_col = jnp.dot(lower_f, lf_col, precision=hp, preferred_element_type=F32)
            cum_row = jnp.dot(lf_row, upper_f, precision=hp, preferred_element_type=F32)
        else:
            cum_col = jnp.dot(upper_f, lf_col, precision=hp, preferred_element_type=F32)
            cum_row = jnp.dot(lf_row, lower_f, precision=hp, preferred_element_type=F32)
        for ch in chains[d * M_HEADS:(d + 1) * M_HEADS]:
            gi = 2 * d * M_HEADS + ch["hd"]
            gf = gi + M_HEADS
            b_row = cum_row[gf:gf + 1, :]
            ch["b_col"] = cum_col[:, gf:gf + 1]
            ch["r_row"] = grow[gi:gi + 1, :] - b_row
            ch["b_tot"] = b_row[:, L - 1:L] if d == 0 else b_row[:, 0:1]

    for ch in chains:
        ws_row = ch["b_tot"] + ch["r_row"]
        m_new = jnp.maximum(ch["b_tot"] + ch["m_old"], jnp.max(ws_row, axis=1, keepdims=True))
        a = jnp.exp(ch["b_tot"] + ch["m_old"] - m_new)
        w_row = jnp.exp(ws_row - m_new)
        kw = (ch["k"].astype(F32).T * w_row).astype(BF16)
        st = ch["st"]
        c_ref[st] = a * ch["c_old"] + jnp.dot(kw, ch["v_ext"], preferred_element_type=F32)
        m_ref[st:st + 1, :] = jnp.broadcast_to(m_new, (1, LANES))

    for ch in chains:
        r = jnp.where(ch["mask"], ch["r_row"], -jnp.inf)
        m_col = jnp.maximum(jnp.max(r, axis=1, keepdims=True), ch["m_old"])
        ch["w"] = (jnp.exp(r - m_col) * ch["qk"]).astype(BF16)
        ch["dec"] = jnp.exp(ch["m_old"] - m_col)
        ch["floor"] = jnp.exp(-(ch["b_col"] + m_col))

    for ch in chains:
        ext = jnp.dot(ch["w"], ch["v_ext"], preferred_element_type=F32) + ch["dec"] * ch["qc"]
        num, den = ext[:, :M_HEAD_DIM], ext[:, M_HEAD_DIM:]
        h = num / jnp.maximum(jnp.abs(den), ch["floor"])
        ch["h_ref"][:, ch["cols"]] = h.astype(ch["h_ref"].dtype)


def _mlstm(m4, gate, gate_t, p, B, S):
    T = m4.shape[0]
    L = M_CHUNK
    nc = S // L
    fwd = lambda col: (lambda b, c: (b * nc + c, col))
    bwd = lambda col: (lambda b, c: (b * nc + nc - 1 - c, col))
    blk = lambda f: pl.BlockSpec((L, M_WIDTH), f)
    return pl.pallas_call(
        _mlstm_kernel,
        grid=(B, nc),
        in_specs=[
            blk(fwd(0)), blk(fwd(1)), blk(fwd(2)),
            blk(bwd(0)), blk(bwd(1)), blk(bwd(2)),
            pl.BlockSpec((L, N_GATE_PRE), fwd(0)),
            pl.BlockSpec((L, N_GATE_PRE), bwd(0)),
            pl.BlockSpec((N_GATE_PRE, L), lambda b, c: (0, b * nc + c)),
            pl.BlockSpec((N_GATE_PRE, L), lambda b, c: (0, b * nc + nc - 1 - c)),
            _const_spec((1, N_GATE_PRE)), _const_spec((N_GATE_PRE, 1)),
        ],
        out_specs=[blk(fwd(0)), blk(bwd(0))],
        out_shape=[jax.ShapeDtypeStruct((T, M_WIDTH), BF16)] * 2,
        scratch_shapes=[
            pltpu.VMEM((2 * M_HEADS, M_HEAD_DIM, 2 * M_HEAD_DIM), F32),
            pltpu.VMEM((2 * M_HEADS, LANES), F32),
        ],
        compiler_params=pltpu.CompilerParams(
            dimension_semantics=("parallel", "arbitrary"), vmem_limit_bytes=VMEM_LIMIT),
        name="mlstm",
    )(m4, m4, m4, m4, m4, m4, gate, gate, gate_t, gate_t, p["b_if_row"], p["b_if_col"])


def _merge_kernel(a_ref, hf_ref, hb_ref, so_ref, sg_ref, x_ref, gm_ref, wao_ref, wmo_ref, wout_ref, gp_ref, o_ref):
    hs = hf_ref[...].astype(F32) + hb_ref[...].astype(F32)
    parts = []
    for hd in range(M_HEADS):
        blk = hs[:, hd * M_HEAD_DIM:(hd + 1) * M_HEAD_DIM]
        ms = jnp.mean(blk * blk, axis=-1, keepdims=True)
        parts.append(blk * lax.rsqrt(ms + EPS))
    hn = jnp.concatenate(parts, axis=1) * gm_ref[...]
    hm = (hn * so_ref[...].astype(F32)).astype(BF16)
    a_out = jnp.dot(a_ref[...], wao_ref[...], preferred_element_type=F32)
    m_out = jnp.dot(hm, wmo_ref[...], preferred_element_type=F32)
    sg = sg_ref[...]
    merged = sg[:, :D_MODEL].astype(F32) * a_out + sg[:, D_MODEL:].astype(F32) * m_out
    y = jnp.dot(merged.astype(BF16), wout_ref[...], preferred_element_type=F32)
    ms = jnp.mean(y * y, axis=-1, keepdims=True)
    o_ref[...] = x_ref[...] + y * lax.rsqrt(ms + EPS) * gp_ref[...]


def _merge(a, hf, hb, m4, sg, x2, p):
    T = x2.shape[0]
    tm = TOKEN_TILE
    row = lambda i: (i, 0)
    return pl.pallas_call(
        _merge_kernel,
        grid=(T // tm,),
        in_specs=[
            pl.BlockSpec((tm, ATTN_WIDTH), row),
            pl.BlockSpec((tm, M_WIDTH), row),
            pl.BlockSpec((tm, M_WIDTH), row),
            pl.BlockSpec((tm, M_WIDTH), lambda i: (i, 3)),
            pl.BlockSpec((tm, 2 * D_MODEL), row),
            pl.BlockSpec((tm, D_MODEL), row),
            _const_spec((1, M_WIDTH)),
            _const_spec((ATTN_WIDTH, D_MODEL)), _const_spec((M_WIDTH, D_MODEL)),
            _const_spec((D_MODEL, D_MODEL)), _const_spec((1, D_MODEL)),
        ],
        out_specs=pl.BlockSpec((tm, D_MODEL), row),
        out_shape=jax.ShapeDtypeStruct((T, D_MODEL), F32),
        compiler_params=pltpu.CompilerParams(
            dimension_semantics=("parallel",), vmem_limit_bytes=VMEM_LIMIT),
        name="merge",
    )(a, hf, hb, m4, sg, x2, p["g_mlstm"], p["wao"], p["wmo"], p["wout"], p["g_mix_post"])


def _ffn_kernel(xp_ref, x_ref, xn_ref, g_ref, wg_ref, wv_ref, cw_ref, cb_ref, wd_ref, gp_ref, o_ref,
                h_ref, ge_ref, *, tiles_per_seq):
    tm = x_ref.shape[0]
    g = g_ref[...]

    def norm(x):
        ms = jnp.mean(x * x, axis=-1, keepdims=True)
        return (x * lax.rsqrt(ms + EPS) * g).astype(BF16)

    h_ref[0:HALO, :] = norm(xp_ref[...])
    h_ref[HALO:HALO + tm, :] = norm(x_ref[...])
    h_ref[HALO + tm:, :] = norm(xn_ref[...])
    pos = pl.program_id(0) % tiles_per_seq
    keep_prev = (pos != 0).astype(F32)
    keep_next = (pos != tiles_per_seq - 1).astype(F32)

    ge_ref[...] = jnp.dot(h_ref[...], wg_ref[...], preferred_element_type=F32)
    ge_ref[HALO - 1:HALO, :] = ge_ref[HALO - 1:HALO, :] * keep_prev
    ge_ref[HALO + tm:HALO + tm + 1, :] = ge_ref[HALO + tm:HALO + tm + 1, :] * keep_next
    val = jnp.dot(h_ref[HALO:HALO + tm, :], wv_ref[...], preferred_element_type=F32)
    cw = cw_ref[...]
    conv = (ge_ref[HALO - 1:HALO - 1 + tm, :] * cw[0:1]
            + ge_ref[HALO:HALO + tm, :] * cw[1:2]
            + ge_ref[HALO + 1:HALO + 1 + tm, :] * cw[2:3]
            + cb_ref[...])
    act = (jax.nn.gelu(conv, approximate=True) * val).astype(BF16)
    y = jnp.dot(act, wd_ref[...], preferred_element_type=F32)
    ms = jnp.mean(y * y, axis=-1, keepdims=True)
    o_ref[...] = x_ref[...] + y * lax.rsqrt(ms + EPS) * gp_ref[...]


def _ffn(x1, p, S):
    T = x1.shape[0]
    tm = FFN_TOKEN_TILE
    hpt = tm // HALO
    n_halo = T // HALO
    kern = functools.partial(_ffn_kernel, tiles_per_seq=S // tm)
    return pl.pallas_call(
        kern,
        grid=(T // tm,),
        in_specs=[
            pl.BlockSpec((HALO, D_MODEL), lambda i: (jnp.maximum(i * hpt - 1, 0), 0)),
            pl.BlockSpec((tm, D_MODEL), lambda i: (i, 0)),
            pl.BlockSpec((HALO, D_MODEL), lambda i: (jnp.minimum((i + 1) * hpt, n_halo - 1), 0)),
            _const_spec((1, D_MODEL)),
            _const_spec((D_MODEL, D_FF)), _const_spec((D_MODEL, D_FF)),
            _const_spec((8, D_FF)), _const_spec((1, D_FF)),
            _const_spec((D_FF, D_MODEL)), _const_spec((1, D_MODEL)),
        ],
        out_specs=pl.BlockSpec((tm, D_MODEL), lambda i: (i, 0)),
        out_shape=jax.ShapeDtypeStruct((T, D_MODEL), F32),
        scratch_shapes=[
            pltpu.VMEM((tm + 2 * HALO, D_MODEL), BF16),
            pltpu.VMEM((tm + 2 * HALO, D_FF), F32),
        ],
        compiler_params=pltpu.CompilerParams(
            dimension_semantics=("parallel",), vmem_limit_bytes=VMEM_LIMIT),
        name="ffn",
    )(x1, x1, x1, p["g_ffn_pre"], p["wup_g"], p["wup_v"], p["conv_w"], p["conv_b"], p["wdown"], p["g_ffn_post"])


def _rope_tables(S, q_gain, k_gain):
    rows = S // GRID_W
    r = jnp.repeat(jnp.arange(rows, dtype=F32), GRID_W)
    c = jnp.tile(jnp.arange(GRID_W, dtype=F32), rows)
    n_pairs_axis = HEAD_DIM // 4
    freq = ROPE_THETA ** (-jnp.arange(n_pairs_axis, dtype=F32) / n_pairs_axis)
    ang = jnp.concatenate([r[:, None] * freq, c[:, None] * freq], axis=-1)
    cos, sin = jnp.cos(ang), jnp.sin(ang)
    cc = jnp.concatenate([cos, cos], axis=-1)
    ss = jnp.concatenate([-sin, sin], axis=-1)

    def tables(gain, scale):
        g_half = jnp.concatenate([gain[0::2], gain[1::2]])
        g_swap = jnp.concatenate([gain[1::2], gain[0::2]])
        ta = cc * g_half * scale
        tb = ss * g_swap * scale
        return jnp.tile(ta, (1, LANES // HEAD_DIM)), jnp.tile(tb, (1, LANES // HEAD_DIM))

    tqa, tqb = tables(q_gain, LOG2E * HEAD_DIM ** -0.5)
    tka, tkb = tables(k_gain, 1.0)
    return tqa, tqb, tka, tkb


def _prepare(norm_mix_pre, w_in, b_if, q_norm, k_norm, mlstm_norm, w_attn_o, w_mlstm_o, w_out,
             norm_mix_post, norm_ffn_pre, w_up, conv_w, conv_b, w_down, norm_ffn_post, seq_lens):
    half = np.concatenate([np.arange(0, HEAD_DIM, 2), np.arange(1, HEAD_DIM, 2)])
    slots = [h for j in range(GQA_GROUP) for h in (j, j + GQA_GROUP)]
    q_cols = np.concatenate([h * HEAD_DIM + half for h in slots])
    k_cols = ATTN_WIDTH + np.concatenate([h * HEAD_DIM + half for h in range(N_KV_HEADS)])
    v_cols = ATTN_WIDTH + KV_WIDTH + np.arange(KV_WIDTH)
    o_rows = np.concatenate([h * HEAD_DIM + np.arange(HEAD_DIM) for h in slots])
    m0 = ATTN_WIDTH + 2 * KV_WIDTH
    g0 = m0 + 4 * M_WIDTH
    s0 = g0 + N_GATE_PRE

    p = {}
    p["wa"] = w_in[:, np.concatenate([q_cols, k_cols, v_cols])].astype(BF16)
    p["wm"] = w_in[:, m0:g0].astype(BF16)
    p["wg"] = jnp.pad(w_in[:, g0:s0], ((0, 0), (0, LANES - N_GATE_PRE))).astype(BF16)
    p["ws"] = w_in[:, s0:].astype(BF16)
    blk = np.arange(MXU_DIM) // HEAD_DIM
    p["bd"] = jnp.asarray((blk[:, None] == blk[None, :]).astype(np.float32) / HEAD_DIM, BF16)
    p["g_mix_pre"] = norm_mix_pre.reshape(1, D_MODEL)
    p["b_if_row"] = b_if.reshape(1, N_GATE_PRE)
    p["b_if_col"] = b_if.reshape(N_GATE_PRE, 1)
    p["g_mlstm"] = mlstm_norm.reshape(1, M_WIDTH)
    p["wao"] = w_attn_o[o_rows].astype(BF16)
    p["wmo"] = w_mlstm_o.astype(BF16)
    p["wout"] = w_out.astype(BF16)
    p["g_mix_post"] = norm_mix_post.reshape(1, D_MODEL)
    p["g_ffn_pre"] = norm_ffn_pre.reshape(1, D_MODEL)
    p["wup_g"] = w_up[:, :D_FF].astype(BF16)
    p["wup_v"] = w_up[:, D_FF:].astype(BF16)
    p["conv_w"] = jnp.pad(conv_w, ((0, 8 - CONV_W), (0, 0)))
    p["conv_b"] = conv_b.reshape(1, D_FF)
    p["wdown"] = w_down.astype(BF16)
    p["g_ffn_post"] = norm_ffn_post.reshape(1, D_MODEL)
    p["score_bound"] = (1.02 * LOG2E * HEAD_DIM ** 0.5) * jnp.max(jnp.abs(q_norm)) * jnp.max(jnp.abs(k_norm))
    p["tqa"], p["tqb"], p["tka"], p["tkb"] = {}, {}, {}, {}
    for S in seq_lens:
        p["tqa"][S], p["tqb"][S], p["tka"][S], p["tkb"][S] = _rope_tables(S, q_norm, k_norm)
    return p


def _trunk(x, p):
    B, S, _ = x.shape
    T = B * S
    x2 = x.reshape(T, D_MODEL)
    q, k, v, m4, gate, sg = _in_proj(x2, p, S)
    n_chunks = S // ATTN_KV_CHUNK
    vt = v.reshape(B, n_chunks, ATTN_KV_CHUNK, N_KV_HEADS, HEAD_DIM).transpose(0, 3, 1, 4, 2)
    a = lax.cond(p["score_bound"] <= MAX_UNSTABILISED_SCORE,
                 lambda: _attention(q, k, vt, B, S, False),
                 lambda: _attention(q, k, vt, B, S, True))
    hf, hb = _mlstm(m4, gate, gate.T, p, B, S)
    x1 = _merge(a, hf, hb, m4, sg, x2, p)
    y = _ffn(x1, p, S)
    return y.reshape(B, S, D_MODEL)


def kernel(x_prompt, x_sample, norm_mix_pre, w_in, b_if, q_norm, k_norm, mlstm_norm, w_attn_o, w_mlstm_o,
           w_out, norm_mix_post, norm_ffn_pre, w_up, conv_w, conv_b, w_down, norm_ffn_post):
    depth = w_in.shape[0]
    seq_lens = sorted({x_prompt.shape[1], x_sample.shape[1]})
    layers = [
        _prepare(norm_mix_pre[l], w_in[l], b_if[l], q_norm[l], k_norm[l], mlstm_norm[l], w_attn_o[l],
                 w_mlstm_o[l], w_out[l], norm_mix_post[l], norm_ffn_pre[l], w_up[l], conv_w[l], conv_b[l],
                 w_down[l], norm_ffn_post[l], seq_lens)
        for l in range(depth)
    ]

    def trunk(x):
        for p in layers:
            x = _trunk(x, p)
        return x

    return (trunk(x_prompt), trunk(x_sample))
```

```python
import functools

import numpy as np
import jax
import jax.numpy as jnp
from jax import lax
from jax.experimental import pallas as pl
from jax.experimental.pallas import tpu as pltpu

D_MODEL = 1024
GRID_W = 64
N_Q_HEADS = 8
N_KV_HEADS = 2
GQA_GROUP = N_Q_HEADS // N_KV_HEADS
HEAD_DIM = 64
ATTN_WIDTH = N_Q_HEADS * HEAD_DIM
KV_WIDTH = N_KV_HEADS * HEAD_DIM
ROPE_THETA = 10000.0
M_HEADS = 4
M_HEAD_DIM = 128
M_WIDTH = M_HEADS * M_HEAD_DIM
N_GATE_PRE = 4 * M_HEADS
D_FF = 2816
CONV_W = 3
EPS = 1e-6

LANES = 128
MXU_DIM = 256
VMEM_LIMIT = 56 * 1024 * 1024

TOKEN_TILE = 512
ATTN_Q_TILE = 512
ATTN_KV_CHUNK = 256
M_CHUNK = 128
FFN_TOKEN_TILE = 256
HALO = 16
LOG2E = 1.4426950408889634
MAX_UNSTABILISED_SCORE = 40.0

BF16 = jnp.bfloat16
F32 = jnp.float32


def _sigmoid(x):
    return 0.5 * jnp.tanh(0.5 * x) + 0.5


def _const_spec(shape):
    n = len(shape)
    return pl.BlockSpec(shape, lambda *_: (0,) * n, pipeline_mode=pl.Buffered(1))


def _in_proj_kernel(x_ref, g_ref, wa_ref, wm_ref, wg_ref, ws_ref, bd_ref,
                    tqa_ref, tqb_ref, tka_ref, tkb_ref,
                    q_ref, k_ref, v_ref, m_ref, gate_ref, sg_ref):
    x = x_ref[...]
    ms = jnp.mean(x * x, axis=-1, keepdims=True)
    h = (x * lax.rsqrt(ms + EPS) * g_ref[...]).astype(BF16)

    za = jnp.dot(h, wa_ref[...], preferred_element_type=F32)
    bd = bd_ref[...]
    lane = lax.broadcasted_iota(jnp.int32, (x.shape[0], LANES), 1)
    first_half = (lane % HEAD_DIM) < (HEAD_DIM // 2)

    def head_norm_rope(z, msq, ta, tb):
        sw = jnp.where(first_half, pltpu.roll(z, LANES - HEAD_DIM // 2, 1), pltpu.roll(z, HEAD_DIM // 2, 1))
        return lax.rsqrt(msq + EPS) * (z * ta + sw * tb)

    tqa, tqb = tqa_ref[...], tqb_ref[...]
    for half in range(ATTN_WIDTH // MXU_DIM):
        zq = za[:, half * MXU_DIM:(half + 1) * MXU_DIM]
        msq = jnp.dot((zq * zq).astype(BF16), bd, preferred_element_type=F32)
        for j in range(MXU_DIM // LANES):
            c0 = half * MXU_DIM + j * LANES
            out = head_norm_rope(zq[:, j * LANES:(j + 1) * LANES], msq[:, j * LANES:(j + 1) * LANES], tqa, tqb)
            q_ref[:, c0:c0 + LANES] = out.astype(BF16)
    zk = za[:, ATTN_WIDTH:ATTN_WIDTH + KV_WIDTH]
    msk = jnp.dot((zk * zk).astype(BF16), bd[:LANES, :LANES], preferred_element_type=F32)
    k_ref[...] = head_norm_rope(zk, msk, tka_ref[...], tkb_ref[...]).astype(BF16)
    v_ref[...] = za[:, ATTN_WIDTH + KV_WIDTH:].astype(BF16)

    zm = jnp.dot(h, wm_ref[...], preferred_element_type=F32)
    m_ref[:, 0:M_WIDTH] = zm[:, 0:M_WIDTH].astype(BF16)
    m_ref[:, M_WIDTH:2 * M_WIDTH] = (zm[:, M_WIDTH:2 * M_WIDTH] * (M_HEAD_DIM ** -0.5)).astype(BF16)
    m_ref[:, 2 * M_WIDTH:3 * M_WIDTH] = zm[:, 2 * M_WIDTH:3 * M_WIDTH].astype(BF16)
    m_ref[:, 3 * M_WIDTH:] = _sigmoid(zm[:, 3 * M_WIDTH:]).astype(BF16)

    zg = jnp.dot(h, wg_ref[...], preferred_element_type=F32)
    gate_ref[...] = zg[:, :N_GATE_PRE]

    zs = jnp.dot(h, ws_ref[...], preferred_element_type=F32)
    sg_ref[...] = _sigmoid(zs).astype(BF16)


def _in_proj(x2, p, S):
    T = x2.shape[0]
    tm = TOKEN_TILE
    spt = S // tm
    row = lambda i: (i, 0)
    pos = lambda i: (i % spt, 0)
    tab = pl.BlockSpec((tm, LANES), pos)
    return pl.pallas_call(
        _in_proj_kernel,
        grid=(T // tm,),
        in_specs=[
            pl.BlockSpec((tm, D_MODEL), row),
            _const_spec((1, D_MODEL)),
            _const_spec(p["wa"].shape), _const_spec(p["wm"].shape),
            _const_spec(p["wg"].shape), _const_spec(p["ws"].shape),
            _const_spec((MXU_DIM, MXU_DIM)),
            tab, tab, tab, tab,
        ],
        out_specs=[
            pl.BlockSpec((tm, ATTN_WIDTH), row),
            pl.BlockSpec((tm, KV_WIDTH), row),
            pl.BlockSpec((tm, KV_WIDTH), row),
            pl.BlockSpec((tm, 4 * M_WIDTH), row),
            pl.BlockSpec((tm, N_GATE_PRE), row),
            pl.BlockSpec((tm, 2 * D_MODEL), row),
        ],
        out_shape=[
            jax.ShapeDtypeStruct((T, ATTN_WIDTH), BF16),
            jax.ShapeDtypeStruct((T, KV_WIDTH), BF16),
            jax.ShapeDtypeStruct((T, KV_WIDTH), BF16),
            jax.ShapeDtypeStruct((T, 4 * M_WIDTH), BF16),
            jax.ShapeDtypeStruct((T, N_GATE_PRE), F32),
            jax.ShapeDtypeStruct((T, 2 * D_MODEL), BF16),
        ],
        compiler_params=pltpu.CompilerParams(
            dimension_semantics=("parallel",), vmem_limit_bytes=VMEM_LIMIT),
        name="in_proj",
    )(x2, p["g_mix_pre"], p["wa"], p["wm"], p["wg"], p["ws"], p["bd"],
      p["tqa"][S], p["tqb"][S], p["tka"][S], p["tkb"][S])


def _mlstm_stages(qf_ref, kf_ref, vf_ref, qb_ref, kb_ref, vb_ref, gcf_ref, gcb_ref, grf_ref, grb_ref, bc_ref, br_ref,
                  hf_ref, hb_ref, c_ref, m_ref, n_steps):
    L = M_CHUNK
    ri = lax.broadcasted_iota(jnp.int32, (L, L), 0)
    ci = lax.broadcasted_iota(jnp.int32, (L, L), 1)
    lower = ci <= ri
    upper = ci >= ri
    lower_f = lower.astype(F32)
    upper_f = upper.astype(F32)
    ones = jnp.ones((L, M_HEAD_DIM), BF16)
    hp = lax.Precision.HIGHEST
    dirs = ((qf_ref, kf_ref, vf_ref, gcf_ref, grf_ref, hf_ref, lower),
            (qb_ref, kb_ref, vb_ref, gcb_ref, grb_ref, hb_ref, upper))

    for t in range(n_steps):
        rows = (slice(t * L, (t + 1) * L), slice((n_steps - 1 - t) * L, (n_steps - t) * L))
        chains = []
        for d, (q_ref, k_ref, v_ref, _, _, h_ref, mask) in enumerate(dirs):
            for hd in range(M_HEADS):
                cols = slice(hd * M_HEAD_DIM, (hd + 1) * M_HEAD_DIM)
                st = d * M_HEADS + hd
                ch = dict(st=st, d=d, hd=hd, mask=mask, h_ref=h_ref, cols=cols,
                          q=q_ref[rows[d], cols], k=k_ref[rows[d], cols],
                          v_ext=jnp.concatenate([v_ref[rows[d], cols], ones], axis=1),
                          m_old=m_ref[st:st + 1, 0:1],
                          c_old=c_ref[st])
                ch["qk"] = lax.dot_general(ch["q"], ch["k"], (((1,), (1,)), ((), ())), preferred_element_type=F32)
                ch["qc"] = jnp.dot(ch["q"], ch["c_old"].astype(BF16), preferred_element_type=F32)
                chains.append(ch)
        yield

        for d, (_, _, _, gc_ref, gr_ref, _, _) in enumerate(dirs):
            gcol = gc_ref[rows[d], :] + bc_ref[...]
            grow = gr_ref[:, rows[d]] + br_ref[...]
            lf_col = jax.nn.log_sigmoid(gcol)
            lf_row = jax.nn.log_sigmoid(grow)
            if d == 0:
                cum_col = jnp.dot(lower_f, lf_col, precision=hp, preferred_element_type=F32)
                cum_row = jnp.dot(lf_row, upper_f, precision=hp, preferred_element_type=F32)
            else:
                cum_col = jnp.dot(upper_f, lf_col, precision=hp, preferred_element_type=F32)
                cum_row = jnp.dot(lf_row, lower_f, precision=hp, preferred_element_type=F32)
            for ch in chains[d * M_HEADS:(d + 1) * M_HEADS]:
                gi = 2 * d * M_HEADS + ch["hd"]
                gf = gi + M_HEADS
                b_row = cum_row[gf:gf + 1, :]
                ch["b_col"] = cum_col[:, gf:gf + 1]
                ch["r_row"] = grow[gi:gi + 1, :] - b_row
                ch["b_tot"] = b_row[:, L - 1:L] if d == 0 else b_row[:, 0:1]
        yield

        for ch in chains:
            ws_row = ch["b_tot"] + ch["r_row"]
            m_new = jnp.maximum(ch["b_tot"] + ch["m_old"], jnp.max(ws_row, axis=1, keepdims=True))
            a = jnp.exp(ch["b_tot"] + ch["m_old"] - m_new)
            w_row = jnp.exp(ws_row - m_new)
            kw = (ch["k"].astype(F32).T * w_row).astype(BF16)
            st = ch["st"]
            c_ref[st] = a * ch["c_old"] + jnp.dot(kw, ch["v_ext"], preferred_element_type=F32)
            m_ref[st:st + 1, :] = jnp.broadcast_to(m_new, (1, LANES))
        yield

        for ch in chains:
            r = jnp.where(ch["mask"], ch["r_row"], -jnp.inf)
            m_col = jnp.maximum(jnp.max(r, axis=1, keepdims=True), ch["m_old"])
            ch["w"] = (jnp.exp(r - m_col) * ch["qk"]).astype(BF16)
            ch["dec"] = jnp.exp(ch["m_old"] - m_col)
            ch["floor"] = jnp.exp(-(ch["b_col"] + m_col))
        yield

        for ch in chains:
            ext = jnp.dot(ch["w"], ch["v_ext"], preferred_element_type=F32) + ch["dec"] * ch["qc"]
            num, den = ext[:, :M_HEAD_DIM], ext[:, M_HEAD_DIM:]
            h = num / jnp.maximum(jnp.abs(den), ch["floor"])
            ch["h_ref"][rows[ch["d"]], ch["cols"]] = h.astype(ch["h_ref"].dtype)
        yield


MLSTM_STAGES_PER_CHUNK = 5


def _mixer_kernel(q_ref, k_ref, vt_ref,
                  qf_ref, kf_ref, vf_ref, qb_ref, kb_ref, vb_ref, gcf_ref, gcb_ref, grf_ref, grb_ref, bc_ref, br_ref,
                  o_ref, hf_ref, hb_ref,
                  w_ref, p0_ref, p1_ref, acc_ref, l_ref, c_ref, m_ref, *, n_chunks, tq, kc, stabilise):
    row = lax.broadcasted_iota(jnp.int32, (LANES, tq), 0)
    low = row < HEAD_DIM
    for j in range(GQA_GROUP):
        qt = q_ref[:, j * LANES:(j + 1) * LANES].astype(F32).T
        w_ref[:, (2 * j) * tq:(2 * j + 1) * tq] = jnp.where(low, qt, 0.0).astype(BF16)
        w_ref[:, (2 * j + 1) * tq:(2 * j + 2) * tq] = jnp.where(low, 0.0, qt).astype(BF16)
    acc_ref[...] = jnp.zeros(acc_ref.shape, F32)
    l_ref[...] = jnp.zeros(l_ref.shape, F32)

    @pl.when(pl.program_id(1) == 0)
    def _():
        c_ref[...] = jnp.zeros(c_ref.shape, F32)
        m_ref[...] = jnp.zeros(m_ref.shape, F32)

    def step(c, ms, src=None, dst=None):
        new_ms = []
        if dst is not None:
            start = (c + 1) * kc
            if not isinstance(start, int):
                start = pl.multiple_of(start, kc)
            kch = k_ref[pl.ds(start, kc), :]
        for slot in range(N_Q_HEADS):
            cols = slice(slot * tq, (slot + 1) * tq)
            if dst is not None:
                s = jnp.dot(kch, w_ref[:, cols], preferred_element_type=F32)
                if stabilise:
                    m = ms[slot][0]
                    mn = jnp.maximum(m, jnp.max(s, axis=0, keepdims=True))
                    new_ms.append((mn, jnp.exp2(m - mn)))
                    s = s - mn
                    l_ref[slot] = l_ref[slot] * new_ms[-1][1]
                pr = jnp.exp2(s)
                dst[:, cols] = pr.astype(BF16)
                l_ref[slot] += jnp.sum(pr.reshape(kc // 8, 8, tq), axis=0)
            if src is not None:
                pv = jnp.dot(vt_ref[slot % N_KV_HEADS, c], src[:, cols], preferred_element_type=F32)
                if stabilise:
                    acc_ref[slot] = acc_ref[slot] * ms[slot][1] + pv
                else:
                    acc_ref[slot] += pv
        return ms if dst is None else tuple(new_ms)

    mlstm_steps = tq // M_CHUNK
    mlstm = _mlstm_stages(qf_ref, kf_ref, vf_ref, qb_ref, kb_ref, vb_ref, gcf_ref, gcb_ref, grf_ref, grb_ref,
                          bc_ref, br_ref, hf_ref, hb_ref, c_ref, m_ref, mlstm_steps)
    if stabilise:
        neg_inf = jnp.full((1, tq), -jnp.inf, F32)
        state = step(-1, tuple((neg_inf, neg_inf) for _ in range(N_Q_HEADS)), dst=p0_ref)
        state = lax.fori_loop(
            0, n_chunks // 2 - 1,
            lambda j, ms: step(2 * j + 1, step(2 * j, ms, src=p0_ref, dst=p1_ref), src=p1_ref, dst=p0_ref), state)
        state = step(n_chunks - 2, state, src=p0_ref, dst=p1_ref)
        step(n_chunks - 1, state, src=p1_ref)
        for _ in mlstm:
            pass
    else:
        bufs = (p0_ref, p1_ref)
        steps = ([(-1, None, p0_ref)]
                 + [(c, bufs[c % 2], bufs[(c + 1) % 2]) for c in range(n_chunks - 1)]
                 + [(n_chunks - 1, bufs[(n_chunks - 1) % 2], None)])
        n_stages = mlstm_steps * MLSTM_STAGES_PER_CHUNK
        done = 0
        for idx, (c, src, dst) in enumerate(steps):
            step(c, (), src=src, dst=dst)
            while done < ((idx + 1) * n_stages) // len(steps):
                next(mlstm)
                done += 1

    outs = [acc_ref[slot] / jnp.sum(l_ref[slot], axis=0, keepdims=True) for slot in range(N_Q_HEADS)]
    o_ref[...] = jnp.concatenate(outs, axis=0).T.astype(BF16)


def _mixers(q, k, vt, m4, gate, gate_t, p, B, S, stabilise):
    T = q.shape[0]
    tq, kc = ATTN_Q_TILE, ATTN_KV_CHUNK
    nq, n_chunks = S // tq, S // kc
    kern = functools.partial(_mixer_kernel, n_chunks=n_chunks, tq=tq, kc=kc, stabilise=stabilise)
    fwd = lambda col: (lambda b, i: (b * nq + i, col))
    bwd = lambda col: (lambda b, i: (b * nq + nq - 1 - i, col))
    blk = lambda f: pl.BlockSpec((tq, M_WIDTH), f)
    return pl.pallas_call(
        kern,
        grid=(B, nq),
        in_specs=[
            pl.BlockSpec((tq, ATTN_WIDTH), fwd(0)),
            pl.BlockSpec((S, KV_WIDTH), lambda b, i: (b, 0)),
            pl.BlockSpec((None, N_KV_HEADS, n_chunks, HEAD_DIM, kc), lambda b, i: (b, 0, 0, 0, 0)),
            blk(fwd(0)), blk(fwd(1)), blk(fwd(2)),
            blk(bwd(0)), blk(bwd(1)), blk(bwd(2)),
            pl.BlockSpec((tq, N_GATE_PRE), fwd(0)),
            pl.BlockSpec((tq, N_GATE_PRE), bwd(0)),
            pl.BlockSpec((N_GATE_PRE, tq), lambda b, i: (0, b * nq + i)),
            pl.BlockSpec((N_GATE_PRE, tq), lambda b, i: (0, b * nq + nq - 1 - i)),
            _const_spec((1, N_GATE_PRE)), _const_spec((N_GATE_PRE, 1)),
        ],
        out_specs=[pl.BlockSpec((tq, ATTN_WIDTH), fwd(0)), blk(fwd(0)), blk(bwd(0))],
        out_shape=[jax.ShapeDtypeStruct((T, ATTN_WIDTH), BF16)] + [jax.ShapeDtypeStruct((T, M_WIDTH), BF16)] * 2,
        scratch_shapes=[
            pltpu.VMEM((LANES, N_Q_HEADS * tq), BF16),
            pltpu.VMEM((kc, N_Q_HEADS * tq), BF16),
            pltpu.VMEM((kc, N_Q_HEADS * tq), BF16),
            pltpu.VMEM((N_Q_HEADS, HEAD_DIM, tq), F32),
            pltpu.VMEM((N_Q_HEADS, 8, tq), F32),
            pltpu.VMEM((2 * M_HEADS, M_HEAD_DIM, 2 * M_HEAD_DIM), F32),
            pltpu.VMEM((2 * M_HEADS, LANES), F32),
        ],
        compiler_params=pltpu.CompilerParams(
            dimension_semantics=("parallel", "arbitrary"), vmem_limit_bytes=VMEM_LIMIT),
        name="mixers",
    )(q, k, vt, m4, m4, m4, m4, m4, m4, gate, gate, gate_t, gate_t, p["b_if_row"], p["b_if_col"])


def _merge_kernel(a_ref, hf_ref, hb_ref, so_ref, sg_ref, x_ref, gm_ref, wao_ref, wmo_ref, wout_ref, gp_ref, o_ref):
    hs = hf_ref[...].astype(F32) + hb_ref[...].astype(F32)
    parts = []
    for hd in range(M_HEADS):
        blk = hs[:, hd * M_HEAD_DIM:(hd + 1) * M_HEAD_DIM]
        ms = jnp.mean(blk * blk, axis=-1, keepdims=True)
        parts.append(blk * lax.rsqrt(ms + EPS))
    hn = jnp.concatenate(parts, axis=1) * gm_ref[...]
    hm = (hn * so_ref[...].astype(F32)).astype(BF16)
    a_out = jnp.dot(a_ref[...], wao_ref[...], preferred_element_type=F32)
    m_out = jnp.dot(hm, wmo_ref[...], preferred_element_type=F32)
    sg = sg_ref[...]
    merged = sg[:, :D_MODEL].astype(F32) * a_out + sg[:, D_MODEL:].astype(F32) * m_out
    y = jnp.dot(merged.astype(BF16), wout_ref[...], preferred_element_type=F32)
    ms = jnp.mean(y * y, axis=-1, keepdims=True)
    o_ref[...] = x_ref[...] + y * lax.rsqrt(ms + EPS) * gp_ref[...]


def _merge(a, hf, hb, m4, sg, x2, p):
    T = x2.shape[0]
    tm = TOKEN_TILE
    row = lambda i: (i, 0)
    return pl.pallas_call(
        _merge_kernel,
        grid=(T // tm,),
        in_specs=[
            pl.BlockSpec((tm, ATTN_WIDTH), row),
            pl.BlockSpec((tm, M_WIDTH), row),
            pl.BlockSpec((tm, M_WIDTH), row),
            pl.BlockSpec((tm, M_WIDTH), lambda i: (i, 3)),
            pl.BlockSpec((tm, 2 * D_MODEL), row),
            pl.BlockSpec((tm, D_MODEL), row),
            _const_spec((1, M_WIDTH)),
            _const_spec((ATTN_WIDTH, D_MODEL)), _const_spec((M_WIDTH, D_MODEL)),
            _const_spec((D_MODEL, D_MODEL)), _const_spec((1, D_MODEL)),
        ],
        out_specs=pl.BlockSpec((tm, D_MODEL), row),
        out_shape=jax.ShapeDtypeStruct((T, D_MODEL), F32),
        compiler_params=pltpu.CompilerParams(
            dimension_semantics=("parallel",), vmem_limit_bytes=VMEM_LIMIT),
        name="merge",
    )(a, hf, hb, m4, sg, x2, p["g_mlstm"], p["wao"], p["wmo"], p["wout"], p["g_mix_post"])


def _ffn_kernel(xp_ref, x_ref, xn_ref, g_ref, wg_ref, wv_ref, cw_ref, cb_ref, wd_ref, gp_ref, o_ref,
                h_ref, ge_ref, *, tiles_per_seq):
    tm = x_ref.shape[0]
    g = g_ref[...]

    def norm(x):
        ms = jnp.mean(x * x, axis=-1, keepdims=True)
        return (x * lax.rsqrt(ms + EPS) * g).astype(BF16)

    h_ref[0:HALO, :] = norm(xp_ref[...])
    h_ref[HALO:HALO + tm, :] = norm(x_ref[...])
    h_ref[HALO + tm:, :] = norm(xn_ref[...])
    pos = pl.program_id(0) % tiles_per_seq
    keep_prev = (pos != 0).astype(F32)
    keep_next = (pos != tiles_per_seq - 1).astype(F32)

    ge_ref[...] = jnp.dot(h_ref[...], wg_ref[...], preferred_element_type=F32)
    ge_ref[HALO - 1:HALO, :] = ge_ref[HALO - 1:HALO, :] * keep_prev
    ge_ref[HALO + tm:HALO + tm + 1, :] = ge_ref[HALO + tm:HALO + tm + 1, :] * keep_next
    val = jnp.dot(h_ref[HALO:HALO + tm, :], wv_ref[...], preferred_element_type=F32)
    cw = cw_ref[...]
    conv = (ge_ref[HALO - 1:HALO - 1 + tm, :] * cw[0:1]
            + ge_ref[HALO:HALO + tm, :] * cw[1:2]
            + ge_ref[HALO + 1:HALO + 1 + tm, :] * cw[2:3]
            + cb_ref[...])
    act = (jax.nn.gelu(conv, approximate=True) * val).astype(BF16)
    y = jnp.dot(act, wd_ref[...], preferred_element_type=F32)
    ms = jnp.mean(y * y, axis=-1, keepdims=True)
    o_ref[...] = x_ref[...] + y * lax.rsqrt(ms + EPS) * gp_ref[...]


def _ffn(x1, p, S):
    T = x1.shape[0]
    tm = FFN_TOKEN_TILE
    hpt = tm // HALO
    n_halo = T // HALO
    kern = functools.partial(_ffn_kernel, tiles_per_seq=S // tm)
    return pl.pallas_call(
        kern,
        grid=(T // tm,),
        in_specs=[
            pl.BlockSpec((HALO, D_MODEL), lambda i: (jnp.maximum(i * hpt - 1, 0), 0)),
            pl.BlockSpec((tm, D_MODEL), lambda i: (i, 0)),
            pl.BlockSpec((HALO, D_MODEL), lambda i: (jnp.minimum((i + 1) * hpt, n_halo - 1), 0)),
            _const_spec((1, D_MODEL)),
            _const_spec((D_MODEL, D_FF)), _const_spec((D_MODEL, D_FF)),
            _const_spec((8, D_FF)), _const_spec((1, D_FF)),
            _const_spec((D_FF, D_MODEL)), _const_spec((1, D_MODEL)),
        ],
        out_specs=pl.BlockSpec((tm, D_MODEL), lambda i: (i, 0)),
        out_shape=jax.ShapeDtypeStruct((T, D_MODEL), F32),
        scratch_shapes=[
            pltpu.VMEM((tm + 2 * HALO, D_MODEL), BF16),
            pltpu.VMEM((tm + 2 * HALO, D_FF), F32),
        ],
        compiler_params=pltpu.CompilerParams(
            dimension_semantics=("parallel",), vmem_limit_bytes=VMEM_LIMIT),
        name="ffn",
    )(x1, x1, x1, p["g_ffn_pre"], p["wup_g"], p["wup_v"], p["conv_w"], p["conv_b"], p["wdown"], p["g_ffn_post"])


def _rope_tables(S, q_gain, k_gain):
    rows = S // GRID_W
    r = jnp.repeat(jnp.arange(rows, dtype=F32), GRID_W)
    c = jnp.tile(jnp.arange(GRID_W, dtype=F32), rows)
    n_pairs_axis = HEAD_DIM // 4
    freq = ROPE_THETA ** (-jnp.arange(n_pairs_axis, dtype=F32) / n_pairs_axis)
    ang = jnp.concatenate([r[:, None] * freq, c[:, None] * freq], axis=-1)
    cos, sin = jnp.cos(ang), jnp.sin(ang)
    cc = jnp.concatenate([cos, cos], axis=-1)
    ss = jnp.concatenate([-sin, sin], axis=-1)

    def tables(gain, scale):
        g_half = jnp.concatenate([gain[0::2], gain[1::2]])
        g_swap = jnp.concatenate([gain[1::2], gain[0::2]])
        ta = cc * g_half * scale
        tb = ss * g_swap * scale
        return jnp.tile(ta, (1, LANES // HEAD_DIM)), jnp.tile(tb, (1, LANES // HEAD_DIM))

    tqa, tqb = tables(q_gain, LOG2E * HEAD_DIM ** -0.5)
    tka, tkb = tables(k_gain, 1.0)
    return tqa, tqb, tka, tkb


def _prepare(norm_mix_pre, w_in, b_if, q_norm, k_norm, mlstm_norm, w_attn_o, w_mlstm_o, w_out,
             norm_mix_post, norm_ffn_pre, w_up, conv_w, conv_b, w_down, norm_ffn_post, seq_lens):
    half = np.concatenate([np.arange(0, HEAD_DIM, 2), np.arange(1, HEAD_DIM, 2)])
    slots = [h for j in range(GQA_GROUP) for h in (j, j + GQA_GROUP)]
    q_cols = np.concatenate([h * HEAD_DIM + half for h in slots])
    k_cols = ATTN_WIDTH + np.concatenate([h * HEAD_DIM + half for h in range(N_KV_HEADS)])
    v_cols = ATTN_WIDTH + KV_WIDTH + np.arange(KV_WIDTH)
    o_rows = np.concatenate([h * HEAD_DIM + np.arange(HEAD_DIM) for h in slots])
    m0 = ATTN_WIDTH + 2 * KV_WIDTH
    g0 = m0 + 4 * M_WIDTH
    s0 = g0 + N_GATE_PRE

    p = {}
    p["wa"] = w_in[:, np.concatenate([q_cols, k_cols, v_cols])].astype(BF16)
    p["wm"] = w_in[:, m0:g0].astype(BF16)
    p["wg"] = jnp.pad(w_in[:, g0:s0], ((0, 0), (0, LANES - N_GATE_PRE))).astype(BF16)
    p["ws"] = w_in[:, s0:].astype(BF16)
    blk = np.arange(MXU_DIM) // HEAD_DIM
    p["bd"] = jnp.asarray((blk[:, None] == blk[None, :]).astype(np.float32) / HEAD_DIM, BF16)
    p["g_mix_pre"] = norm_mix_pre.reshape(1, D_MODEL)
    p["b_if_row"] = b_if.reshape(1, N_GATE_PRE)
    p["b_if_col"] = b_if.reshape(N_GATE_PRE, 1)
    p["g_mlstm"] = mlstm_norm.reshape(1, M_WIDTH)
    p["wao"] = w_attn_o[o_rows].astype(BF16)
    p["wmo"] = w_mlstm_o.astype(BF16)
    p["wout"] = w_out.astype(BF16)
    p["g_mix_post"] = norm_mix_post.reshape(1, D_MODEL)
    p["g_ffn_pre"] = norm_ffn_pre.reshape(1, D_MODEL)
    p["wup_g"] = w_up[:, :D_FF].astype(BF16)
    p["wup_v"] = w_up[:, D_FF:].astype(BF16)
    p["conv_w"] = jnp.pad(conv_w, ((0, 8 - CONV_W), (0, 0)))
    p["conv_b"] = conv_b.reshape(1, D_FF)
    p["wdown"] = w_down.astype(BF16)
    p["g_ffn_post"] = norm_ffn_post.reshape(1, D_MODEL)
    p["score_bound"] = (1.02 * LOG2E * HEAD_DIM ** 0.5) * jnp.max(jnp.abs(q_norm)) * jnp.max(jnp.abs(k_norm))
    p["tqa"], p["tqb"], p["tka"], p["tkb"] = {}, {}, {}, {}
    for S in seq_lens:
        p["tqa"][S], p["tqb"][S], p["tka"][S], p["tkb"][S] = _rope_tables(S, q_norm, k_norm)
    return p


def _trunk(x, p):
    B, S, _ = x.shape
    T = B * S
    x2 = x.reshape(T, D_MODEL)
    q, k, v, m4, gate, sg = _in_proj(x2, p, S)
    n_chunks = S // ATTN_KV_CHUNK
    vt = v.reshape(B, n_chunks, ATTN_KV_CHUNK, N_KV_HEADS, HEAD_DIM).transpose(0, 3, 1, 4, 2)
    gate_t = gate.T
    a, hf, hb = lax.cond(p["score_bound"] <= MAX_UNSTABILISED_SCORE,
                         lambda: _mixers(q, k, vt, m4, gate, gate_t, p, B, S, False),
                         lambda: _mixers(q, k, vt, m4, gate, gate_t, p, B, S, True))
    x1 = _merge(a, hf, hb, m4, sg, x2, p)
    y = _ffn(x1, p, S)
    return y.reshape(B, S, D_MODEL)


def kernel(x_prompt, x_sample, norm_mix_pre, w_in, b_if, q_norm, k_norm, mlstm_norm, w_attn_o, w_mlstm_o,
           w_out, norm_mix_post, norm_ffn_pre, w_up, conv_w, conv_b, w_down, norm_ffn_post):
    depth = w_in.shape[0]
    seq_lens = sorted({x_prompt.shape[1], x_sample.shape[1]})
    layers = [
        _prepare(norm_mix_pre[l], w_in[l], b_if[l], q_norm[l], k_norm[l], mlstm_norm[l], w_attn_o[l],
                 w_mlstm_o[l], w_out[l], norm_mix_post[l], norm_ffn_pre[l], w_up[l], conv_w[l], conv_b[l],
                 w_down[l], norm_ffn_post[l], seq_lens)
        for l in range(depth)
    ]

    def trunk(x):
        for p in layers:
            x = _trunk(x, p)
        return x

    return (trunk(x_prompt), trunk(x_sample))
```

```python
import functools

import numpy as np
import jax
import jax.numpy as jnp
from jax import lax
from jax.experimental import pallas as pl
from jax.experimental.pallas import tpu as pltpu

D_MODEL = 1024
GRID_W = 64
N_Q_HEADS = 8
N_KV_HEADS = 2
GQA_GROUP = N_Q_HEADS // N_KV_HEADS
HEAD_DIM = 64
ATTN_WIDTH = N_Q_HEADS * HEAD_DIM
KV_WIDTH = N_KV_HEADS * HEAD_DIM
ROPE_THETA = 10000.0
M_HEADS = 4
M_HEAD_DIM = 128
M_WIDTH = M_HEADS * M_HEAD_DIM
N_GATE_PRE = 4 * M_HEADS
D_FF = 2816
CONV_W = 3
EPS = 1e-6

LANES = 128
MXU_DIM = 256
VMEM_LIMIT = 56 * 1024 * 1024

TOKEN_TILE = 512
ATTN_Q_TILE = 512
ATTN_KV_CHUNK = 256
M_CHUNK = 128
FFN_TOKEN_TILE = 256
HALO = 16
LOG2E = 1.4426950408889634
MAX_UNSTABILISED_SCORE = 40.0

BF16 = jnp.bfloat16
F32 = jnp.float32


def _sigmoid(x):
    return 0.5 * jnp.tanh(0.5 * x) + 0.5


def _const_spec(shape):
    n = len(shape)
    return pl.BlockSpec(shape, lambda *_: (0,) * n, pipeline_mode=pl.Buffered(1))


def _in_proj_kernel(x_ref, g_ref, wa_ref, wm_ref, wg_ref, ws_ref, bd_ref,
                    tqa_ref, tqb_ref, tka_ref, tkb_ref,
                    q_ref, k_ref, v_ref, m_ref, gate_ref, sg_ref):
    x = x_ref[...]
    ms = jnp.mean(x * x, axis=-1, keepdims=True)
    h = (x * lax.rsqrt(ms + EPS) * g_ref[...]).astype(BF16)

    za = jnp.dot(h, wa_ref[...], preferred_element_type=F32)
    bd = bd_ref[...]
    lane = lax.broadcasted_iota(jnp.int32, (x.shape[0], LANES), 1)
    first_half = (lane % HEAD_DIM) < (HEAD_DIM // 2)

    def head_norm_rope(z, msq, ta, tb):
        sw = jnp.where(first_half, pltpu.roll(z, LANES - HEAD_DIM // 2, 1), pltpu.roll(z, HEAD_DIM // 2, 1))
        return lax.rsqrt(msq + EPS) * (z * ta + sw * tb)

    tqa, tqb = tqa_ref[...], tqb_ref[...]
    for half in range(ATTN_WIDTH // MXU_DIM):
        zq = za[:, half * MXU_DIM:(half + 1) * MXU_DIM]
        msq = jnp.dot((zq * zq).astype(BF16), bd, preferred_element_type=F32)
        for j in range(MXU_DIM // LANES):
            c0 = half * MXU_DIM + j * LANES
            out = head_norm_rope(zq[:, j * LANES:(j + 1) * LANES], msq[:, j * LANES:(j + 1) * LANES], tqa, tqb)
            q_ref[:, c0:c0 + LANES] = out.astype(BF16)
    zk = za[:, ATTN_WIDTH:ATTN_WIDTH + KV_WIDTH]
    msk = jnp.dot((zk * zk).astype(BF16), bd[:LANES, :LANES], preferred_element_type=F32)
    k_ref[...] = head_norm_rope(zk, msk, tka_ref[...], tkb_ref[...]).astype(BF16)
    v_ref[...] = za[:, ATTN_WIDTH + KV_WIDTH:].astype(BF16)

    zm = jnp.dot(h, wm_ref[...], preferred_element_type=F32)
    m_ref[:, 0:M_WIDTH] = zm[:, 0:M_WIDTH].astype(BF16)
    m_ref[:, M_WIDTH:2 * M_WIDTH] = (zm[:, M_WIDTH:2 * M_WIDTH] * (M_HEAD_DIM ** -0.5)).astype(BF16)
    m_ref[:, 2 * M_WIDTH:3 * M_WIDTH] = zm[:, 2 * M_WIDTH:3 * M_WIDTH].astype(BF16)
    m_ref[:, 3 * M_WIDTH:] = _sigmoid(zm[:, 3 * M_WIDTH:]).astype(BF16)

    zg = jnp.dot(h, wg_ref[...], preferred_element_type=F32)
    gate_ref[...] = zg[:, :N_GATE_PRE]

    zs = jnp.dot(h, ws_ref[...], preferred_element_type=F32)
    sg_ref[...] = _sigmoid(zs).astype(BF16)


def _in_proj(x2, p, S):
    T = x2.shape[0]
    tm = TOKEN_TILE
    spt = S // tm
    row = lambda i: (i, 0)
    pos = lambda i: (i % spt, 0)
    tab = pl.BlockSpec((tm, LANES), pos)
    return pl.pallas_call(
        _in_proj_kernel,
        grid=(T // tm,),
        in_specs=[
            pl.BlockSpec((tm, D_MODEL), row),
            _const_spec((1, D_MODEL)),
            _const_spec(p["wa"].shape), _const_spec(p["wm"].shape),
            _const_spec(p["wg"].shape), _const_spec(p["ws"].shape),
            _const_spec((MXU_DIM, MXU_DIM)),
            tab, tab, tab, tab,
        ],
        out_specs=[
            pl.BlockSpec((tm, ATTN_WIDTH), row),
            pl.BlockSpec((tm, KV_WIDTH), row),
            pl.BlockSpec((tm, KV_WIDTH), row),
            pl.BlockSpec((tm, 4 * M_WIDTH), row),
            pl.BlockSpec((tm, N_GATE_PRE), row),
            pl.BlockSpec((tm, 2 * D_MODEL), row),
        ],
        out_shape=[
            jax.ShapeDtypeStruct((T, ATTN_WIDTH), BF16),
            jax.ShapeDtypeStruct((T, KV_WIDTH), BF16),
            jax.ShapeDtypeStruct((T, KV_WIDTH), BF16),
            jax.ShapeDtypeStruct((T, 4 * M_WIDTH), BF16),
            jax.ShapeDtypeStruct((T, N_GATE_PRE), F32),
            jax.ShapeDtypeStruct((T, 2 * D_MODEL), BF16),
        ],
        compiler_params=pltpu.CompilerParams(
            dimension_semantics=("parallel",), vmem_limit_bytes=VMEM_LIMIT),
        name="in_proj",
    )(x2, p["g_mix_pre"], p["wa"], p["wm"], p["wg"], p["ws"], p["bd"],
      p["tqa"][S], p["tqb"][S], p["tka"][S], p["tkb"][S])


def _mlstm_stages(qf_ref, kf_ref, vf_ref, qb_ref, kb_ref, vb_ref, gcf_ref, gcb_ref, grf_ref, grb_ref, bc_ref, br_ref,
                  hf_ref, hb_ref, c_ref, m_ref, n_steps):
    L = M_CHUNK
    ri = lax.broadcasted_iota(jnp.int32, (L, L), 0)
    ci = lax.broadcasted_iota(jnp.int32, (L, L), 1)
    lower = ci <= ri
    upper = ci >= ri
    lower_b = lower.astype(BF16)
    upper_b = upper.astype(BF16)
    ones = jnp.ones((L, M_HEAD_DIM), BF16)
    fwd_cols = lax.broadcasted_iota(jnp.int32, (L, N_GATE_PRE), 1) < N_GATE_PRE // 2
    fwd_rows = lax.broadcasted_iota(jnp.int32, (N_GATE_PRE, L), 0) < N_GATE_PRE // 2

    def split_dot(tri, x, tri_first):
        hi = x.astype(BF16)
        lo = (x - hi.astype(F32)).astype(BF16)
        if tri_first:
            return jnp.dot(tri, hi, preferred_element_type=F32) + jnp.dot(tri, lo, preferred_element_type=F32)
        return jnp.dot(hi, tri, preferred_element_type=F32) + jnp.dot(lo, tri, preferred_element_type=F32)

    dirs = ((qf_ref, kf_ref, vf_ref, gcf_ref, grf_ref, hf_ref, lower),
            (qb_ref, kb_ref, vb_ref, gcb_ref, grb_ref, hb_ref, upper))

    for t in range(n_steps):
        rows = (slice(t * L, (t + 1) * L), slice((n_steps - 1 - t) * L, (n_steps - t) * L))
        chains = []
        for d, (q_ref, k_ref, v_ref, _, _, h_ref, mask) in enumerate(dirs):
            for hd in range(M_HEADS):
                cols = slice(hd * M_HEAD_DIM, (hd + 1) * M_HEAD_DIM)
                st = d * M_HEADS + hd
                ch = dict(st=st, d=d, hd=hd, mask=mask, h_ref=h_ref, cols=cols,
                          q=q_ref[rows[d], cols], k=k_ref[rows[d], cols],
                          v_ext=jnp.concatenate([v_ref[rows[d], cols], ones], axis=1),
                          m_old=m_ref[st:st + 1, 0:1],
                          c_old=c_ref[st])
                ch["qk"] = lax.dot_general(ch["q"], ch["k"], (((1,), (1,)), ((), ())), preferred_element_type=F32)
                ch["qc"] = jnp.dot(ch["q"], ch["c_old"].astype(BF16), preferred_element_type=F32)
                chains.append(ch)
        yield

        gcol = jnp.where(fwd_cols, gcf_ref[rows[0], :], gcb_ref[rows[1], :]) + bc_ref[...]
        grow = jnp.where(fwd_rows, grf_ref[:, rows[0]], grb_ref[:, rows[1]]) + br_ref[...]
        lf_col = jax.nn.log_sigmoid(gcol)
        lf_row = jax.nn.log_sigmoid(grow)
        pre_col = split_dot(lower_b, lf_col, True)
        pre_row = split_dot(upper_b, lf_row, False)
        tot_col = pre_col[L - 1:L, :]
        tot_row = pre_row[:, L - 1:L]
        for ch in chains:
            gi = 2 * ch["d"] * M_HEADS + ch["hd"]
            gf = gi + M_HEADS
            b_row = pre_row[gf:gf + 1, :]
            b_col = pre_col[:, gf:gf + 1]
            if ch["d"] == 1:
                b_row = tot_row[gf:gf + 1, :] - b_row + lf_row[gf:gf + 1, :]
                b_col = tot_col[:, gf:gf + 1] - b_col + lf_col[:, gf:gf + 1]
            ch["b_col"] = b_col
            ch["r_row"] = grow[gi:gi + 1, :] - b_row
            ch["b_tot"] = tot_row[gf:gf + 1, :]
        yield

        for ch in chains:
            ws_row = ch["b_tot"] + ch["r_row"]
            m_new = jnp.maximum(ch["b_tot"] + ch["m_old"], jnp.max(ws_row, axis=1, keepdims=True))
            a = jnp.exp(ch["b_tot"] + ch["m_old"] - m_new)
            w_row = jnp.exp(ws_row - m_new)
            kw = (ch["k"].astype(F32).T * w_row).astype(BF16)
            st = ch["st"]
            c_ref[st] = a * ch["c_old"] + jnp.dot(kw, ch["v_ext"], preferred_element_type=F32)
            m_ref[st:st + 1, :] = jnp.broadcast_to(m_new, (1, LANES))
        yield

        for ch in chains:
            r = jnp.where(ch["mask"], ch["r_row"], -jnp.inf)
            m_col = jnp.maximum(jnp.max(r, axis=1, keepdims=True), ch["m_old"])
            ch["w"] = (jnp.exp(r - m_col) * ch["qk"]).astype(BF16)
            ch["dec"] = jnp.exp(ch["m_old"] - m_col)
            ch["floor"] = jnp.exp(-(ch["b_col"] + m_col))
        yield

        for ch in chains:
            ext = jnp.dot(ch["w"], ch["v_ext"], preferred_element_type=F32) + ch["dec"] * ch["qc"]
            num, den = ext[:, :M_HEAD_DIM], ext[:, M_HEAD_DIM:]
            h = num / jnp.maximum(jnp.abs(den), ch["floor"])
            ch["h_ref"][rows[ch["d"]], ch["cols"]] = h.astype(ch["h_ref"].dtype)
        yield


MLSTM_STAGES_PER_CHUNK = 5


def _mixer_kernel(q_ref, k_ref, vt_ref,
                  qf_ref, kf_ref, vf_ref, qb_ref, kb_ref, vb_ref, gcf_ref, gcb_ref, grf_ref, grb_ref, bc_ref, br_ref,
                  o_ref, hf_ref, hb_ref,
                  w_ref, p0_ref, p1_ref, acc_ref, l_ref, c_ref, m_ref, *, n_chunks, tq, kc, stabilise):
    row = lax.broadcasted_iota(jnp.int32, (LANES, tq), 0)
    low = row < HEAD_DIM
    for j in range(GQA_GROUP):
        qt = q_ref[:, j * LANES:(j + 1) * LANES].astype(F32).T
        w_ref[:, (2 * j) * tq:(2 * j + 1) * tq] = jnp.where(low, qt, 0.0).astype(BF16)
        w_ref[:, (2 * j + 1) * tq:(2 * j + 2) * tq] = jnp.where(low, 0.0, qt).astype(BF16)
    acc_ref[...] = jnp.zeros(acc_ref.shape, F32)
    l_ref[...] = jnp.zeros(l_ref.shape, F32)

    @pl.when(pl.program_id(1) == 0)
    def _():
        c_ref[...] = jnp.zeros(c_ref.shape, F32)
        m_ref[...] = jnp.zeros(m_ref.shape, F32)

    def step(c, ms, src=None, dst=None):
        new_ms = []
        if dst is not None:
            start = (c + 1) * kc
            if not isinstance(start, int):
                start = pl.multiple_of(start, kc)
            kch = k_ref[pl.ds(start, kc), :]
        for slot in range(N_Q_HEADS):
            cols = slice(slot * tq, (slot + 1) * tq)
            if dst is not None:
                s = jnp.dot(kch, w_ref[:, cols], preferred_element_type=F32)
                if stabilise:
                    m = ms[slot][0]
                    mn = jnp.maximum(m, jnp.max(s, axis=0, keepdims=True))
                    new_ms.append((mn, jnp.exp2(m - mn)))
                    s = s - mn
                    l_ref[slot] = l_ref[slot] * new_ms[-1][1]
                pr = jnp.exp2(s)
                dst[:, cols] = pr.astype(BF16)
                l_ref[slot] += jnp.sum(pr.reshape(kc // 8, 8, tq), axis=0)
            if src is not None:
                pv = jnp.dot(vt_ref[slot % N_KV_HEADS, c], src[:, cols], preferred_element_type=F32)
                if stabilise:
                    acc_ref[slot] = acc_ref[slot] * ms[slot][1] + pv
                else:
                    acc_ref[slot] += pv
        return ms if dst is None else tuple(new_ms)

    mlstm_steps = tq // M_CHUNK
    mlstm = _mlstm_stages(qf_ref, kf_ref, vf_ref, qb_ref, kb_ref, vb_ref, gcf_ref, gcb_ref, grf_ref, grb_ref,
                          bc_ref, br_ref, hf_ref, hb_ref, c_ref, m_ref, mlstm_steps)
    if stabilise:
        neg_inf = jnp.full((1, tq), -jnp.inf, F32)
        state = step(-1, tuple((neg_inf, neg_inf) for _ in range(N_Q_HEADS)), dst=p0_ref)
        state = lax.fori_loop(
            0, n_chunks // 2 - 1,
            lambda j, ms: step(2 * j + 1, step(2 * j, ms, src=p0_ref, dst=p1_ref), src=p1_ref, dst=p0_ref), state)
        state = step(n_chunks - 2, state, src=p0_ref, dst=p1_ref)
        step(n_chunks - 1, state, src=p1_ref)
        for _ in mlstm:
            pass
    else:
        bufs = (p0_ref, p1_ref)
        steps = ([(-1, None, p0_ref)]
                 + [(c, bufs[c % 2], bufs[(c + 1) % 2]) for c in range(n_chunks - 1)]
                 + [(n_chunks - 1, bufs[(n_chunks - 1) % 2], None)])
        n_stages = mlstm_steps * MLSTM_STAGES_PER_CHUNK
        done = 0
        for idx, (c, src, dst) in enumerate(steps):
            step(c, (), src=src, dst=dst)
            while done < ((idx + 1) * n_stages) // len(steps):
                next(mlstm)
                done += 1

    outs = [acc_ref[slot] / jnp.sum(l_ref[slot], axis=0, keepdims=True) for slot in range(N_Q_HEADS)]
    o_ref[...] = jnp.concatenate(outs, axis=0).T.astype(BF16)


def _mixers(q, k, vt, m4, gate, gate_t, p, B, S, stabilise):
    T = q.shape[0]
    tq, kc = ATTN_Q_TILE, ATTN_KV_CHUNK
    nq, n_chunks = S // tq, S // kc
    kern = functools.partial(_mixer_kernel, n_chunks=n_chunks, tq=tq, kc=kc, stabilise=stabilise)
    fwd = lambda col: (lambda b, i: (b * nq + i, col))
    bwd = lambda col: (lambda b, i: (b * nq + nq - 1 - i, col))
    blk = lambda f: pl.BlockSpec((tq, M_WIDTH), f)
    return pl.pallas_call(
        kern,
        grid=(B, nq),
        in_specs=[
            pl.BlockSpec((tq, ATTN_WIDTH), fwd(0)),
            pl.BlockSpec((S, KV_WIDTH), lambda b, i: (b, 0)),
            pl.BlockSpec((None, N_KV_HEADS, n_chunks, HEAD_DIM, kc), lambda b, i: (b, 0, 0, 0, 0)),
            blk(fwd(0)), blk(fwd(1)), blk(fwd(2)),
            blk(bwd(0)), blk(bwd(1)), blk(bwd(2)),
            pl.BlockSpec((tq, N_GATE_PRE), fwd(0)),
            pl.BlockSpec((tq, N_GATE_PRE), bwd(0)),
            pl.BlockSpec((N_GATE_PRE, tq), lambda b, i: (0, b * nq + i)),
            pl.BlockSpec((N_GATE_PRE, tq), lambda b, i: (0, b * nq + nq - 1 - i)),
            _const_spec((1, N_GATE_PRE)), _const_spec((N_GATE_PRE, 1)),
        ],
        out_specs=[pl.BlockSpec((tq, ATTN_WIDTH), fwd(0)), blk(fwd(0)), blk(bwd(0))],
        out_shape=[jax.ShapeDtypeStruct((T, ATTN_WIDTH), BF16)] + [jax.ShapeDtypeStruct((T, M_WIDTH), BF16)] * 2,
        scratch_shapes=[
            pltpu.VMEM((LANES, N_Q_HEADS * tq), BF16),
            pltpu.VMEM((kc, N_Q_HEADS * tq), BF16),
            pltpu.VMEM((kc, N_Q_HEADS * tq), BF16),
            pltpu.VMEM((N_Q_HEADS, HEAD_DIM, tq), F32),
            pltpu.VMEM((N_Q_HEADS, 8, tq), F32),
            pltpu.VMEM((2 * M_HEADS, M_HEAD_DIM, 2 * M_HEAD_DIM), F32),
            pltpu.VMEM((2 * M_HEADS, LANES), F32),
        ],
        compiler_params=pltpu.CompilerParams(
            dimension_semantics=("parallel", "arbitrary"), vmem_limit_bytes=VMEM_LIMIT),
        name="mixers",
    )(q, k, vt, m4, m4, m4, m4, m4, m4, gate, gate, gate_t, gate_t, p["b_if_row"], p["b_if_col"])


def _merge_kernel(a_ref, hf_ref, hb_ref, so_ref, sg_ref, x_ref, gm_ref, wao_ref, wmo_ref, wout_ref, gp_ref, o_ref):
    hs = hf_ref[...].astype(F32) + hb_ref[...].astype(F32)
    parts = []
    for hd in range(M_HEADS):
        blk = hs[:, hd * M_HEAD_DIM:(hd + 1) * M_HEAD_DIM]
        ms = jnp.mean(blk * blk, axis=-1, keepdims=True)
        parts.append(blk * lax.rsqrt(ms + EPS))
    hn = jnp.concatenate(parts, axis=1) * gm_ref[...]
    hm = (hn * so_ref[...].astype(F32)).astype(BF16)
    a_out = jnp.dot(a_ref[...], wao_ref[...], preferred_element_type=F32)
    m_out = jnp.dot(hm, wmo_ref[...], preferred_element_type=F32)
    sg = sg_ref[...]
    merged = sg[:, :D_MODEL].astype(F32) * a_out + sg[:, D_MODEL:].astype(F32) * m_out
    y = jnp.dot(merged.astype(BF16), wout_ref[...], preferred_element_type=F32)
    ms = jnp.mean(y * y, axis=-1, keepdims=True)
    o_ref[...] = x_ref[...] + y * lax.rsqrt(ms + EPS) * gp_ref[...]


def _merge(a, hf, hb, m4, sg, x2, p):
    T = x2.shape[0]
    tm = TOKEN_TILE
    row = lambda i: (i, 0)
    return pl.pallas_call(
        _merge_kernel,
        grid=(T // tm,),
        in_specs=[
            pl.BlockSpec((tm, ATTN_WIDTH), row),
            pl.BlockSpec((tm, M_WIDTH), row),
            pl.BlockSpec((tm, M_WIDTH), row),
            pl.BlockSpec((tm, M_WIDTH), lambda i: (i, 3)),
            pl.BlockSpec((tm, 2 * D_MODEL), row),
            pl.BlockSpec((tm, D_MODEL), row),
            _const_spec((1, M_WIDTH)),
            _const_spec((ATTN_WIDTH, D_MODEL)), _const_spec((M_WIDTH, D_MODEL)),
            _const_spec((D_MODEL, D_MODEL)), _const_spec((1, D_MODEL)),
        ],
        out_specs=pl.BlockSpec((tm, D_MODEL), row),
        out_shape=jax.ShapeDtypeStruct((T, D_MODEL), F32),
        compiler_params=pltpu.CompilerParams(
            dimension_semantics=("parallel",), vmem_limit_bytes=VMEM_LIMIT),
        name="merge",
    )(a, hf, hb, m4, sg, x2, p["g_mlstm"], p["wao"], p["wmo"], p["wout"], p["g_mix_post"])


def _ffn_kernel(xp_ref, x_ref, xn_ref, g_ref, wg_ref, wv_ref, cw_ref, cb_ref, wd_ref, gp_ref, o_ref,
                h_ref, ge_ref, *, tiles_per_seq):
    tm = x_ref.shape[0]
    g = g_ref[...]

    def norm(x):
        ms = jnp.mean(x * x, axis=-1, keepdims=True)
        return (x * lax.rsqrt(ms + EPS) * g).astype(BF16)

    h_ref[0:HALO, :] = norm(xp_ref[...])
    h_ref[HALO:HALO + tm, :] = norm(x_ref[...])
    h_ref[HALO + tm:, :] = norm(xn_ref[...])
    pos = pl.program_id(0) % tiles_per_seq
    keep_prev = (pos != 0).astype(F32)
    keep_next = (pos != tiles_per_seq - 1).astype(F32)

    ge_ref[...] = jnp.dot(h_ref[...], wg_ref[...], preferred_element_type=F32)
    ge_ref[HALO - 1:HALO, :] = ge_ref[HALO - 1:HALO, :] * keep_prev
    ge_ref[HALO + tm:HALO + tm + 1, :] = ge_ref[HALO + tm:HALO + tm + 1, :] * keep_next
    val = jnp.dot(h_ref[HALO:HALO + tm, :], wv_ref[...], preferred_element_type=F32)
    cw = cw_ref[...]
    conv = (ge_ref[HALO - 1:HALO - 1 + tm, :] * cw[0:1]
            + ge_ref[HALO:HALO + tm, :] * cw[1:2]
            + ge_ref[HALO + 1:HALO + 1 + tm, :] * cw[2:3]
            + cb_ref[...])
    act = (jax.nn.gelu(conv, approximate=True) * val).astype(BF16)
    y = jnp.dot(act, wd_ref[...], preferred_element_type=F32)
    ms = jnp.mean(y * y, axis=-1, keepdims=True)
    o_ref[...] = x_ref[...] + y * lax.rsqrt(ms + EPS) * gp_ref[...]


def _ffn(x1, p, S):
    T = x1.shape[0]
    tm = FFN_TOKEN_TILE
    hpt = tm // HALO
    n_halo = T // HALO
    kern = functools.partial(_ffn_kernel, tiles_per_seq=S // tm)
    return pl.pallas_call(
        kern,
        grid=(T // tm,),
        in_specs=[
            pl.BlockSpec((HALO, D_MODEL), lambda i: (jnp.maximum(i * hpt - 1, 0), 0)),
            pl.BlockSpec((tm, D_MODEL), lambda i: (i, 0)),
            pl.BlockSpec((HALO, D_MODEL), lambda i: (jnp.minimum((i + 1) * hpt, n_halo - 1), 0)),
            _const_spec((1, D_MODEL)),
            _const_spec((D_MODEL, D_FF)), _const_spec((D_MODEL, D_FF)),
            _const_spec((8, D_FF)), _const_spec((1, D_FF)),
            _const_spec((D_FF, D_MODEL)), _const_spec((1, D_MODEL)),
        ],
        out_specs=pl.BlockSpec((tm, D_MODEL), lambda i: (i, 0)),
        out_shape=jax.ShapeDtypeStruct((T, D_MODEL), F32),
        scratch_shapes=[
            pltpu.VMEM((tm + 2 * HALO, D_MODEL), BF16),
            pltpu.VMEM((tm + 2 * HALO, D_FF), F32),
        ],
        compiler_params=pltpu.CompilerParams(
            dimension_semantics=("parallel",), vmem_limit_bytes=VMEM_LIMIT),
        name="ffn",
    )(x1, x1, x1, p["g_ffn_pre"], p["wup_g"], p["wup_v"], p["conv_w"], p["conv_b"], p["wdown"], p["g_ffn_post"])


def _rope_tables(S, q_gain, k_gain):
    rows = S // GRID_W
    r = jnp.repeat(jnp.arange(rows, dtype=F32), GRID_W)
    c = jnp.tile(jnp.arange(GRID_W, dtype=F32), rows)
    n_pairs_axis = HEAD_DIM // 4
    freq = ROPE_THETA ** (-jnp.arange(n_pairs_axis, dtype=F32) / n_pairs_axis)
    ang = jnp.concatenate([r[:, None] * freq, c[:, None] * freq], axis=-1)
    cos, sin = jnp.cos(ang), jnp.sin(ang)
    cc = jnp.concatenate([cos, cos], axis=-1)
    ss = jnp.concatenate([-sin, sin], axis=-1)

    def tables(gain, scale):
        g_half = jnp.concatenate([gain[0::2], gain[1::2]])
        g_swap = jnp.concatenate([gain[1::2], gain[0::2]])
        ta = cc * g_half * scale
        tb = ss * g_swap * scale
        return jnp.tile(ta, (1, LANES // HEAD_DIM)), jnp.tile(tb, (1, LANES // HEAD_DIM))

    tqa, tqb = tables(q_gain, LOG2E * HEAD_DIM ** -0.5)
    tka, tkb = tables(k_gain, 1.0)
    return tqa, tqb, tka, tkb


def _prepare(norm_mix_pre, w_in, b_if, q_norm, k_norm, mlstm_norm, w_attn_o, w_mlstm_o, w_out,
             norm_mix_post, norm_ffn_pre, w_up, conv_w, conv_b, w_down, norm_ffn_post, seq_lens):
    half = np.concatenate([np.arange(0, HEAD_DIM, 2), np.arange(1, HEAD_DIM, 2)])
    slots = [h for j in range(GQA_GROUP) for h in (j, j + GQA_GROUP)]
    q_cols = np.concatenate([h * HEAD_DIM + half for h in slots])
    k_cols = ATTN_WIDTH + np.concatenate([h * HEAD_DIM + half for h in range(N_KV_HEADS)])
    v_cols = ATTN_WIDTH + KV_WIDTH + np.arange(KV_WIDTH)
    o_rows = np.concatenate([h * HEAD_DIM + np.arange(HEAD_DIM) for h in slots])
    m0 = ATTN_WIDTH + 2 * KV_WIDTH
    g0 = m0 + 4 * M_WIDTH
    s0 = g0 + N_GATE_PRE

    p = {}
    p["wa"] = w_in[:, np.concatenate([q_cols, k_cols, v_cols])].astype(BF16)
    p["wm"] = w_in[:, m0:g0].astype(BF16)
    p["wg"] = jnp.pad(w_in[:, g0:s0], ((0, 0), (0, LANES - N_GATE_PRE))).astype(BF16)
    p["ws"] = w_in[:, s0:].astype(BF16)
    blk = np.arange(MXU_DIM) // HEAD_DIM
    p["bd"] = jnp.asarray((blk[:, None] == blk[None, :]).astype(np.float32) / HEAD_DIM, BF16)
    p["g_mix_pre"] = norm_mix_pre.reshape(1, D_MODEL)
    p["b_if_row"] = b_if.reshape(1, N_GATE_PRE)
    p["b_if_col"] = b_if.reshape(N_GATE_PRE, 1)
    p["g_mlstm"] = mlstm_norm.reshape(1, M_WIDTH)
    p["wao"] = w_attn_o[o_rows].astype(BF16)
    p["wmo"] = w_mlstm_o.astype(BF16)
    p["wout"] = w_out.astype(BF16)
    p["g_mix_post"] = norm_mix_post.reshape(1, D_MODEL)
    p["g_ffn_pre"] = norm_ffn_pre.reshape(1, D_MODEL)
    p["wup_g"] = w_up[:, :D_FF].astype(BF16)
    p["wup_v"] = w_up[:, D_FF:].astype(BF16)
    p["conv_w"] = jnp.pad(conv_w, ((0, 8 - CONV_W), (0, 0)))
    p["conv_b"] = conv_b.reshape(1, D_FF)
    p["wdown"] = w_down.astype(BF16)
    p["g_ffn_post"] = norm_ffn_post.reshape(1, D_MODEL)
    p["score_bound"] = (1.02 * LOG2E * HEAD_DIM ** 0.5) * jnp.max(jnp.abs(q_norm)) * jnp.max(jnp.abs(k_norm))
    p["tqa"], p["tqb"], p["tka"], p["tkb"] = {}, {}, {}, {}
    for S in seq_lens:
        p["tqa"][S], p["tqb"][S], p["tka"][S], p["tkb"][S] = _rope_tables(S, q_norm, k_norm)
    return p


def _trunk(x, p):
    B, S, _ = x.shape
    T = B * S
    x2 = x.reshape(T, D_MODEL)
    q, k, v, m4, gate, sg = _in_proj(x2, p, S)
    n_chunks = S // ATTN_KV_CHUNK
    vt = v.reshape(B, n_chunks, ATTN_KV_CHUNK, N_KV_HEADS, HEAD_DIM).transpose(0, 3, 1, 4, 2)
    gate_t = gate.T
    a, hf, hb = lax.cond(p["score_bound"] <= MAX_UNSTABILISED_SCORE,
                         lambda: _mixers(q, k, vt, m4, gate, gate_t, p, B, S, False),
                         lambda: _mixers(q, k, vt, m4, gate, gate_t, p, B, S, True))
    x1 = _merge(a, hf, hb, m4, sg, x2, p)
    y = _ffn(x1, p, S)
    return y.reshape(B, S, D_MODEL)


def kernel(x_prompt, x_sample, norm_mix_pre, w_in, b_if, q_norm, k_norm, mlstm_norm, w_attn_o, w_mlstm_o,
           w_out, norm_mix_post, norm_ffn_pre, w_up, conv_w, conv_b, w_down, norm_ffn_post):
    depth = w_in.shape[0]
    seq_lens = sorted({x_prompt.shape[1], x_sample.shape[1]})
    layers = [
        _prepare(norm_mix_pre[l], w_in[l], b_if[l], q_norm[l], k_norm[l], mlstm_norm[l], w_attn_o[l],
                 w_mlstm_o[l], w_out[l], norm_mix_post[l], norm_ffn_pre[l], w_up[l], conv_w[l], conv_b[l],
                 w_down[l], norm_ffn_post[l], seq_lens)
        for l in range(depth)
    ]

    def trunk(x):
        for p in layers:
            x = _trunk(x, p)
        return x

    return (trunk(x_prompt), trunk(x_sample))
```

```python
import functools

import numpy as np
import jax
import jax.numpy as jnp
from jax import lax
from jax.experimental import pallas as pl
from jax.experimental.pallas import tpu as pltpu

D_MODEL = 1024
GRID_W = 64
N_Q_HEADS = 8
N_KV_HEADS = 2
GQA_GROUP = N_Q_HEADS // N_KV_HEADS
HEAD_DIM = 64
ATTN_WIDTH = N_Q_HEADS * HEAD_DIM
KV_WIDTH = N_KV_HEADS * HEAD_DIM
ROPE_THETA = 10000.0
M_HEADS = 4
M_HEAD_DIM = 128
M_WIDTH = M_HEADS * M_HEAD_DIM
N_GATE_PRE = 4 * M_HEADS
D_FF = 2816
CONV_W = 3
EPS = 1e-6

LANES = 128
MXU_DIM = 256
VMEM_LIMIT = 56 * 1024 * 1024

TOKEN_TILE = 512
ATTN_Q_TILE = 512
ATTN_KV_CHUNK = 256
M_CHUNK = 128
FFN_TOKEN_TILE = 256
HALO = 16
LOG2E = 1.4426950408889634
MAX_UNSTABILISED_SCORE = 40.0

BF16 = jnp.bfloat16
F32 = jnp.float32


def _sigmoid(x):
    return 0.5 * jnp.tanh(0.5 * x) + 0.5


def _const_spec(shape):
    n = len(shape)
    return pl.BlockSpec(shape, lambda *_: (0,) * n, pipeline_mode=pl.Buffered(1))


def _in_proj_kernel(x_ref, g_ref, wa_ref, wm_ref, wg_ref, ws_ref, bd_ref,
                    tqa_ref, tqb_ref, tka_ref, tkb_ref,
                    q_ref, k_ref, v_ref, m_ref, gate_ref, sg_ref):
    x = x_ref[...]
    ms = jnp.mean(x * x, axis=-1, keepdims=True)
    h = (x * lax.rsqrt(ms + EPS) * g_ref[...]).astype(BF16)

    za = jnp.dot(h, wa_ref[...], preferred_element_type=F32)
    bd = bd_ref[...]
    lane = lax.broadcasted_iota(jnp.int32, (x.shape[0], LANES), 1)
    first_half = (lane % HEAD_DIM) < (HEAD_DIM // 2)

    def head_norm_rope(z, msq, ta, tb):
        sw = jnp.where(first_half, pltpu.roll(z, LANES - HEAD_DIM // 2, 1), pltpu.roll(z, HEAD_DIM // 2, 1))
        return lax.rsqrt(msq + EPS) * (z * ta + sw * tb)

    tqa, tqb = tqa_ref[...], tqb_ref[...]
    for half in range(ATTN_WIDTH // MXU_DIM):
        zq = za[:, half * MXU_DIM:(half + 1) * MXU_DIM]
        msq = jnp.dot((zq * zq).astype(BF16), bd, preferred_element_type=F32)
        for j in range(MXU_DIM // LANES):
            c0 = half * MXU_DIM + j * LANES
            out = head_norm_rope(zq[:, j * LANES:(j + 1) * LANES], msq[:, j * LANES:(j + 1) * LANES], tqa, tqb)
            q_ref[:, c0:c0 + LANES] = out.astype(BF16)
    zk = za[:, ATTN_WIDTH:ATTN_WIDTH + KV_WIDTH]
    msk = jnp.dot((zk * zk).astype(BF16), bd[:LANES, :LANES], preferred_element_type=F32)
    k_ref[...] = head_norm_rope(zk, msk, tka_ref[...], tkb_ref[...]).astype(BF16)
    v_ref[...] = za[:, ATTN_WIDTH + KV_WIDTH:].astype(BF16)

    zm = jnp.dot(h, wm_ref[...], preferred_element_type=F32)
    m_ref[:, 0:M_WIDTH] = zm[:, 0:M_WIDTH].astype(BF16)
    m_ref[:, M_WIDTH:2 * M_WIDTH] = (zm[:, M_WIDTH:2 * M_WIDTH] * (M_HEAD_DIM ** -0.5)).astype(BF16)
    m_ref[:, 2 * M_WIDTH:3 * M_WIDTH] = zm[:, 2 * M_WIDTH:3 * M_WIDTH].astype(BF16)
    m_ref[:, 3 * M_WIDTH:] = _sigmoid(zm[:, 3 * M_WIDTH:]).astype(BF16)

    zg = jnp.dot(h, wg_ref[...], preferred_element_type=F32)
    gate_ref[...] = zg[:, :N_GATE_PRE]

    zs = jnp.dot(h, ws_ref[...], preferred_element_type=F32)
    sg_ref[...] = _sigmoid(zs).astype(BF16)


def _in_proj(x2, p, S):
    T = x2.shape[0]
    tm = TOKEN_TILE
    spt = S // tm
    row = lambda i: (i, 0)
    pos = lambda i: (i % spt, 0)
    tab = pl.BlockSpec((tm, LANES), pos)
    return pl.pallas_call(
        _in_proj_kernel,
        grid=(T // tm,),
        in_specs=[
            pl.BlockSpec((tm, D_MODEL), row),
            _const_spec((1, D_MODEL)),
            _const_spec(p["wa"].shape), _const_spec(p["wm"].shape),
            _const_spec(p["wg"].shape), _const_spec(p["ws"].shape),
            _const_spec((MXU_DIM, MXU_DIM)),
            tab, tab, tab, tab,
        ],
        out_specs=[
            pl.BlockSpec((tm, ATTN_WIDTH), row),
            pl.BlockSpec((tm, KV_WIDTH), row),
            pl.BlockSpec((tm, KV_WIDTH), row),
            pl.BlockSpec((tm, 4 * M_WIDTH), row),
            pl.BlockSpec((tm, N_GATE_PRE), row),
            pl.BlockSpec((tm, 2 * D_MODEL), row),
        ],
        out_shape=[
            jax.ShapeDtypeStruct((T, ATTN_WIDTH), BF16),
            jax.ShapeDtypeStruct((T, KV_WIDTH), BF16),
            jax.ShapeDtypeStruct((T, KV_WIDTH), BF16),
            jax.ShapeDtypeStruct((T, 4 * M_WIDTH), BF16),
            jax.ShapeDtypeStruct((T, N_GATE_PRE), F32),
            jax.ShapeDtypeStruct((T, 2 * D_MODEL), BF16),
        ],
        compiler_params=pltpu.CompilerParams(
            dimension_semantics=("parallel",), vmem_limit_bytes=VMEM_LIMIT),
        name="in_proj",
    )(x2, p["g_mix_pre"], p["wa"], p["wm"], p["wg"], p["ws"], p["bd"],
      p["tqa"][S], p["tqb"][S], p["tka"][S], p["tkb"][S])


def _mlstm_stages(qf_ref, kf_ref, vf_ref, qb_ref, kb_ref, vb_ref, gcf_ref, gcb_ref, grf_ref, grb_ref, bc_ref, br_ref,
                  hf_ref, hb_ref, c_ref, m_ref, n_steps):
    L = M_CHUNK
    ri = lax.broadcasted_iota(jnp.int32, (L, L), 0)
    ci = lax.broadcasted_iota(jnp.int32, (L, L), 1)
    lower = ci <= ri
    upper = ci >= ri
    lower_b = lower.astype(BF16)
    upper_b = upper.astype(BF16)
    ones = jnp.ones((L, M_HEAD_DIM), BF16)
    fwd_cols = lax.broadcasted_iota(jnp.int32, (L, N_GATE_PRE), 1) < N_GATE_PRE // 2
    fwd_rows = lax.broadcasted_iota(jnp.int32, (N_GATE_PRE, L), 0) < N_GATE_PRE // 2

    def split_dot(tri, x, tri_first):
        hi = x.astype(BF16)
        lo = (x - hi.astype(F32)).astype(BF16)
        if tri_first:
            return jnp.dot(tri, hi, preferred_element_type=F32) + jnp.dot(tri, lo, preferred_element_type=F32)
        return jnp.dot(hi, tri, preferred_element_type=F32) + jnp.dot(lo, tri, preferred_element_type=F32)

    dirs = ((qf_ref, kf_ref, vf_ref, gcf_ref, grf_ref, hf_ref, lower),
            (qb_ref, kb_ref, vb_ref, gcb_ref, grb_ref, hb_ref, upper))

    for t in range(n_steps):
        rows = (slice(t * L, (t + 1) * L), slice((n_steps - 1 - t) * L, (n_steps - t) * L))
        chains = []
        for d, (q_ref, k_ref, v_ref, _, _, h_ref, mask) in enumerate(dirs):
            for hd in range(M_HEADS):
                cols = slice(hd * M_HEAD_DIM, (hd + 1) * M_HEAD_DIM)
                st = d * M_HEADS + hd
                ch = dict(st=st, d=d, hd=hd, mask=mask, h_ref=h_ref, cols=cols,
                          q=q_ref[rows[d], cols], k=k_ref[rows[d], cols],
                          v_ext=jnp.concatenate([v_ref[rows[d], cols], ones], axis=1),
                          m_old=m_ref[st:st + 1, 0:1],
                          c_old=c_ref[st])
                ch["qk"] = lax.dot_general(ch["q"], ch["k"], (((1,), (1,)), ((), ())), preferred_element_type=F32)
                ch["qc"] = jnp.dot(ch["q"], ch["c_old"].astype(BF16), preferred_element_type=F32)
                chains.append(ch)
        yield

        gcol = jnp.where(fwd_cols, gcf_ref[rows[0], :], gcb_ref[rows[1], :]) + bc_ref[...]
        grow = jnp.where(fwd_rows, grf_ref[:, rows[0]], grb_ref[:, rows[1]]) + br_ref[...]
        lf_col = jax.nn.log_sigmoid(gcol)
        lf_row = jax.nn.log_sigmoid(grow)
        pre_col = split_dot(lower_b, lf_col, True)
        pre_row = split_dot(upper_b, lf_row, False)
        tot_col = pre_col[L - 1:L, :]
        tot_row = pre_row[:, L - 1:L]
        for ch in chains:
            gi = 2 * ch["d"] * M_HEADS + ch["hd"]
            gf = gi + M_HEADS
            b_row = pre_row[gf:gf + 1, :]
            b_col = pre_col[:, gf:gf + 1]
            if ch["d"] == 1:
                b_row = tot_row[gf:gf + 1, :] - b_row + lf_row[gf:gf + 1, :]
                b_col = tot_col[:, gf:gf + 1] - b_col + lf_col[:, gf:gf + 1]
            ch["b_col"] = b_col
            ch["r_row"] = grow[gi:gi + 1, :] - b_row
            ch["b_tot"] = tot_row[gf:gf + 1, :]
        yield

        for ch in chains:
            ws_row = ch["b_tot"] + ch["r_row"]
            m_new = jnp.maximum(ch["b_tot"] + ch["m_old"], jnp.max(ws_row, axis=1, keepdims=True))
            a = jnp.exp(ch["b_tot"] + ch["m_old"] - m_new)
            w_row = jnp.exp(ws_row - m_new)
            kw = (ch["k"].astype(F32).T * w_row).astype(BF16)
            st = ch["st"]
            c_ref[st] = a * ch["c_old"] + jnp.dot(kw, ch["v_ext"], preferred_element_type=F32)
            m_ref[st:st + 1, :] = jnp.broadcast_to(m_new, (1, LANES))
        yield

        for ch in chains:
            r = jnp.where(ch["mask"], ch["r_row"], -jnp.inf)
            m_col = jnp.maximum(jnp.max(r, axis=1, keepdims=True), ch["m_old"])
            ch["w"] = (jnp.exp(r - m_col) * ch["qk"]).astype(BF16)
            ch["dec"] = jnp.exp(ch["m_old"] - m_col)
            ch["floor"] = jnp.exp(-(ch["b_col"] + m_col))
        yield

        for ch in chains:
            ext = jnp.dot(ch["w"], ch["v_ext"], preferred_element_type=F32) + ch["dec"] * ch["qc"]
            num, den = ext[:, :M_HEAD_DIM], ext[:, M_HEAD_DIM:]
            h = num / jnp.maximum(jnp.abs(den), ch["floor"])
            ch["h_ref"][rows[ch["d"]], ch["cols"]] = h.astype(ch["h_ref"].dtype)
        yield


MLSTM_STAGES_PER_CHUNK = 5


def _mixer_kernel(stabilise_ref, q_ref, k_ref, vt_ref,
                  qf_ref, kf_ref, vf_ref, qb_ref, kb_ref, vb_ref, gcf_ref, gcb_ref, grf_ref, grb_ref, bc_ref, br_ref,
                  o_ref, hf_ref, hb_ref,
                  w_ref, p0_ref, p1_ref, acc_ref, l_ref, c_ref, m_ref, *, n_chunks, tq, kc):
    row = lax.broadcasted_iota(jnp.int32, (LANES, tq), 0)
    low = row < HEAD_DIM
    for j in range(GQA_GROUP):
        qt = q_ref[:, j * LANES:(j + 1) * LANES].astype(F32).T
        w_ref[:, (2 * j) * tq:(2 * j + 1) * tq] = jnp.where(low, qt, 0.0).astype(BF16)
        w_ref[:, (2 * j + 1) * tq:(2 * j + 2) * tq] = jnp.where(low, 0.0, qt).astype(BF16)
    acc_ref[...] = jnp.zeros(acc_ref.shape, F32)
    l_ref[...] = jnp.zeros(l_ref.shape, F32)

    @pl.when(pl.program_id(1) == 0)
    def _():
        c_ref[...] = jnp.zeros(c_ref.shape, F32)
        m_ref[...] = jnp.zeros(m_ref.shape, F32)

    def step(c, ms, src=None, dst=None, *, stabilise):
        new_ms = []
        if dst is not None:
            start = (c + 1) * kc
            if not isinstance(start, int):
                start = pl.multiple_of(start, kc)
            kch = k_ref[pl.ds(start, kc), :]
        for slot in range(N_Q_HEADS):
            cols = slice(slot * tq, (slot + 1) * tq)
            if dst is not None:
                s = jnp.dot(kch, w_ref[:, cols], preferred_element_type=F32)
                if stabilise:
                    m = ms[slot][0]
                    mn = jnp.maximum(m, jnp.max(s, axis=0, keepdims=True))
                    new_ms.append((mn, jnp.exp2(m - mn)))
                    s = s - mn
                    l_ref[slot] = l_ref[slot] * new_ms[-1][1]
                pr = jnp.exp2(s)
                dst[:, cols] = pr.astype(BF16)
                l_ref[slot] += jnp.sum(pr.reshape(kc // 8, 8, tq), axis=0)
            if src is not None:
                pv = jnp.dot(vt_ref[slot % N_KV_HEADS, c], src[:, cols], preferred_element_type=F32)
                if stabilise:
                    acc_ref[slot] = acc_ref[slot] * ms[slot][1] + pv
                else:
                    acc_ref[slot] += pv
        return ms if dst is None else tuple(new_ms)

    mlstm_steps = tq // M_CHUNK
    mlstm_args = (qf_ref, kf_ref, vf_ref, qb_ref, kb_ref, vb_ref, gcf_ref, gcb_ref, grf_ref, grb_ref,
                  bc_ref, br_ref, hf_ref, hb_ref, c_ref, m_ref, mlstm_steps)

    @pl.when(stabilise_ref[0] != 0)
    def _():
        st = functools.partial(step, stabilise=True)
        neg_inf = jnp.full((1, tq), -jnp.inf, F32)
        state = st(-1, tuple((neg_inf, neg_inf) for _ in range(N_Q_HEADS)), dst=p0_ref)
        state = lax.fori_loop(
            0, n_chunks // 2 - 1,
            lambda j, ms: st(2 * j + 1, st(2 * j, ms, src=p0_ref, dst=p1_ref), src=p1_ref, dst=p0_ref), state)
        state = st(n_chunks - 2, state, src=p0_ref, dst=p1_ref)
        st(n_chunks - 1, state, src=p1_ref)
        for _ in _mlstm_stages(*mlstm_args):
            pass

    @pl.when(stabilise_ref[0] == 0)
    def _():
        mlstm = _mlstm_stages(*mlstm_args)
        bufs = (p0_ref, p1_ref)
        steps = ([(-1, None, p0_ref)]
                 + [(c, bufs[c % 2], bufs[(c + 1) % 2]) for c in range(n_chunks - 1)]
                 + [(n_chunks - 1, bufs[(n_chunks - 1) % 2], None)])
        n_stages = mlstm_steps * MLSTM_STAGES_PER_CHUNK
        done = 0
        for idx, (c, src, dst) in enumerate(steps):
            step(c, (), src=src, dst=dst, stabilise=False)
            while done < ((idx + 1) * n_stages) // len(steps):
                next(mlstm)
                done += 1

    outs = [acc_ref[slot] / jnp.sum(l_ref[slot], axis=0, keepdims=True) for slot in range(N_Q_HEADS)]
    o_ref[...] = jnp.concatenate(outs, axis=0).T.astype(BF16)


def _mixers(stabilise, q, k, vt, m4, gate, gate_t, p, B, S):
    T = q.shape[0]
    tq, kc = ATTN_Q_TILE, ATTN_KV_CHUNK
    nq, n_chunks = S // tq, S // kc
    kern = functools.partial(_mixer_kernel, n_chunks=n_chunks, tq=tq, kc=kc)
    fwd = lambda col: (lambda b, i, _: (b * nq + i, col))
    bwd = lambda col: (lambda b, i, _: (b * nq + nq - 1 - i, col))
    blk = lambda f: pl.BlockSpec((tq, M_WIDTH), f)
    grid_spec = pltpu.PrefetchScalarGridSpec(
        num_scalar_prefetch=1,
        grid=(B, nq),
        in_specs=[
            pl.BlockSpec((tq, ATTN_WIDTH), fwd(0)),
            pl.BlockSpec((S, KV_WIDTH), lambda b, i, _: (b, 0)),
            pl.BlockSpec((None, N_KV_HEADS, n_chunks, HEAD_DIM, kc), lambda b, i, _: (b, 0, 0, 0, 0)),
            blk(fwd(0)), blk(fwd(1)), blk(fwd(2)),
            blk(bwd(0)), blk(bwd(1)), blk(bwd(2)),
            pl.BlockSpec((tq, N_GATE_PRE), fwd(0)),
            pl.BlockSpec((tq, N_GATE_PRE), bwd(0)),
            pl.BlockSpec((N_GATE_PRE, tq), lambda b, i, _: (0, b * nq + i)),
            pl.BlockSpec((N_GATE_PRE, tq), lambda b, i, _: (0, b * nq + nq - 1 - i)),
            _const_spec((1, N_GATE_PRE)), _const_spec((N_GATE_PRE, 1)),
        ],
        out_specs=[pl.BlockSpec((tq, ATTN_WIDTH), fwd(0)), blk(fwd(0)), blk(bwd(0))],
        scratch_shapes=[
            pltpu.VMEM((LANES, N_Q_HEADS * tq), BF16),
            pltpu.VMEM((kc, N_Q_HEADS * tq), BF16),
            pltpu.VMEM((kc, N_Q_HEADS * tq), BF16),
            pltpu.VMEM((N_Q_HEADS, HEAD_DIM, tq), F32),
            pltpu.VMEM((N_Q_HEADS, 8, tq), F32),
            pltpu.VMEM((2 * M_HEADS, M_HEAD_DIM, 2 * M_HEAD_DIM), F32),
            pltpu.VMEM((2 * M_HEADS, LANES), F32),
        ],
    )
    return pl.pallas_call(
        kern,
        grid_spec=grid_spec,
        out_shape=[jax.ShapeDtypeStruct((T, ATTN_WIDTH), BF16)] + [jax.ShapeDtypeStruct((T, M_WIDTH), BF16)] * 2,
        compiler_params=pltpu.CompilerParams(
            dimension_semantics=("parallel", "arbitrary"), vmem_limit_bytes=VMEM_LIMIT),
        name="mixers",
    )(stabilise, q, k, vt, m4, m4, m4, m4, m4, m4, gate, gate, gate_t, gate_t, p["b_if_row"], p["b_if_col"])


def _merge_kernel(a_ref, hf_ref, hb_ref, so_ref, sg_ref, x_ref, gm_ref, wao_ref, wmo_ref, wout_ref, gp_ref, o_ref):
    hs = hf_ref[...].astype(F32) + hb_ref[...].astype(F32)
    parts = []
    for hd in range(M_HEADS):
        blk = hs[:, hd * M_HEAD_DIM:(hd + 1) * M_HEAD_DIM]
        ms = jnp.mean(blk * blk, axis=-1, keepdims=True)
        parts.append(blk * lax.rsqrt(ms + EPS))
    hn = jnp.concatenate(parts, axis=1) * gm_ref[...]
    hm = (hn * so_ref[...].astype(F32)).astype(BF16)
    a_out = jnp.dot(a_ref[...], wao_ref[...], preferred_element_type=F32)
    m_out = jnp.dot(hm, wmo_ref[...], preferred_element_type=F32)
    sg = sg_ref[...]
    merged = sg[:, :D_MODEL].astype(F32) * a_out + sg[:, D_MODEL:].astype(F32) * m_out
    y = jnp.dot(merged.astype(BF16), wout_ref[...], preferred_element_type=F32)
    ms = jnp.mean(y * y, axis=-1, keepdims=True)
    o_ref[...] = x_ref[...] + y * lax.rsqrt(ms + EPS) * gp_ref[...]


def _merge(a, hf, hb, m4, sg, x2, p):
    T = x2.shape[0]
    tm = TOKEN_TILE
    row = lambda i: (i, 0)
    return pl.pallas_call(
        _merge_kernel,
        grid=(T // tm,),
        in_specs=[
            pl.BlockSpec((tm, ATTN_WIDTH), row),
            pl.BlockSpec((tm, M_WIDTH), row),
            pl.BlockSpec((tm, M_WIDTH), row),
            pl.BlockSpec((tm, M_WIDTH), lambda i: (i, 3)),
            pl.BlockSpec((tm, 2 * D_MODEL), row),
            pl.BlockSpec((tm, D_MODEL), row),
            _const_spec((1, M_WIDTH)),
            _const_spec((ATTN_WIDTH, D_MODEL)), _const_spec((M_WIDTH, D_MODEL)),
            _const_spec((D_MODEL, D_MODEL)), _const_spec((1, D_MODEL)),
        ],
        out_specs=pl.BlockSpec((tm, D_MODEL), row),
        out_shape=jax.ShapeDtypeStruct((T, D_MODEL), F32),
        compiler_params=pltpu.CompilerParams(
            dimension_semantics=("parallel",), vmem_limit_bytes=VMEM_LIMIT),
        name="merge",
    )(a, hf, hb, m4, sg, x2, p["g_mlstm"], p["wao"], p["wmo"], p["wout"], p["g_mix_post"])


def _ffn_kernel(xp_ref, x_ref, xn_ref, g_ref, wg_ref, wv_ref, cw_ref, cb_ref, wd_ref, gp_ref, o_ref,
                h_ref, ge_ref, *, tiles_per_seq):
    tm = x_ref.shape[0]
    g = g_ref[...]

    def norm(x):
        ms = jnp.mean(x * x, axis=-1, keepdims=True)
        return (x * lax.rsqrt(ms + EPS) * g).astype(BF16)

    h_ref[0:HALO, :] = norm(xp_ref[...])
    h_ref[HALO:HALO + tm, :] = norm(x_ref[...])
    h_ref[HALO + tm:, :] = norm(xn_ref[...])
    pos = pl.program_id(0) % tiles_per_seq
    keep_prev = (pos != 0).astype(F32)
    keep_next = (pos != tiles_per_seq - 1).astype(F32)

    ge_ref[...] = jnp.dot(h_ref[...], wg_ref[...], preferred_element_type=F32)
    ge_ref[HALO - 1:HALO, :] = ge_ref[HALO - 1:HALO, :] * keep_prev
    ge_ref[HALO + tm:HALO + tm + 1, :] = ge_ref[HALO + tm:HALO + tm + 1, :] * keep_next
    val = jnp.dot(h_ref[HALO:HALO + tm, :], wv_ref[...], preferred_element_type=F32)
    cw = cw_ref[...]
    conv = (ge_ref[HALO - 1:HALO - 1 + tm, :] * cw[0:1]
            + ge_ref[HALO:HALO + tm, :] * cw[1:2]
            + ge_ref[HALO + 1:HALO + 1 + tm, :] * cw[2:3]
            + cb_ref[...])
    act = (jax.nn.gelu(conv, approximate=True) * val).astype(BF16)
    y = jnp.dot(act, wd_ref[...], preferred_element_type=F32)
    ms = jnp.mean(y * y, axis=-1, keepdims=True)
    o_ref[...] = x_ref[...] + y * lax.rsqrt(ms + EPS) * gp_ref[...]


def _ffn(x1, p, S):
    T = x1.shape[0]
    tm = FFN_TOKEN_TILE
    hpt = tm // HALO
    n_halo = T // HALO
    kern = functools.partial(_ffn_kernel, tiles_per_seq=S // tm)
    return pl.pallas_call(
        kern,
        grid=(T // tm,),
        in_specs=[
            pl.BlockSpec((HALO, D_MODEL), lambda i: (jnp.maximum(i * hpt - 1, 0), 0)),
            pl.BlockSpec((tm, D_MODEL), lambda i: (i, 0)),
            pl.BlockSpec((HALO, D_MODEL), lambda i: (jnp.minimum((i + 1) * hpt, n_halo - 1), 0)),
            _const_spec((1, D_MODEL)),
            _const_spec((D_MODEL, D_FF)), _const_spec((D_MODEL, D_FF)),
            _const_spec((8, D_FF)), _const_spec((1, D_FF)),
            _const_spec((D_FF, D_MODEL)), _const_spec((1, D_MODEL)),
        ],
        out_specs=pl.BlockSpec((tm, D_MODEL), lambda i: (i, 0)),
        out_shape=jax.ShapeDtypeStruct((T, D_MODEL), F32),
        scratch_shapes=[
            pltpu.VMEM((tm + 2 * HALO, D_MODEL), BF16),
            pltpu.VMEM((tm + 2 * HALO, D_FF), F32),
        ],
        compiler_params=pltpu.CompilerParams(
            dimension_semantics=("parallel",), vmem_limit_bytes=VMEM_LIMIT),
        name="ffn",
    )(x1, x1, x1, p["g_ffn_pre"], p["wup_g"], p["wup_v"], p["conv_w"], p["conv_b"], p["wdown"], p["g_ffn_post"])


def _rope_tables(S, q_gain, k_gain):
    rows = S // GRID_W
    r = jnp.repeat(jnp.arange(rows, dtype=F32), GRID_W)
    c = jnp.tile(jnp.arange(GRID_W, dtype=F32), rows)
    n_pairs_axis = HEAD_DIM // 4
    freq = ROPE_THETA ** (-jnp.arange(n_pairs_axis, dtype=F32) / n_pairs_axis)
    ang = jnp.concatenate([r[:, None] * freq, c[:, None] * freq], axis=-1)
    cos, sin = jnp.cos(ang), jnp.sin(ang)
    cc = jnp.concatenate([cos, cos], axis=-1)
    ss = jnp.concatenate([-sin, sin], axis=-1)

    def tables(gain, scale):
        g_half = jnp.concatenate([gain[0::2], gain[1::2]])
        g_swap = jnp.concatenate([gain[1::2], gain[0::2]])
        ta = cc * g_half * scale
        tb = ss * g_swap * scale
        return jnp.tile(ta, (1, LANES // HEAD_DIM)), jnp.tile(tb, (1, LANES // HEAD_DIM))

    tqa, tqb = tables(q_gain, LOG2E * HEAD_DIM ** -0.5)
    tka, tkb = tables(k_gain, 1.0)
    return tqa, tqb, tka, tkb


def _prepare(norm_mix_pre, w_in, b_if, q_norm, k_norm, mlstm_norm, w_attn_o, w_mlstm_o, w_out,
             norm_mix_post, norm_ffn_pre, w_up, conv_w, conv_b, w_down, norm_ffn_post, seq_lens):
    half = np.concatenate([np.arange(0, HEAD_DIM, 2), np.arange(1, HEAD_DIM, 2)])
    slots = [h for j in range(GQA_GROUP) for h in (j, j + GQA_GROUP)]
    q_cols = np.concatenate([h * HEAD_DIM + half for h in slots])
    k_cols = ATTN_WIDTH + np.concatenate([h * HEAD_DIM + half for h in range(N_KV_HEADS)])
    v_cols = ATTN_WIDTH + KV_WIDTH + np.arange(KV_WIDTH)
    o_rows = np.concatenate([h * HEAD_DIM + np.arange(HEAD_DIM) for h in slots])
    m0 = ATTN_WIDTH + 2 * KV_WIDTH
    g0 = m0 + 4 * M_WIDTH
    s0 = g0 + N_GATE_PRE

    p = {}
    p["wa"] = w_in[:, np.concatenate([q_cols, k_cols, v_cols])].astype(BF16)
    p["wm"] = w_in[:, m0:g0].astype(BF16)
    p["wg"] = jnp.pad(w_in[:, g0:s0], ((0, 0), (0, LANES - N_GATE_PRE))).astype(BF16)
    p["ws"] = w_in[:, s0:].astype(BF16)
    blk = np.arange(MXU_DIM) // HEAD_DIM
    p["bd"] = jnp.asarray((blk[:, None] == blk[None, :]).astype(np.float32) / HEAD_DIM, BF16)
    p["g_mix_pre"] = norm_mix_pre.reshape(1, D_MODEL)
    p["b_if_row"] = b_if.reshape(1, N_GATE_PRE)
    p["b_if_col"] = b_if.reshape(N_GATE_PRE, 1)
    p["g_mlstm"] = mlstm_norm.reshape(1, M_WIDTH)
    p["wao"] = w_attn_o[o_rows].astype(BF16)
    p["wmo"] = w_mlstm_o.astype(BF16)
    p["wout"] = w_out.astype(BF16)
    p["g_mix_post"] = norm_mix_post.reshape(1, D_MODEL)
    p["g_ffn_pre"] = norm_ffn_pre.reshape(1, D_MODEL)
    p["wup_g"] = w_up[:, :D_FF].astype(BF16)
    p["wup_v"] = w_up[:, D_FF:].astype(BF16)
    p["conv_w"] = jnp.pad(conv_w, ((0, 8 - CONV_W), (0, 0)))
    p["conv_b"] = conv_b.reshape(1, D_FF)
    p["wdown"] = w_down.astype(BF16)
    p["g_ffn_post"] = norm_ffn_post.reshape(1, D_MODEL)
    p["score_bound"] = (1.02 * LOG2E * HEAD_DIM ** 0.5) * jnp.max(jnp.abs(q_norm)) * jnp.max(jnp.abs(k_norm))
    p["tqa"], p["tqb"], p["tka"], p["tkb"] = {}, {}, {}, {}
    for S in seq_lens:
        p["tqa"][S], p["tqb"][S], p["tka"][S], p["tkb"][S] = _rope_tables(S, q_norm, k_norm)
    return p


def _trunk(x, p):
    B, S, _ = x.shape
    T = B * S
    x2 = x.reshape(T, D_MODEL)
    q, k, v, m4, gate, sg = _in_proj(x2, p, S)
    n_chunks = S // ATTN_KV_CHUNK
    vt = v.reshape(B, n_chunks, ATTN_KV_CHUNK, N_KV_HEADS, HEAD_DIM).transpose(0, 3, 1, 4, 2)
    stabilise = (p["score_bound"] > MAX_UNSTABILISED_SCORE).astype(jnp.int32).reshape(1)
    a, hf, hb = _mixers(stabilise, q, k, vt, m4, gate, gate.T, p, B, S)
    x1 = _merge(a, hf, hb, m4, sg, x2, p)
    y = _ffn(x1, p, S)
    return y.reshape(B, S, D_MODEL)


def kernel(x_prompt, x_sample, norm_mix_pre, w_in, b_if, q_norm, k_norm, mlstm_norm, w_attn_o, w_mlstm_o,
           w_out, norm_mix_post, norm_ffn_pre, w_up, conv_w, conv_b, w_down, norm_ffn_post):
    depth = w_in.shape[0]
    seq_lens = sorted({x_prompt.shape[1], x_sample.shape[1]})
    layers = [
        _prepare(norm_mix_pre[l], w_in[l], b_if[l], q_norm[l], k_norm[l], mlstm_norm[l], w_attn_o[l],
                 w_mlstm_o[l], w_out[l], norm_mix_post[l], norm_ffn_pre[l], w_up[l], conv_w[l], conv_b[l],
                 w_down[l], norm_ffn_post[l], seq_lens)
        for l in range(depth)
    ]

    def trunk(x):
        for p in layers:
            x = _trunk(x, p)
        return x

    return (trunk(x_prompt), trunk(x_sample))
```

```python
import functools

import numpy as np
import jax
import jax.numpy as jnp
from jax import lax
from jax.experimental import pallas as pl
from jax.experimental.pallas import tpu as pltpu

D_MODEL = 1024
GRID_W = 64
N_Q_HEADS = 8
N_KV_HEADS = 2
GQA_GROUP = N_Q_HEADS // N_KV_HEADS
HEAD_DIM = 64
ATTN_WIDTH = N_Q_HEADS * HEAD_DIM
KV_WIDTH = N_KV_HEADS * HEAD_DIM
ROPE_THETA = 10000.0
M_HEADS = 4
M_HEAD_DIM = 128
M_WIDTH = M_HEADS * M_HEAD_DIM
N_GATE_PRE = 4 * M_HEADS
D_FF = 2816
CONV_W = 3
EPS = 1e-6

LANES = 128
MXU_DIM = 256
VMEM_LIMIT = 56 * 1024 * 1024

TOKEN_TILE = 512
ATTN_Q_TILE = 512
ATTN_KV_CHUNK = 256
M_CHUNK = 128
FFN_TOKEN_TILE = 256
HALO = 16
LOG2E = 1.4426950408889634
MAX_UNSTABILISED_SCORE = 40.0

BF16 = jnp.bfloat16
F32 = jnp.float32


def _sigmoid(x):
    return 0.5 * jnp.tanh(0.5 * x) + 0.5


def _const_spec(shape):
    n = len(shape)
    return pl.BlockSpec(shape, lambda *_: (0,) * n, pipeline_mode=pl.Buffered(1))


def _in_proj_kernel(x_ref, g_ref, wa_ref, wm_ref, wg_ref, ws_ref, bd_ref,
                    tqa_ref, tqb_ref, tka_ref, tkb_ref,
                    q_ref, k_ref, vt_ref, m_ref, gate_ref, gate_t_ref, sg_ref):
    x = x_ref[...]
    ms = jnp.mean(x * x, axis=-1, keepdims=True)
    h = (x * lax.rsqrt(ms + EPS) * g_ref[...]).astype(BF16)

    za = jnp.dot(h, wa_ref[...], preferred_element_type=F32)
    bd = bd_ref[...]
    lane = lax.broadcasted_iota(jnp.int32, (x.shape[0], LANES), 1)
    first_half = (lane % HEAD_DIM) < (HEAD_DIM // 2)

    def head_norm_rope(z, msq, ta, tb):
        sw = jnp.where(first_half, pltpu.roll(z, LANES - HEAD_DIM // 2, 1), pltpu.roll(z, HEAD_DIM // 2, 1))
        return lax.rsqrt(msq + EPS) * (z * ta + sw * tb)

    tqa, tqb = tqa_ref[...], tqb_ref[...]
    for half in range(ATTN_WIDTH // MXU_DIM):
        zq = za[:, half * MXU_DIM:(half + 1) * MXU_DIM]
        msq = jnp.dot((zq * zq).astype(BF16), bd, preferred_element_type=F32)
        for j in range(MXU_DIM // LANES):
            c0 = half * MXU_DIM + j * LANES
            out = head_norm_rope(zq[:, j * LANES:(j + 1) * LANES], msq[:, j * LANES:(j + 1) * LANES], tqa, tqb)
            q_ref[:, c0:c0 + LANES] = out.astype(BF16)
    zk = za[:, ATTN_WIDTH:ATTN_WIDTH + KV_WIDTH]
    msk = jnp.dot((zk * zk).astype(BF16), bd[:LANES, :LANES], preferred_element_type=F32)
    k_ref[...] = head_norm_rope(zk, msk, tka_ref[...], tkb_ref[...]).astype(BF16)
    v_t = za[:, ATTN_WIDTH + KV_WIDTH:].T
    for kvh in range(N_KV_HEADS):
        for c in range(x.shape[0] // ATTN_KV_CHUNK):
            vt_ref[kvh, c] = v_t[kvh * HEAD_DIM:(kvh + 1) * HEAD_DIM,
                                 c * ATTN_KV_CHUNK:(c + 1) * ATTN_KV_CHUNK].astype(BF16)

    zm = jnp.dot(h, wm_ref[...], preferred_element_type=F32)
    m_ref[:, 0:M_WIDTH] = zm[:, 0:M_WIDTH].astype(BF16)
    m_ref[:, M_WIDTH:2 * M_WIDTH] = (zm[:, M_WIDTH:2 * M_WIDTH] * (M_HEAD_DIM ** -0.5)).astype(BF16)
    m_ref[:, 2 * M_WIDTH:3 * M_WIDTH] = zm[:, 2 * M_WIDTH:3 * M_WIDTH].astype(BF16)
    m_ref[:, 3 * M_WIDTH:] = _sigmoid(zm[:, 3 * M_WIDTH:]).astype(BF16)

    zg = jnp.dot(h, wg_ref[...], preferred_element_type=F32)
    gate_ref[...] = zg[:, :N_GATE_PRE]
    gate_t_ref[...] = zg.T[:N_GATE_PRE, :]

    zs = jnp.dot(h, ws_ref[...], preferred_element_type=F32)
    sg_ref[...] = _sigmoid(zs).astype(BF16)


def _in_proj(x2, p, S):
    T = x2.shape[0]
    tm = TOKEN_TILE
    spt = S // tm
    row = lambda i: (i, 0)
    pos = lambda i: (i % spt, 0)
    tab = pl.BlockSpec((tm, LANES), pos)
    return pl.pallas_call(
        _in_proj_kernel,
        grid=(T // tm,),
        in_specs=[
            pl.BlockSpec((tm, D_MODEL), row),
            _const_spec((1, D_MODEL)),
            _const_spec(p["wa"].shape), _const_spec(p["wm"].shape),
            _const_spec(p["wg"].shape), _const_spec(p["ws"].shape),
            _const_spec((MXU_DIM, MXU_DIM)),
            tab, tab, tab, tab,
        ],
        out_specs=[
            pl.BlockSpec((tm, ATTN_WIDTH), row),
            pl.BlockSpec((tm, KV_WIDTH), row),
            pl.BlockSpec((None, N_KV_HEADS, tm // ATTN_KV_CHUNK, HEAD_DIM, ATTN_KV_CHUNK),
                         lambda i: (i // spt, 0, i % spt, 0, 0)),
            pl.BlockSpec((tm, 4 * M_WIDTH), row),
            pl.BlockSpec((tm, N_GATE_PRE), row),
            pl.BlockSpec((N_GATE_PRE, tm), lambda i: (0, i)),
            pl.BlockSpec((tm, 2 * D_MODEL), row),
        ],
        out_shape=[
            jax.ShapeDtypeStruct((T, ATTN_WIDTH), BF16),
            jax.ShapeDtypeStruct((T, KV_WIDTH), BF16),
            jax.ShapeDtypeStruct((T // S, N_KV_HEADS, S // ATTN_KV_CHUNK, HEAD_DIM, ATTN_KV_CHUNK), BF16),
            jax.ShapeDtypeStruct((T, 4 * M_WIDTH), BF16),
            jax.ShapeDtypeStruct((T, N_GATE_PRE), F32),
            jax.ShapeDtypeStruct((N_GATE_PRE, T), F32),
            jax.ShapeDtypeStruct((T, 2 * D_MODEL), BF16),
        ],
        compiler_params=pltpu.CompilerParams(
            dimension_semantics=("parallel",), vmem_limit_bytes=VMEM_LIMIT),
        name="in_proj",
    )(x2, p["g_mix_pre"], p["wa"], p["wm"], p["wg"], p["ws"], p["bd"],
      p["tqa"][S], p["tqb"][S], p["tka"][S], p["tkb"][S])


def _mlstm_stages(qf_ref, kf_ref, vf_ref, qb_ref, kb_ref, vb_ref, gcf_ref, gcb_ref, grf_ref, grb_ref, bc_ref, br_ref,
                  hf_ref, hb_ref, c_ref, m_ref, n_steps):
    L = M_CHUNK
    ri = lax.broadcasted_iota(jnp.int32, (L, L), 0)
    ci = lax.broadcasted_iota(jnp.int32, (L, L), 1)
    lower = ci <= ri
    upper = ci >= ri
    lower_b = lower.astype(BF16)
    upper_b = upper.astype(BF16)
    ones = jnp.ones((L, M_HEAD_DIM), BF16)
    fwd_cols = lax.broadcasted_iota(jnp.int32, (L, N_GATE_PRE), 1) < N_GATE_PRE // 2
    fwd_rows = lax.broadcasted_iota(jnp.int32, (N_GATE_PRE, L), 0) < N_GATE_PRE // 2

    def split_dot(tri, x, tri_first):
        hi = x.astype(BF16)
        lo = (x - hi.astype(F32)).astype(BF16)
        if tri_first:
            return jnp.dot(tri, hi, preferred_element_type=F32) + jnp.dot(tri, lo, preferred_element_type=F32)
        return jnp.dot(hi, tri, preferred_element_type=F32) + jnp.dot(lo, tri, preferred_element_type=F32)

    dirs = ((qf_ref, kf_ref, vf_ref, gcf_ref, grf_ref, hf_ref, lower),
            (qb_ref, kb_ref, vb_ref, gcb_ref, grb_ref, hb_ref, upper))

    for t in range(n_steps):
        rows = (slice(t * L, (t + 1) * L), slice((n_steps - 1 - t) * L, (n_steps - t) * L))
        chains = []
        for d, (q_ref, k_ref, v_ref, _, _, h_ref, mask) in enumerate(dirs):
            for hd in range(M_HEADS):
                cols = slice(hd * M_HEAD_DIM, (hd + 1) * M_HEAD_DIM)
                st = d * M_HEADS + hd
                ch = dict(st=st, d=d, hd=hd, mask=mask, h_ref=h_ref, cols=cols,
                          q=q_ref[rows[d], cols], k=k_ref[rows[d], cols],
                          v_ext=jnp.concatenate([v_ref[rows[d], cols], ones], axis=1),
                          m_old=m_ref[st:st + 1, 0:1],
                          c_old=c_ref[st])
                ch["qk"] = lax.dot_general(ch["q"], ch["k"], (((1,), (1,)), ((), ())), preferred_element_type=F32)
                chains.append(ch)
        yield

        gcol = jnp.where(fwd_cols, gcf_ref[rows[0], :], gcb_ref[rows[1], :]) + bc_ref[...]
        grow = jnp.where(fwd_rows, grf_ref[:, rows[0]], grb_ref[:, rows[1]]) + br_ref[...]
        lf_col = jax.nn.log_sigmoid(gcol)
        lf_row = jax.nn.log_sigmoid(grow)
        pre_col = split_dot(lower_b, lf_col, True)
        pre_row = split_dot(upper_b, lf_row, False)
        tot_col = pre_col[L - 1:L, :]
        tot_row = pre_row[:, L - 1:L]
        for ch in chains:
            gi = 2 * ch["d"] * M_HEADS + ch["hd"]
            gf = gi + M_HEADS
            b_row = pre_row[gf:gf + 1, :]
            b_col = pre_col[:, gf:gf + 1]
            if ch["d"] == 1:
                b_row = tot_row[gf:gf + 1, :] - b_row + lf_row[gf:gf + 1, :]
                b_col = tot_col[:, gf:gf + 1] - b_col + lf_col[:, gf:gf + 1]
            ch["b_col"] = b_col
            ch["r_row"] = grow[gi:gi + 1, :] - b_row
            ch["b_tot"] = tot_row[gf:gf + 1, :]
        yield

        for ch in chains:
            ws_row = ch["b_tot"] + ch["r_row"]
            m_new = jnp.maximum(ch["b_tot"] + ch["m_old"], jnp.max(ws_row, axis=1, keepdims=True))
            a = jnp.exp(ch["b_tot"] + ch["m_old"] - m_new)
            w_row = jnp.exp(ws_row - m_new)
            kw = (ch["k"].astype(F32).T * w_row).astype(BF16)
            st = ch["st"]
            c_ref[st] = a * ch["c_old"] + jnp.dot(kw, ch["v_ext"], preferred_element_type=F32)
            m_ref[st:st + 1, :] = jnp.broadcast_to(m_new, (1, LANES))
        yield

        for ch in chains:
            r = jnp.where(ch["mask"], ch["r_row"], -jnp.inf)
            m_col = jnp.maximum(jnp.max(r, axis=1, keepdims=True), ch["m_old"])
            dec = jnp.exp(ch["m_old"] - m_col)
            ch["lhs"] = jnp.concatenate([(jnp.exp(r - m_col) * ch["qk"]).astype(BF16),
                                         (ch["q"].astype(F32) * dec).astype(BF16)], axis=1)
            ch["floor"] = jnp.exp(-(ch["b_col"] + m_col))
        yield

        for ch in chains:
            rhs = jnp.concatenate([ch["v_ext"], ch["c_old"].astype(BF16)], axis=0)
            ext = jnp.dot(ch["lhs"], rhs, preferred_element_type=F32)
            num, den = ext[:, :M_HEAD_DIM], ext[:, M_HEAD_DIM:]
            h = num / jnp.maximum(jnp.abs(den), ch["floor"])
            ch["h_ref"][rows[ch["d"]], ch["cols"]] = h.astype(ch["h_ref"].dtype)
        yield


MLSTM_STAGES_PER_CHUNK = 5


def _mixer_kernel(q_ref, k_ref, vt_ref,
                  qf_ref, kf_ref, vf_ref, qb_ref, kb_ref, vb_ref, gcf_ref, gcb_ref, grf_ref, grb_ref, bc_ref, br_ref,
                  o_ref, hf_ref, hb_ref,
                  w_ref, p0_ref, p1_ref, acc_ref, l_ref, c_ref, m_ref, *, n_chunks, tq, kc, stabilise):
    row = lax.broadcasted_iota(jnp.int32, (LANES, tq), 0)
    low = row < HEAD_DIM
    for j in range(GQA_GROUP):
        qt = q_ref[:, j * LANES:(j + 1) * LANES].astype(F32).T
        w_ref[:, (2 * j) * tq:(2 * j + 1) * tq] = jnp.where(low, qt, 0.0).astype(BF16)
        w_ref[:, (2 * j + 1) * tq:(2 * j + 2) * tq] = jnp.where(low, 0.0, qt).astype(BF16)
    acc_ref[...] = jnp.zeros(acc_ref.shape, F32)
    l_ref[...] = jnp.zeros(l_ref.shape, F32)

    @pl.when(pl.program_id(1) == 0)
    def _():
        c_ref[...] = jnp.zeros(c_ref.shape, F32)
        m_ref[...] = jnp.zeros(m_ref.shape, F32)

    def step(c, ms, src=None, dst=None):
        new_ms = []
        if dst is not None:
            start = (c + 1) * kc
            if not isinstance(start, int):
                start = pl.multiple_of(start, kc)
            kch = k_ref[pl.ds(start, kc), :]
        for slot in range(N_Q_HEADS):
            cols = slice(slot * tq, (slot + 1) * tq)
            if dst is not None:
                s = jnp.dot(kch, w_ref[:, cols], preferred_element_type=F32)
                if stabilise:
                    m = ms[slot][0]
                    mn = jnp.maximum(m, jnp.max(s, axis=0, keepdims=True))
                    new_ms.append((mn, jnp.exp2(m - mn)))
                    s = s - mn
                    l_ref[slot] = l_ref[slot] * new_ms[-1][1]
                pr = jnp.exp2(s)
                dst[:, cols] = pr.astype(BF16)
                l_ref[slot] += jnp.sum(pr.reshape(kc // 8, 8, tq), axis=0)
            if src is not None:
                pv = jnp.dot(vt_ref[slot % N_KV_HEADS, c], src[:, cols], preferred_element_type=F32)
                if stabilise:
                    acc_ref[slot] = acc_ref[slot] * ms[slot][1] + pv
                else:
                    acc_ref[slot] += pv
        return ms if dst is None else tuple(new_ms)

    mlstm_steps = tq // M_CHUNK
    mlstm = _mlstm_stages(qf_ref, kf_ref, vf_ref, qb_ref, kb_ref, vb_ref, gcf_ref, gcb_ref, grf_ref, grb_ref,
                          bc_ref, br_ref, hf_ref, hb_ref, c_ref, m_ref, mlstm_steps)
    if stabilise:
        neg_inf = jnp.full((1, tq), -jnp.inf, F32)
        state = step(-1, tuple((neg_inf, neg_inf) for _ in range(N_Q_HEADS)), dst=p0_ref)
        state = lax.fori_loop(
            0, n_chunks // 2 - 1,
            lambda j, ms: step(2 * j + 1, step(2 * j, ms, src=p0_ref, dst=p1_ref), src=p1_ref, dst=p0_ref), state)
        state = step(n_chunks - 2, state, src=p0_ref, dst=p1_ref)
        step(n_chunks - 1, state, src=p1_ref)
        for _ in mlstm:
            pass
    else:
        bufs = (p0_ref, p1_ref)
        steps = ([(-1, None, p0_ref)]
                 + [(c, bufs[c % 2], bufs[(c + 1) % 2]) for c in range(n_chunks - 1)]
                 + [(n_chunks - 1, bufs[(n_chunks - 1) % 2], None)])
        n_stages = mlstm_steps * MLSTM_STAGES_PER_CHUNK
        done = 0
        for idx, (c, src, dst) in enumerate(steps):
            step(c, (), src=src, dst=dst)
            while done < ((idx + 1) * n_stages) // len(steps):
                next(mlstm)
                done += 1

    outs = [acc_ref[slot] / jnp.sum(l_ref[slot], axis=0, keepdims=True) for slot in range(N_Q_HEADS)]
    o_ref[...] = jnp.concatenate(outs, axis=0).T.astype(BF16)


def _mixers(q, k, vt, m4, gate, gate_t, p, B, S, stabilise):
    T = q.shape[0]
    tq, kc = ATTN_Q_TILE, ATTN_KV_CHUNK
    nq, n_chunks = S // tq, S // kc
    kern = functools.partial(_mixer_kernel, n_chunks=n_chunks, tq=tq, kc=kc, stabilise=stabilise)
    fwd = lambda col: (lambda b, i: (b * nq + i, col))
    bwd = lambda col: (lambda b, i: (b * nq + nq - 1 - i, col))
    blk = lambda f: pl.BlockSpec((tq, M_WIDTH), f)
    return pl.pallas_call(
        kern,
        grid=(B, nq),
        in_specs=[
            pl.BlockSpec((tq, ATTN_WIDTH), fwd(0)),
            pl.BlockSpec((S, KV_WIDTH), lambda b, i: (b, 0)),
            pl.BlockSpec((None, N_KV_HEADS, n_chunks, HEAD_DIM, kc), lambda b, i: (b, 0, 0, 0, 0)),
            blk(fwd(0)), blk(fwd(1)), blk(fwd(2)),
            blk(bwd(0)), blk(bwd(1)), blk(bwd(2)),
            pl.BlockSpec((tq, N_GATE_PRE), fwd(0)),
            pl.BlockSpec((tq, N_GATE_PRE), bwd(0)),
            pl.BlockSpec((N_GATE_PRE, tq), lambda b, i: (0, b * nq + i)),
            pl.BlockSpec((N_GATE_PRE, tq), lambda b, i: (0, b * nq + nq - 1 - i)),
            _const_spec((1, N_GATE_PRE)), _const_spec((N_GATE_PRE, 1)),
        ],
        out_specs=[pl.BlockSpec((tq, ATTN_WIDTH), fwd(0)), blk(fwd(0)), blk(bwd(0))],
        out_shape=[jax.ShapeDtypeStruct((T, ATTN_WIDTH), BF16)] + [jax.ShapeDtypeStruct((T, M_WIDTH), BF16)] * 2,
        scratch_shapes=[
            pltpu.VMEM((LANES, N_Q_HEADS * tq), BF16),
            pltpu.VMEM((kc, N_Q_HEADS * tq), BF16),
            pltpu.VMEM((kc, N_Q_HEADS * tq), BF16),
            pltpu.VMEM((N_Q_HEADS, HEAD_DIM, tq), F32),
            pltpu.VMEM((N_Q_HEADS, 8, tq), F32),
            pltpu.VMEM((2 * M_HEADS, M_HEAD_DIM, 2 * M_HEAD_DIM), F32),
            pltpu.VMEM((2 * M_HEADS, LANES), F32),
        ],
        compiler_params=pltpu.CompilerParams(
            dimension_semantics=("parallel", "arbitrary"), vmem_limit_bytes=VMEM_LIMIT),
        name="mixers",
    )(q, k, vt, m4, m4, m4, m4, m4, m4, gate, gate, gate_t, gate_t, p["b_if_row"], p["b_if_col"])


def _merge_kernel(a_ref, hf_ref, hb_ref, so_ref, sg_ref, x_ref, gm_ref, wao_ref, wmo_ref, wout_ref, gp_ref, o_ref):
    hs = hf_ref[...].astype(F32) + hb_ref[...].astype(F32)
    parts = []
    for hd in range(M_HEADS):
        blk = hs[:, hd * M_HEAD_DIM:(hd + 1) * M_HEAD_DIM]
        ms = jnp.mean(blk * blk, axis=-1, keepdims=True)
        parts.append(blk * lax.rsqrt(ms + EPS))
    hn = jnp.concatenate(parts, axis=1) * gm_ref[...]
    hm = (hn * so_ref[...].astype(F32)).astype(BF16)
    a_out = jnp.dot(a_ref[...], wao_ref[...], preferred_element_type=F32)
    m_out = jnp.dot(hm, wmo_ref[...], preferred_element_type=F32)
    sg = sg_ref[...]
    merged = sg[:, :D_MODEL].astype(F32) * a_out + sg[:, D_MODEL:].astype(F32) * m_out
    y = jnp.dot(merged.astype(BF16), wout_ref[...], preferred_element_type=F32)
    ms = jnp.mean(y * y, axis=-1, keepdims=True)
    o_ref[...] = x_ref[...] + y * lax.rsqrt(ms + EPS) * gp_ref[...]


def _merge(a, hf, hb, m4, sg, x2, p):
    T = x2.shape[0]
    tm = TOKEN_TILE
    row = lambda i: (i, 0)
    return pl.pallas_call(
        _merge_kernel,
        grid=(T // tm,),
        in_specs=[
            pl.BlockSpec((tm, ATTN_WIDTH), row),
            pl.BlockSpec((tm, M_WIDTH), row),
            pl.BlockSpec((tm, M_WIDTH), row),
            pl.BlockSpec((tm, M_WIDTH), lambda i: (i, 3)),
            pl.BlockSpec((tm, 2 * D_MODEL), row),
            pl.BlockSpec((tm, D_MODEL), row),
            _const_spec((1, M_WIDTH)),
            _const_spec((ATTN_WIDTH, D_MODEL)), _const_spec((M_WIDTH, D_MODEL)),
            _const_spec((D_MODEL, D_MODEL)), _const_spec((1, D_MODEL)),
        ],
        out_specs=pl.BlockSpec((tm, D_MODEL), row),
        out_shape=jax.ShapeDtypeStruct((T, D_MODEL), F32),
        compiler_params=pltpu.CompilerParams(
            dimension_semantics=("parallel",), vmem_limit_bytes=VMEM_LIMIT),
        name="merge",
    )(a, hf, hb, m4, sg, x2, p["g_mlstm"], p["wao"], p["wmo"], p["wout"], p["g_mix_post"])


def _ffn_kernel(xp_ref, x_ref, xn_ref, g_ref, wg_ref, wv_ref, cw_ref, cb_ref, wd_ref, gp_ref, o_ref,
                h_ref, ge_ref, *, tiles_per_seq):
    tm = x_ref.shape[0]
    g = g_ref[...]

    def norm(x):
        ms = jnp.mean(x * x, axis=-1, keepdims=True)
        return (x * lax.rsqrt(ms + EPS) * g).astype(BF16)

    h_ref[0:HALO, :] = norm(xp_ref[...])
    h_ref[HALO:HALO + tm, :] = norm(x_ref[...])
    h_ref[HALO + tm:, :] = norm(xn_ref[...])
    pos = pl.program_id(0) % tiles_per_seq
    keep_prev = (pos != 0).astype(F32)
    keep_next = (pos != tiles_per_seq - 1).astype(F32)

    ge_ref[...] = jnp.dot(h_ref[...], wg_ref[...], preferred_element_type=F32)
    ge_ref[HALO - 1:HALO, :] = ge_ref[HALO - 1:HALO, :] * keep_prev
    ge_ref[HALO + tm:HALO + tm + 1, :] = ge_ref[HALO + tm:HALO + tm + 1, :] * keep_next
    val = jnp.dot(h_ref[HALO:HALO + tm, :], wv_ref[...], preferred_element_type=F32)
    cw = cw_ref[...]
    conv = (ge_ref[HALO - 1:HALO - 1 + tm, :] * cw[0:1]
            + ge_ref[HALO:HALO + tm, :] * cw[1:2]
            + ge_ref[HALO + 1:HALO + 1 + tm, :] * cw[2:3]
            + cb_ref[...])
    act = (jax.nn.gelu(conv, approximate=True) * val).astype(BF16)
    y = jnp.dot(act, wd_ref[...], preferred_element_type=F32)
    ms = jnp.mean(y * y, axis=-1, keepdims=True)
    o_ref[...] = x_ref[...] + y * lax.rsqrt(ms + EPS) * gp_ref[...]


def _ffn(x1, p, S):
    T = x1.shape[0]
    tm = FFN_TOKEN_TILE
    hpt = tm // HALO
    n_halo = T // HALO
    kern = functools.partial(_ffn_kernel, tiles_per_seq=S // tm)
    return pl.pallas_call(
        kern,
        grid=(T // tm,),
        in_specs=[
            pl.BlockSpec((HALO, D_MODEL), lambda i: (jnp.maximum(i * hpt - 1, 0), 0)),
            pl.BlockSpec((tm, D_MODEL), lambda i: (i, 0)),
            pl.BlockSpec((HALO, D_MODEL), lambda i: (jnp.minimum((i + 1) * hpt, n_halo - 1), 0)),
            _const_spec((1, D_MODEL)),
            _const_spec((D_MODEL, D_FF)), _const_spec((D_MODEL, D_FF)),
            _const_spec((8, D_FF)), _const_spec((1, D_FF)),
            _const_spec((D_FF, D_MODEL)), _const_spec((1, D_MODEL)),
        ],
        out_specs=pl.BlockSpec((tm, D_MODEL), lambda i: (i, 0)),
        out_shape=jax.ShapeDtypeStruct((T, D_MODEL), F32),
        scratch_shapes=[
            pltpu.VMEM((tm + 2 * HALO, D_MODEL), BF16),
            pltpu.VMEM((tm + 2 * HALO, D_FF), F32),
        ],
        compiler_params=pltpu.CompilerParams(
            dimension_semantics=("parallel",), vmem_limit_bytes=VMEM_LIMIT),
        name="ffn",
    )(x1, x1, x1, p["g_ffn_pre"], p["wup_g"], p["wup_v"], p["conv_w"], p["conv_b"], p["wdown"], p["g_ffn_post"])


def _rope_tables(S, q_gain, k_gain):
    rows = S // GRID_W
    r = jnp.repeat(jnp.arange(rows, dtype=F32), GRID_W)
    c = jnp.tile(jnp.arange(GRID_W, dtype=F32), rows)
    n_pairs_axis = HEAD_DIM // 4
    freq = ROPE_THETA ** (-jnp.arange(n_pairs_axis, dtype=F32) / n_pairs_axis)
    ang = jnp.concatenate([r[:, None] * freq, c[:, None] * freq], axis=-1)
    cos, sin = jnp.cos(ang), jnp.sin(ang)
    cc = jnp.concatenate([cos, cos], axis=-1)
    ss = jnp.concatenate([-sin, sin], axis=-1)

    def tables(gain, scale):
        g_half = jnp.concatenate([gain[0::2], gain[1::2]])
        g_swap = jnp.concatenate([gain[1::2], gain[0::2]])
        ta = cc * g_half * scale
        tb = ss * g_swap * scale
        return jnp.tile(ta, (1, LANES // HEAD_DIM)), jnp.tile(tb, (1, LANES // HEAD_DIM))

    tqa, tqb = tables(q_gain, LOG2E * HEAD_DIM ** -0.5)
    tka, tkb = tables(k_gain, 1.0)
    return tqa, tqb, tka, tkb


def _prepare(norm_mix_pre, w_in, b_if, q_norm, k_norm, mlstm_norm, w_attn_o, w_mlstm_o, w_out,
             norm_mix_post, norm_ffn_pre, w_up, conv_w, conv_b, w_down, norm_ffn_post, seq_lens):
    half = np.concatenate([np.arange(0, HEAD_DIM, 2), np.arange(1, HEAD_DIM, 2)])
    slots = [h for j in range(GQA_GROUP) for h in (j, j + GQA_GROUP)]
    q_cols = np.concatenate([h * HEAD_DIM + half for h in slots])
    k_cols = ATTN_WIDTH + np.concatenate([h * HEAD_DIM + half for h in range(N_KV_HEADS)])
    v_cols = ATTN_WIDTH + KV_WIDTH + np.arange(KV_WIDTH)
    o_rows = np.concatenate([h * HEAD_DIM + np.arange(HEAD_DIM) for h in slots])
    m0 = ATTN_WIDTH + 2 * KV_WIDTH
    g0 = m0 + 4 * M_WIDTH
    s0 = g0 + N_GATE_PRE

    p = {}
    p["wa"] = w_in[:, np.concatenate([q_cols, k_cols, v_cols])].astype(BF16)
    p["wm"] = w_in[:, m0:g0].astype(BF16)
    p["wg"] = jnp.pad(w_in[:, g0:s0], ((0, 0), (0, LANES - N_GATE_PRE))).astype(BF16)
    p["ws"] = w_in[:, s0:].astype(BF16)
    blk = np.arange(MXU_DIM) // HEAD_DIM
    p["bd"] = jnp.asarray((blk[:, None] == blk[None, :]).astype(np.float32) / HEAD_DIM, BF16)
    p["g_mix_pre"] = norm_mix_pre.reshape(1, D_MODEL)
    p["b_if_row"] = b_if.reshape(1, N_GATE_PRE)
    p["b_if_col"] = b_if.reshape(N_GATE_PRE, 1)
    p["g_mlstm"] = mlstm_norm.reshape(1, M_WIDTH)
    p["wao"] = w_attn_o[o_rows].astype(BF16)
    p["wmo"] = w_mlstm_o.astype(BF16)
    p["wout"] = w_out.astype(BF16)
    p["g_mix_post"] = norm_mix_post.reshape(1, D_MODEL)
    p["g_ffn_pre"] = norm_ffn_pre.reshape(1, D_MODEL)
    p["wup_g"] = w_up[:, :D_FF].astype(BF16)
    p["wup_v"] = w_up[:, D_FF:].astype(BF16)
    p["conv_w"] = jnp.pad(conv_w, ((0, 8 - CONV_W), (0, 0)))
    p["conv_b"] = conv_b.reshape(1, D_FF)
    p["wdown"] = w_down.astype(BF16)
    p["g_ffn_post"] = norm_ffn_post.reshape(1, D_MODEL)
    p["score_bound"] = (1.02 * LOG2E * HEAD_DIM ** 0.5) * jnp.max(jnp.abs(q_norm)) * jnp.max(jnp.abs(k_norm))
    p["tqa"], p["tqb"], p["tka"], p["tkb"] = {}, {}, {}, {}
    for S in seq_lens:
        p["tqa"][S], p["tqb"][S], p["tka"][S], p["tkb"][S] = _rope_tables(S, q_norm, k_norm)
    return p


def _trunk(x, p):
    B, S, _ = x.shape
    T = B * S
    x2 = x.reshape(T, D_MODEL)
    q, k, vt, m4, gate, gate_t, sg = _in_proj(x2, p, S)
    a, hf, hb = lax.cond(p["score_bound"] <= MAX_UNSTABILISED_SCORE,
                         lambda: _mixers(q, k, vt, m4, gate, gate_t, p, B, S, False),
                         lambda: _mixers(q, k, vt, m4, gate, gate_t, p, B, S, True))
    x1 = _merge(a, hf, hb, m4, sg, x2, p)
    y = _ffn(x1, p, S)
    return y.reshape(B, S, D_MODEL)


def kernel(x_prompt, x_sample, norm_mix_pre, w_in, b_if, q_norm, k_norm, mlstm_norm, w_attn_o, w_mlstm_o,
           w_out, norm_mix_post, norm_ffn_pre, w_up, conv_w, conv_b, w_down, norm_ffn_post):
    depth = w_in.shape[0]
    seq_lens = sorted({x_prompt.shape[1], x_sample.shape[1]})
    layers = [
        _prepare(norm_mix_pre[l], w_in[l], b_if[l], q_norm[l], k_norm[l], mlstm_norm[l], w_attn_o[l],
                 w_mlstm_o[l], w_out[l], norm_mix_post[l], norm_ffn_pre[l], w_up[l], conv_w[l], conv_b[l],
                 w_down[l], norm_ffn_post[l], seq_lens)
        for l in range(depth)
    ]

    def trunk(x):
        for p in layers:
            x = _trunk(x, p)
        return x

    return (trunk(x_prompt), trunk(x_sample))
```

```python
import functools

import numpy as np
import jax
import jax.numpy as jnp
from jax import lax
from jax.experimental import pallas as pl
from jax.experimental.pallas import tpu as pltpu

D_MODEL = 1024
GRID_W = 64
N_Q_HEADS = 8
N_KV_HEADS = 2
GQA_GROUP = N_Q_HEADS // N_KV_HEADS
HEAD_DIM = 64
ATTN_WIDTH = N_Q_HEADS * HEAD_DIM
KV_WIDTH = N_KV_HEADS * HEAD_DIM
ROPE_THETA = 10000.0
M_HEADS = 4
M_HEAD_DIM = 128
M_WIDTH = M_HEADS * M_HEAD_DIM
N_GATE_PRE = 4 * M_HEADS
D_FF = 2816
CONV_W = 3
EPS = 1e-6

LANES = 128
MXU_DIM = 256
VMEM_LIMIT = 56 * 1024 * 1024

TOKEN_TILE = 512
ATTN_Q_TILE = 512
ATTN_KV_CHUNK = 256
M_CHUNK = 128
FFN_TOKEN_TILE = 512
FFN_ROW_SPLIT = 2
HALO = 16
LOG2E = 1.4426950408889634
MAX_UNSTABILISED_SCORE = 40.0

BF16 = jnp.bfloat16
F32 = jnp.float32


def _sigmoid(x):
    return 0.5 * jnp.tanh(0.5 * x) + 0.5


def _const_spec(shape):
    n = len(shape)
    return pl.BlockSpec(shape, lambda *_: (0,) * n, pipeline_mode=pl.Buffered(1))


def _in_proj_kernel(x_ref, g_ref, wa_ref, wm_ref, wg_ref, ws_ref, bd_ref,
                    tqa_ref, tqb_ref, tka_ref, tkb_ref,
                    q_ref, k_ref, vt_ref, m_ref, gate_ref, gate_t_ref, sg_ref):
    x = x_ref[...]
    ms = jnp.mean(x * x, axis=-1, keepdims=True)
    h = (x * lax.rsqrt(ms + EPS) * g_ref[...]).astype(BF16)

    za = jnp.dot(h, wa_ref[...], preferred_element_type=F32)
    bd = bd_ref[...]
    lane = lax.broadcasted_iota(jnp.int32, (x.shape[0], LANES), 1)
    first_half = (lane % HEAD_DIM) < (HEAD_DIM // 2)

    def head_norm_rope(z, msq, ta, tb):
        sw = jnp.where(first_half, pltpu.roll(z, LANES - HEAD_DIM // 2, 1), pltpu.roll(z, HEAD_DIM // 2, 1))
        return lax.rsqrt(msq + EPS) * (z * ta + sw * tb)

    tqa, tqb = tqa_ref[...], tqb_ref[...]
    for half in range(ATTN_WIDTH // MXU_DIM):
        zq = za[:, half * MXU_DIM:(half + 1) * MXU_DIM]
        msq = jnp.dot((zq * zq).astype(BF16), bd, preferred_element_type=F32)
        for j in range(MXU_DIM // LANES):
            c0 = half * MXU_DIM + j * LANES
            out = head_norm_rope(zq[:, j * LANES:(j + 1) * LANES], msq[:, j * LANES:(j + 1) * LANES], tqa, tqb)
            q_ref[:, c0:c0 + LANES] = out.astype(BF16)
    zk = za[:, ATTN_WIDTH:ATTN_WIDTH + KV_WIDTH]
    msk = jnp.dot((zk * zk).astype(BF16), bd[:LANES, :LANES], preferred_element_type=F32)
    k_ref[...] = head_norm_rope(zk, msk, tka_ref[...], tkb_ref[...]).astype(BF16)
    v_t = za[:, ATTN_WIDTH + KV_WIDTH:].T
    for kvh in range(N_KV_HEADS):
        for c in range(x.shape[0] // ATTN_KV_CHUNK):
            vt_ref[kvh, c] = v_t[kvh * HEAD_DIM:(kvh + 1) * HEAD_DIM,
                                 c * ATTN_KV_CHUNK:(c + 1) * ATTN_KV_CHUNK].astype(BF16)

    zm = jnp.dot(h, wm_ref[...], preferred_element_type=F32)
    m_ref[:, 0:M_WIDTH] = zm[:, 0:M_WIDTH].astype(BF16)
    m_ref[:, M_WIDTH:2 * M_WIDTH] = (zm[:, M_WIDTH:2 * M_WIDTH] * (M_HEAD_DIM ** -0.5)).astype(BF16)
    m_ref[:, 2 * M_WIDTH:3 * M_WIDTH] = zm[:, 2 * M_WIDTH:3 * M_WIDTH].astype(BF16)
    m_ref[:, 3 * M_WIDTH:] = _sigmoid(zm[:, 3 * M_WIDTH:]).astype(BF16)

    zg = jnp.dot(h, wg_ref[...], preferred_element_type=F32)
    gate_ref[...] = zg[:, :N_GATE_PRE]
    gate_t_ref[...] = zg.T[:N_GATE_PRE, :]

    zs = jnp.dot(h, ws_ref[...], preferred_element_type=F32)
    sg_ref[...] = _sigmoid(zs).astype(BF16)


def _in_proj(x2, p, S):
    T = x2.shape[0]
    tm = TOKEN_TILE
    spt = S // tm
    row = lambda i: (i, 0)
    pos = lambda i: (i % spt, 0)
    tab = pl.BlockSpec((tm, LANES), pos)
    return pl.pallas_call(
        _in_proj_kernel,
        grid=(T // tm,),
        in_specs=[
            pl.BlockSpec((tm, D_MODEL), row),
            _const_spec((1, D_MODEL)),
            _const_spec(p["wa"].shape), _const_spec(p["wm"].shape),
            _const_spec(p["wg"].shape), _const_spec(p["ws"].shape),
            _const_spec((MXU_DIM, MXU_DIM)),
            tab, tab, tab, tab,
        ],
        out_specs=[
            pl.BlockSpec((tm, ATTN_WIDTH), row),
            pl.BlockSpec((tm, KV_WIDTH), row),
            pl.BlockSpec((None, N_KV_HEADS, tm // ATTN_KV_CHUNK, HEAD_DIM, ATTN_KV_CHUNK),
                         lambda i: (i // spt, 0, i % spt, 0, 0)),
            pl.BlockSpec((tm, 4 * M_WIDTH), row),
            pl.BlockSpec((tm, N_GATE_PRE), row),
            pl.BlockSpec((N_GATE_PRE, tm), lambda i: (0, i)),
            pl.BlockSpec((tm, 2 * D_MODEL), row),
        ],
        out_shape=[
            jax.ShapeDtypeStruct((T, ATTN_WIDTH), BF16),
            jax.ShapeDtypeStruct((T, KV_WIDTH), BF16),
            jax.ShapeDtypeStruct((T // S, N_KV_HEADS, S // ATTN_KV_CHUNK, HEAD_DIM, ATTN_KV_CHUNK), BF16),
            jax.ShapeDtypeStruct((T, 4 * M_WIDTH), BF16),
            jax.ShapeDtypeStruct((T, N_GATE_PRE), F32),
            jax.ShapeDtypeStruct((N_GATE_PRE, T), F32),
            jax.ShapeDtypeStruct((T, 2 * D_MODEL), BF16),
        ],
        compiler_params=pltpu.CompilerParams(
            dimension_semantics=("parallel",), vmem_limit_bytes=VMEM_LIMIT),
        name="in_proj",
    )(x2, p["g_mix_pre"], p["wa"], p["wm"], p["wg"], p["ws"], p["bd"],
      p["tqa"][S], p["tqb"][S], p["tka"][S], p["tkb"][S])


def _mlstm_stages(qf_ref, kf_ref, vf_ref, qb_ref, kb_ref, vb_ref, gcf_ref, gcb_ref, grf_ref, grb_ref, bc_ref, br_ref,
                  hf_ref, hb_ref, c_ref, m_ref, n_steps):
    L = M_CHUNK
    ri = lax.broadcasted_iota(jnp.int32, (L, L), 0)
    ci = lax.broadcasted_iota(jnp.int32, (L, L), 1)
    lower = ci <= ri
    upper = ci >= ri
    lower_b = lower.astype(BF16)
    upper_b = upper.astype(BF16)
    ones = jnp.ones((L, M_HEAD_DIM), BF16)
    fwd_cols = lax.broadcasted_iota(jnp.int32, (L, N_GATE_PRE), 1) < N_GATE_PRE // 2
    fwd_rows = lax.broadcasted_iota(jnp.int32, (N_GATE_PRE, L), 0) < N_GATE_PRE // 2

    def split_dot(tri, x, tri_first):
        hi = x.astype(BF16)
        lo = (x - hi.astype(F32)).astype(BF16)
        if tri_first:
            return jnp.dot(tri, hi, preferred_element_type=F32) + jnp.dot(tri, lo, preferred_element_type=F32)
        return jnp.dot(hi, tri, preferred_element_type=F32) + jnp.dot(lo, tri, preferred_element_type=F32)

    dirs = ((qf_ref, kf_ref, vf_ref, gcf_ref, grf_ref, hf_ref, lower),
            (qb_ref, kb_ref, vb_ref, gcb_ref, grb_ref, hb_ref, upper))

    for t in range(n_steps):
        rows = (slice(t * L, (t + 1) * L), slice((n_steps - 1 - t) * L, (n_steps - t) * L))
        chains = []
        for d, (q_ref, k_ref, v_ref, _, _, h_ref, mask) in enumerate(dirs):
            for hd in range(M_HEADS):
                cols = slice(hd * M_HEAD_DIM, (hd + 1) * M_HEAD_DIM)
                st = d * M_HEADS + hd
                ch = dict(st=st, d=d, hd=hd, mask=mask, h_ref=h_ref, cols=cols,
                          q=q_ref[rows[d], cols], k=k_ref[rows[d], cols],
                          v_ext=jnp.concatenate([v_ref[rows[d], cols], ones], axis=1),
                          m_old=m_ref[st:st + 1, 0:1],
                          c_old=c_ref[st])
                ch["qk"] = lax.dot_general(ch["q"], ch["k"], (((1,), (1,)), ((), ())), preferred_element_type=F32)
                chains.append(ch)
        yield

        gcol = jnp.where(fwd_cols, gcf_ref[rows[0], :], gcb_ref[rows[1], :]) + bc_ref[...]
        grow = jnp.where(fwd_rows, grf_ref[:, rows[0]], grb_ref[:, rows[1]]) + br_ref[...]
        lf_col = jax.nn.log_sigmoid(gcol)
        lf_row = jax.nn.log_sigmoid(grow)
        pre_col = split_dot(lower_b, lf_col, True)
        pre_row = split_dot(upper_b, lf_row, False)
        tot_col = pre_col[L - 1:L, :]
        tot_row = pre_row[:, L - 1:L]
        for ch in chains:
            gi = 2 * ch["d"] * M_HEADS + ch["hd"]
            gf = gi + M_HEADS
            b_row = pre_row[gf:gf + 1, :]
            b_col = pre_col[:, gf:gf + 1]
            if ch["d"] == 1:
                b_row = tot_row[gf:gf + 1, :] - b_row + lf_row[gf:gf + 1, :]
                b_col = tot_col[:, gf:gf + 1] - b_col + lf_col[:, gf:gf + 1]
            ch["b_col"] = b_col
            ch["r_row"] = grow[gi:gi + 1, :] - b_row
            ch["b_tot"] = tot_row[gf:gf + 1, :]
        yield

        for ch in chains:
            ws_row = ch["b_tot"] + ch["r_row"]
            m_new = jnp.maximum(ch["b_tot"] + ch["m_old"], jnp.max(ws_row, axis=1, keepdims=True))
            a = jnp.exp(ch["b_tot"] + ch["m_old"] - m_new)
            w_row = jnp.exp(ws_row - m_new)
            kw = (ch["k"].astype(F32).T * w_row).astype(BF16)
            st = ch["st"]
            c_ref[st] = a * ch["c_old"] + jnp.dot(kw, ch["v_ext"], preferred_element_type=F32)
            m_ref[st:st + 1, :] = jnp.broadcast_to(m_new, (1, LANES))
        yield

        for ch in chains:
            r = jnp.where(ch["mask"], ch["r_row"], -jnp.inf)
            m_col = jnp.maximum(jnp.max(r, axis=1, keepdims=True), ch["m_old"])
            dec = jnp.exp(ch["m_old"] - m_col)
            ch["lhs"] = jnp.concatenate([(jnp.exp(r - m_col) * ch["qk"]).astype(BF16),
                                         (ch["q"].astype(F32) * dec).astype(BF16)], axis=1)
            ch["floor"] = jnp.exp(-(ch["b_col"] + m_col))
        yield

        for ch in chains:
            rhs = jnp.concatenate([ch["v_ext"], ch["c_old"].astype(BF16)], axis=0)
            ext = jnp.dot(ch["lhs"], rhs, preferred_element_type=F32)
            num, den = ext[:, :M_HEAD_DIM], ext[:, M_HEAD_DIM:]
            h = num / jnp.maximum(jnp.abs(den), ch["floor"])
            ch["h_ref"][rows[ch["d"]], ch["cols"]] = h.astype(ch["h_ref"].dtype)
        yield


MLSTM_STAGES_PER_CHUNK = 5


def _mixer_kernel(q_ref, k_ref, vt_ref,
                  qf_ref, kf_ref, vf_ref, qb_ref, kb_ref, vb_ref, gcf_ref, gcb_ref, grf_ref, grb_ref, bc_ref, br_ref,
                  o_ref, hf_ref, hb_ref,
                  w_ref, p0_ref, p1_ref, acc_ref, l_ref, c_ref, m_ref, *, n_chunks, tq, kc, stabilise):
    row = lax.broadcasted_iota(jnp.int32, (LANES, tq), 0)
    low = row < HEAD_DIM
    for j in range(GQA_GROUP):
        qt = q_ref[:, j * LANES:(j + 1) * LANES].astype(F32).T
        w_ref[:, (2 * j) * tq:(2 * j + 1) * tq] = jnp.where(low, qt, 0.0).astype(BF16)
        w_ref[:, (2 * j + 1) * tq:(2 * j + 2) * tq] = jnp.where(low, 0.0, qt).astype(BF16)
    acc_ref[...] = jnp.zeros(acc_ref.shape, F32)
    l_ref[...] = jnp.zeros(l_ref.shape, F32)

    @pl.when(pl.program_id(1) == 0)
    def _():
        c_ref[...] = jnp.zeros(c_ref.shape, F32)
        m_ref[...] = jnp.zeros(m_ref.shape, F32)

    def step(c, ms, src=None, dst=None):
        new_ms = []
        if dst is not None:
            start = (c + 1) * kc
            if not isinstance(start, int):
                start = pl.multiple_of(start, kc)
            kch = k_ref[pl.ds(start, kc), :]
        for slot in range(N_Q_HEADS):
            cols = slice(slot * tq, (slot + 1) * tq)
            if dst is not None:
                s = jnp.dot(kch, w_ref[:, cols], preferred_element_type=F32)
                if stabilise:
                    m = ms[slot][0]
                    mn = jnp.maximum(m, jnp.max(s, axis=0, keepdims=True))
                    new_ms.append((mn, jnp.exp2(m - mn)))
                    s = s - mn
                    l_ref[slot] = l_ref[slot] * new_ms[-1][1]
                pr = jnp.exp2(s)
                dst[:, cols] = pr.astype(BF16)
                l_ref[slot] += jnp.sum(pr.reshape(kc // 8, 8, tq), axis=0)
            if src is not None:
                pv = jnp.dot(vt_ref[slot % N_KV_HEADS, c], src[:, cols], preferred_element_type=F32)
                if stabilise:
                    acc_ref[slot] = acc_ref[slot] * ms[slot][1] + pv
                else:
                    acc_ref[slot] += pv
        return ms if dst is None else tuple(new_ms)

    mlstm_steps = tq // M_CHUNK
    mlstm = _mlstm_stages(qf_ref, kf_ref, vf_ref, qb_ref, kb_ref, vb_ref, gcf_ref, gcb_ref, grf_ref, grb_ref,
                          bc_ref, br_ref, hf_ref, hb_ref, c_ref, m_ref, mlstm_steps)
    if stabilise:
        neg_inf = jnp.full((1, tq), -jnp.inf, F32)
        state = step(-1, tuple((neg_inf, neg_inf) for _ in range(N_Q_HEADS)), dst=p0_ref)
        state = lax.fori_loop(
            0, n_chunks // 2 - 1,
            lambda j, ms: step(2 * j + 1, step(2 * j, ms, src=p0_ref, dst=p1_ref), src=p1_ref, dst=p0_ref), state)
        state = step(n_chunks - 2, state, src=p0_ref, dst=p1_ref)
        step(n_chunks - 1, state, src=p1_ref)
        for _ in mlstm:
            pass
    else:
        bufs = (p0_ref, p1_ref)
        steps = ([(-1, None, p0_ref)]
                 + [(c, bufs[c % 2], bufs[(c + 1) % 2]) for c in range(n_chunks - 1)]
                 + [(n_chunks - 1, bufs[(n_chunks - 1) % 2], None)])
        n_stages = mlstm_steps * MLSTM_STAGES_PER_CHUNK
        done = 0
        for idx, (c, src, dst) in enumerate(steps):
            step(c, (), src=src, dst=dst)
            while done < ((idx + 1) * n_stages) // len(steps):
                next(mlstm)
                done += 1

    outs = [acc_ref[slot] / jnp.sum(l_ref[slot], axis=0, keepdims=True) for slot in range(N_Q_HEADS)]
    o_ref[...] = jnp.concatenate(outs, axis=0).T.astype(BF16)


def _mixers(q, k, vt, m4, gate, gate_t, p, B, S, stabilise):
    T = q.shape[0]
    tq, kc = ATTN_Q_TILE, ATTN_KV_CHUNK
    nq, n_chunks = S // tq, S // kc
    kern = functools.partial(_mixer_kernel, n_chunks=n_chunks, tq=tq, kc=kc, stabilise=stabilise)
    fwd = lambda col: (lambda b, i: (b * nq + i, col))
    bwd = lambda col: (lambda b, i: (b * nq + nq - 1 - i, col))
    blk = lambda f: pl.BlockSpec((tq, M_WIDTH), f)
    return pl.pallas_call(
        kern,
        grid=(B, nq),
        in_specs=[
            pl.BlockSpec((tq, ATTN_WIDTH), fwd(0)),
            pl.BlockSpec((S, KV_WIDTH), lambda b, i: (b, 0)),
            pl.BlockSpec((None, N_KV_HEADS, n_chunks, HEAD_DIM, kc), lambda b, i: (b, 0, 0, 0, 0)),
            blk(fwd(0)), blk(fwd(1)), blk(fwd(2)),
            blk(bwd(0)), blk(bwd(1)), blk(bwd(2)),
            pl.BlockSpec((tq, N_GATE_PRE), fwd(0)),
            pl.BlockSpec((tq, N_GATE_PRE), bwd(0)),
            pl.BlockSpec((N_GATE_PRE, tq), lambda b, i: (0, b * nq + i)),
            pl.BlockSpec((N_GATE_PRE, tq), lambda b, i: (0, b * nq + nq - 1 - i)),
            _const_spec((1, N_GATE_PRE)), _const_spec((N_GATE_PRE, 1)),
        ],
        out_specs=[pl.BlockSpec((tq, ATTN_WIDTH), fwd(0)), blk(fwd(0)), blk(bwd(0))],
        out_shape=[jax.ShapeDtypeStruct((T, ATTN_WIDTH), BF16)] + [jax.ShapeDtypeStruct((T, M_WIDTH), BF16)] * 2,
        scratch_shapes=[
            pltpu.VMEM((LANES, N_Q_HEADS * tq), BF16),
            pltpu.VMEM((kc, N_Q_HEADS * tq), BF16),
            pltpu.VMEM((kc, N_Q_HEADS * tq), BF16),
            pltpu.VMEM((N_Q_HEADS, HEAD_DIM, tq), F32),
            pltpu.VMEM((N_Q_HEADS, 8, tq), F32),
            pltpu.VMEM((2 * M_HEADS, M_HEAD_DIM, 2 * M_HEAD_DIM), F32),
            pltpu.VMEM((2 * M_HEADS, LANES), F32),
        ],
        compiler_params=pltpu.CompilerParams(
            dimension_semantics=("parallel", "arbitrary"), vmem_limit_bytes=VMEM_LIMIT),
        name="mixers",
    )(q, k, vt, m4, m4, m4, m4, m4, m4, gate, gate, gate_t, gate_t, p["b_if_row"], p["b_if_col"])


def _merge_kernel(a_ref, hf_ref, hb_ref, so_ref, sg_ref, x_ref, gm_ref, wao_ref, wmo_ref, wout_ref, gp_ref, o_ref):
    hs = hf_ref[...].astype(F32) + hb_ref[...].astype(F32)
    parts = []
    for hd in range(M_HEADS):
        blk = hs[:, hd * M_HEAD_DIM:(hd + 1) * M_HEAD_DIM]
        ms = jnp.mean(blk * blk, axis=-1, keepdims=True)
        parts.append(blk * lax.rsqrt(ms + EPS))
    hn = jnp.concatenate(parts, axis=1) * gm_ref[...]
    hm = (hn * so_ref[...].astype(F32)).astype(BF16)
    a_out = jnp.dot(a_ref[...], wao_ref[...], preferred_element_type=F32)
    m_out = jnp.dot(hm, wmo_ref[...], preferred_element_type=F32)
    sg = sg_ref[...]
    merged = sg[:, :D_MODEL].astype(F32) * a_out + sg[:, D_MODEL:].astype(F32) * m_out
    y = jnp.dot(merged.astype(BF16), wout_ref[...], preferred_element_type=F32)
    ms = jnp.mean(y * y, axis=-1, keepdims=True)
    o_ref[...] = x_ref[...] + y * lax.rsqrt(ms + EPS) * gp_ref[...]


def _merge(a, hf, hb, m4, sg, x2, p):
    T = x2.shape[0]
    tm = TOKEN_TILE
    row = lambda i: (i, 0)
    return pl.pallas_call(
        _merge_kernel,
        grid=(T // tm,),
        in_specs=[
            pl.BlockSpec((tm, ATTN_WIDTH), row),
            pl.BlockSpec((tm, M_WIDTH), row),
            pl.BlockSpec((tm, M_WIDTH), row),
            pl.BlockSpec((tm, M_WIDTH), lambda i: (i, 3)),
            pl.BlockSpec((tm, 2 * D_MODEL), row),
            pl.BlockSpec((tm, D_MODEL), row),
            _const_spec((1, M_WIDTH)),
            _const_spec((ATTN_WIDTH, D_MODEL)), _const_spec((M_WIDTH, D_MODEL)),
            _const_spec((D_MODEL, D_MODEL)), _const_spec((1, D_MODEL)),
        ],
        out_specs=pl.BlockSpec((tm, D_MODEL), row),
        out_shape=jax.ShapeDtypeStruct((T, D_MODEL), F32),
        compiler_params=pltpu.CompilerParams(
            dimension_semantics=("parallel",), vmem_limit_bytes=VMEM_LIMIT),
        name="merge",
    )(a, hf, hb, m4, sg, x2, p["g_mlstm"], p["wao"], p["wmo"], p["wout"], p["g_mix_post"])


def _ffn_kernel(xp_ref, x_ref, xn_ref, g_ref, wg_ref, wv_ref, cw_ref, cb_ref, wd_ref, gp_ref, o_ref,
                h_ref, ge_ref, *, tiles_per_seq):
    tm = x_ref.shape[0]
    g = g_ref[...]

    def norm(x):
        ms = jnp.mean(x * x, axis=-1, keepdims=True)
        return (x * lax.rsqrt(ms + EPS) * g).astype(BF16)

    h_ref[0:HALO, :] = norm(xp_ref[...])
    h_ref[HALO:HALO + tm, :] = norm(x_ref[...])
    h_ref[HALO + tm:, :] = norm(xn_ref[...])
    pos = pl.program_id(0) % tiles_per_seq
    keep_prev = (pos != 0).astype(F32)
    keep_next = (pos != tiles_per_seq - 1).astype(F32)

    ge_ref[...] = jnp.dot(h_ref[...], wg_ref[...], preferred_element_type=F32)
    ge_ref[HALO - 1:HALO, :] = ge_ref[HALO - 1:HALO, :] * keep_prev
    ge_ref[HALO + tm:HALO + tm + 1, :] = ge_ref[HALO + tm:HALO + tm + 1, :] * keep_next
    half = tm // FFN_ROW_SPLIT
    vals = [jnp.dot(h_ref[HALO + r * half:HALO + (r + 1) * half, :], wv_ref[...], preferred_element_type=F32)
            for r in range(FFN_ROW_SPLIT)]
    cw = cw_ref[...]
    for r in range(FFN_ROW_SPLIT):
        r0 = HALO + r * half
        conv = (ge_ref[r0 - 1:r0 - 1 + half, :] * cw[0:1]
                + ge_ref[r0:r0 + half, :] * cw[1:2]
                + ge_ref[r0 + 1:r0 + 1 + half, :] * cw[2:3]
                + cb_ref[...])
        act = (jax.nn.gelu(conv, approximate=True) * vals[r]).astype(BF16)
        y = jnp.dot(act, wd_ref[...], preferred_element_type=F32)
        ms = jnp.mean(y * y, axis=-1, keepdims=True)
        rows = slice(r * half, (r + 1) * half)
        o_ref[rows, :] = x_ref[rows, :] + y * lax.rsqrt(ms + EPS) * gp_ref[...]


def _ffn(x1, p, S):
    T = x1.shape[0]
    tm = FFN_TOKEN_TILE
    hpt = tm // HALO
    n_halo = T // HALO
    kern = functools.partial(_ffn_kernel, tiles_per_seq=S // tm)
    return pl.pallas_call(
        kern,
        grid=(T // tm,),
        in_specs=[
            pl.BlockSpec((HALO, D_MODEL), lambda i: (jnp.maximum(i * hpt - 1, 0), 0)),
            pl.BlockSpec((tm, D_MODEL), lambda i: (i, 0)),
            pl.BlockSpec((HALO, D_MODEL), lambda i: (jnp.minimum((i + 1) * hpt, n_halo - 1), 0)),
            _const_spec((1, D_MODEL)),
            _const_spec((D_MODEL, D_FF)), _const_spec((D_MODEL, D_FF)),
            _const_spec((8, D_FF)), _const_spec((1, D_FF)),
            _const_spec((D_FF, D_MODEL)), _const_spec((1, D_MODEL)),
        ],
        out_specs=pl.BlockSpec((tm, D_MODEL), lambda i: (i, 0)),
        out_shape=jax.ShapeDtypeStruct((T, D_MODEL), F32),
        scratch_shapes=[
            pltpu.VMEM((tm + 2 * HALO, D_MODEL), BF16),
            pltpu.VMEM((tm + 2 * HALO, D_FF), F32),
        ],
        compiler_params=pltpu.CompilerParams(
            dimension_semantics=("parallel",), vmem_limit_bytes=VMEM_LIMIT),
        name="ffn",
    )(x1, x1, x1, p["g_ffn_pre"], p["wup_g"], p["wup_v"], p["conv_w"], p["conv_b"], p["wdown"], p["g_ffn_post"])


def _rope_tables(S, q_gain, k_gain):
    rows = S // GRID_W
    r = jnp.repeat(jnp.arange(rows, dtype=F32), GRID_W)
    c = jnp.tile(jnp.arange(GRID_W, dtype=F32), rows)
    n_pairs_axis = HEAD_DIM // 4
    freq = ROPE_THETA ** (-jnp.arange(n_pairs_axis, dtype=F32) / n_pairs_axis)
    ang = jnp.concatenate([r[:, None] * freq, c[:, None] * freq], axis=-1)
    cos, sin = jnp.cos(ang), jnp.sin(ang)
    cc = jnp.concatenate([cos, cos], axis=-1)
    ss = jnp.concatenate([-sin, sin], axis=-1)

    def tables(gain, scale):
        g_half = jnp.concatenate([gain[0::2], gain[1::2]])
        g_swap = jnp.concatenate([gain[1::2], gain[0::2]])
        ta = cc * g_half * scale
        tb = ss * g_swap * scale
        return jnp.tile(ta, (1, LANES // HEAD_DIM)), jnp.tile(tb, (1, LANES // HEAD_DIM))

    tqa, tqb = tables(q_gain, LOG2E * HEAD_DIM ** -0.5)
    tka, tkb = tables(k_gain, 1.0)
    return tqa, tqb, tka, tkb


def _prepare(norm_mix_pre, w_in, b_if, q_norm, k_norm, mlstm_norm, w_attn_o, w_mlstm_o, w_out,
             norm_mix_post, norm_ffn_pre, w_up, conv_w, conv_b, w_down, norm_ffn_post, seq_lens):
    half = np.concatenate([np.arange(0, HEAD_DIM, 2), np.arange(1, HEAD_DIM, 2)])
    slots = [h for j in range(GQA_GROUP) for h in (j, j + GQA_GROUP)]
    q_cols = np.concatenate([h * HEAD_DIM + half for h in slots])
    k_cols = ATTN_WIDTH + np.concatenate([h * HEAD_DIM + half for h in range(N_KV_HEADS)])
    v_cols = ATTN_WIDTH + KV_WIDTH + np.arange(KV_WIDTH)
    o_rows = np.concatenate([h * HEAD_DIM + np.arange(HEAD_DIM) for h in slots])
    m0 = ATTN_WIDTH + 2 * KV_WIDTH
    g0 = m0 + 4 * M_WIDTH
    s0 = g0 + N_GATE_PRE

    p = {}
    p["wa"] = w_in[:, np.concatenate([q_cols, k_cols, v_cols])].astype(BF16)
    p["wm"] = w_in[:, m0:g0].astype(BF16)
    p["wg"] = jnp.pad(w_in[:, g0:s0], ((0, 0), (0, LANES - N_GATE_PRE))).astype(BF16)
    p["ws"] = w_in[:, s0:].astype(BF16)
    blk = np.arange(MXU_DIM) // HEAD_DIM
    p["bd"] = jnp.asarray((blk[:, None] == blk[None, :]).astype(np.float32) / HEAD_DIM, BF16)
    p["g_mix_pre"] = norm_mix_pre.reshape(1, D_MODEL)
    p["b_if_row"] = b_if.reshape(1, N_GATE_PRE)
    p["b_if_col"] = b_if.reshape(N_GATE_PRE, 1)
    p["g_mlstm"] = mlstm_norm.reshape(1, M_WIDTH)
    p["wao"] = w_attn_o[o_rows].astype(BF16)
    p["wmo"] = w_mlstm_o.astype(BF16)
    p["wout"] = w_out.astype(BF16)
    p["g_mix_post"] = norm_mix_post.reshape(1, D_MODEL)
    p["g_ffn_pre"] = norm_ffn_pre.reshape(1, D_MODEL)
    p["wup_g"] = w_up[:, :D_FF].astype(BF16)
    p["wup_v"] = w_up[:, D_FF:].astype(BF16)
    p["conv_w"] = jnp.pad(conv_w, ((0, 8 - CONV_W), (0, 0)))
    p["conv_b"] = conv_b.reshape(1, D_FF)
    p["wdown"] = w_down.astype(BF16)
    p["g_ffn_post"] = norm_ffn_post.reshape(1, D_MODEL)
    p["score_bound"] = (1.02 * LOG2E * HEAD_DIM ** 0.5) * jnp.max(jnp.abs(q_norm)) * jnp.max(jnp.abs(k_norm))
    p["tqa"], p["tqb"], p["tka"], p["tkb"] = {}, {}, {}, {}
    for S in seq_lens:
        p["tqa"][S], p["tqb"][S], p["tka"][S], p["tkb"][S] = _rope_tables(S, q_norm, k_norm)
    return p


def _trunk(x, p):
    B, S, _ = x.shape
    T = B * S
    x2 = x.reshape(T, D_MODEL)
    q, k, vt, m4, gate, gate_t, sg = _in_proj(x2, p, S)
    a, hf, hb = lax.cond(p["score_bound"] <= MAX_UNSTABILISED_SCORE,
                         lambda: _mixers(q, k, vt, m4, gate, gate_t, p, B, S, False),
                         lambda: _mixers(q, k, vt, m4, gate, gate_t, p, B, S, True))
    x1 = _merge(a, hf, hb, m4, sg, x2, p)
    y = _ffn(x1, p, S)
    return y.reshape(B, S, D_MODEL)


def kernel(x_prompt, x_sample, norm_mix_pre, w_in, b_if, q_norm, k_norm, mlstm_norm, w_attn_o, w_mlstm_o,
           w_out, norm_mix_post, norm_ffn_pre, w_up, conv_w, conv_b, w_down, norm_ffn_post):
    depth = w_in.shape[0]
    seq_lens = sorted({x_prompt.shape[1], x_sample.shape[1]})
    layers = [
        _prepare(norm_mix_pre[l], w_in[l], b_if[l], q_norm[l], k_norm[l], mlstm_norm[l], w_attn_o[l],
                 w_mlstm_o[l], w_out[l], norm_mix_post[l], norm_ffn_pre[l], w_up[l], conv_w[l], conv_b[l],
                 w_down[l], norm_ffn_post[l], seq_lens)
        for l in range(depth)
    ]

    def trunk(x):
        for p in layers:
            x = _trunk(x, p)
        return x

    return (trunk(x_prompt), trunk(x_sample))
```

```python
import functools

import numpy as np
import jax
import jax.numpy as jnp
from jax import lax
from jax.experimental import pallas as pl
from jax.experimental.pallas import tpu as pltpu

D_MODEL = 1024
GRID_W = 64
N_Q_HEADS = 8
N_KV_HEADS = 2
GQA_GROUP = N_Q_HEADS // N_KV_HEADS
HEAD_DIM = 64
ATTN_WIDTH = N_Q_HEADS * HEAD_DIM
KV_WIDTH = N_KV_HEADS * HEAD_DIM
ROPE_THETA = 10000.0
M_HEADS = 4
M_HEAD_DIM = 128
M_WIDTH = M_HEADS * M_HEAD_DIM
N_GATE_PRE = 4 * M_HEADS
D_FF = 2816
CONV_W = 3
EPS = 1e-6

LANES = 128
MXU_DIM = 256
VMEM_LIMIT = 56 * 1024 * 1024

TOKEN_TILE = 512
ATTN_Q_TILE = 512
ATTN_KV_CHUNK = 256
M_CHUNK = 128
FFN_TOKEN_TILE = 512
FFN_ROW_SPLIT = 2
HALO = 16
LOG2E = 1.4426950408889634
MAX_UNSTABILISED_SCORE = 40.0

BF16 = jnp.bfloat16
F32 = jnp.float32


def _sigmoid(x):
    return 0.5 * jnp.tanh(0.5 * x) + 0.5


def _const_spec(shape):
    n = len(shape)
    return pl.BlockSpec(shape, lambda *_: (0,) * n, pipeline_mode=pl.Buffered(1))


def _in_proj_kernel(x_ref, g_ref, wa_ref, wm_ref, wg_ref, ws_ref, bd_ref,
                    tqa_ref, tqb_ref, tka_ref, tkb_ref,
                    q_ref, k_ref, vt_ref, m_ref, gate_ref, gate_t_ref, sg_ref):
    x = x_ref[...]
    ms = jnp.mean(x * x, axis=-1, keepdims=True)
    h = (x * lax.rsqrt(ms + EPS) * g_ref[...]).astype(BF16)

    za = jnp.dot(h, wa_ref[...], preferred_element_type=F32)
    zm = jnp.dot(h, wm_ref[...], preferred_element_type=F32)
    zg = jnp.dot(h, wg_ref[...], preferred_element_type=F32)
    zs = jnp.dot(h, ws_ref[...], preferred_element_type=F32)
    bd = bd_ref[...]
    lane = lax.broadcasted_iota(jnp.int32, (x.shape[0], LANES), 1)
    first_half = (lane % HEAD_DIM) < (HEAD_DIM // 2)

    def head_norm_rope(z, msq, ta, tb):
        sw = jnp.where(first_half, pltpu.roll(z, LANES - HEAD_DIM // 2, 1), pltpu.roll(z, HEAD_DIM // 2, 1))
        return lax.rsqrt(msq + EPS) * (z * ta + sw * tb)

    tqa, tqb = tqa_ref[...], tqb_ref[...]
    for half in range(ATTN_WIDTH // MXU_DIM):
        zq = za[:, half * MXU_DIM:(half + 1) * MXU_DIM]
        msq = jnp.dot((zq * zq).astype(BF16), bd, preferred_element_type=F32)
        for j in range(MXU_DIM // LANES):
            c0 = half * MXU_DIM + j * LANES
            out = head_norm_rope(zq[:, j * LANES:(j + 1) * LANES], msq[:, j * LANES:(j + 1) * LANES], tqa, tqb)
            q_ref[:, c0:c0 + LANES] = out.astype(BF16)
    zk = za[:, ATTN_WIDTH:ATTN_WIDTH + KV_WIDTH]
    msk = jnp.dot((zk * zk).astype(BF16), bd[:LANES, :LANES], preferred_element_type=F32)
    k_ref[...] = head_norm_rope(zk, msk, tka_ref[...], tkb_ref[...]).astype(BF16)
    v_t = za[:, ATTN_WIDTH + KV_WIDTH:].T
    for kvh in range(N_KV_HEADS):
        for c in range(x.shape[0] // ATTN_KV_CHUNK):
            vt_ref[kvh, c] = v_t[kvh * HEAD_DIM:(kvh + 1) * HEAD_DIM,
                                 c * ATTN_KV_CHUNK:(c + 1) * ATTN_KV_CHUNK].astype(BF16)

    m_ref[:, 0:M_WIDTH] = zm[:, 0:M_WIDTH].astype(BF16)
    m_ref[:, M_WIDTH:2 * M_WIDTH] = (zm[:, M_WIDTH:2 * M_WIDTH] * (M_HEAD_DIM ** -0.5)).astype(BF16)
    m_ref[:, 2 * M_WIDTH:3 * M_WIDTH] = zm[:, 2 * M_WIDTH:3 * M_WIDTH].astype(BF16)
    m_ref[:, 3 * M_WIDTH:] = _sigmoid(zm[:, 3 * M_WIDTH:]).astype(BF16)

    gate_ref[...] = zg[:, :N_GATE_PRE]
    gate_t_ref[...] = zg.T[:N_GATE_PRE, :]

    sg_ref[...] = _sigmoid(zs).astype(BF16)


def _in_proj(x2, p, S):
    T = x2.shape[0]
    tm = TOKEN_TILE
    spt = S // tm
    row = lambda i: (i, 0)
    pos = lambda i: (i % spt, 0)
    tab = pl.BlockSpec((tm, LANES), pos)
    return pl.pallas_call(
        _in_proj_kernel,
        grid=(T // tm,),
        in_specs=[
            pl.BlockSpec((tm, D_MODEL), row),
            _const_spec((1, D_MODEL)),
            _const_spec(p["wa"].shape), _const_spec(p["wm"].shape),
            _const_spec(p["wg"].shape), _const_spec(p["ws"].shape),
            _const_spec((MXU_DIM, MXU_DIM)),
            tab, tab, tab, tab,
        ],
        out_specs=[
            pl.BlockSpec((tm, ATTN_WIDTH), row),
            pl.BlockSpec((tm, KV_WIDTH), row),
            pl.BlockSpec((None, N_KV_HEADS, tm // ATTN_KV_CHUNK, HEAD_DIM, ATTN_KV_CHUNK),
                         lambda i: (i // spt, 0, i % spt, 0, 0)),
            pl.BlockSpec((tm, 4 * M_WIDTH), row),
            pl.BlockSpec((tm, N_GATE_PRE), row),
            pl.BlockSpec((N_GATE_PRE, tm), lambda i: (0, i)),
            pl.BlockSpec((tm, 2 * D_MODEL), row),
        ],
        out_shape=[
            jax.ShapeDtypeStruct((T, ATTN_WIDTH), BF16),
            jax.ShapeDtypeStruct((T, KV_WIDTH), BF16),
            jax.ShapeDtypeStruct((T // S, N_KV_HEADS, S // ATTN_KV_CHUNK, HEAD_DIM, ATTN_KV_CHUNK), BF16),
            jax.ShapeDtypeStruct((T, 4 * M_WIDTH), BF16),
            jax.ShapeDtypeStruct((T, N_GATE_PRE), F32),
            jax.ShapeDtypeStruct((N_GATE_PRE, T), F32),
            jax.ShapeDtypeStruct((T, 2 * D_MODEL), BF16),
        ],
        compiler_params=pltpu.CompilerParams(
            dimension_semantics=("parallel",), vmem_limit_bytes=VMEM_LIMIT),
        name="in_proj",
    )(x2, p["g_mix_pre"], p["wa"], p["wm"], p["wg"], p["ws"], p["bd"],
      p["tqa"][S], p["tqb"][S], p["tka"][S], p["tkb"][S])


def _mlstm_stages(qf_ref, kf_ref, vf_ref, qb_ref, kb_ref, vb_ref, gcf_ref, gcb_ref, grf_ref, grb_ref, bc_ref, br_ref,
                  hf_ref, hb_ref, c_ref, m_ref, n_steps):
    L = M_CHUNK
    ri = lax.broadcasted_iota(jnp.int32, (L, L), 0)
    ci = lax.broadcasted_iota(jnp.int32, (L, L), 1)
    lower = ci <= ri
    upper = ci >= ri
    lower_b = lower.astype(BF16)
    upper_b = upper.astype(BF16)
    ones = jnp.ones((L, M_HEAD_DIM), BF16)
    fwd_cols = lax.broadcasted_iota(jnp.int32, (L, N_GATE_PRE), 1) < N_GATE_PRE // 2
    fwd_rows = lax.broadcasted_iota(jnp.int32, (N_GATE_PRE, L), 0) < N_GATE_PRE // 2

    def split_dot(tri, x, tri_first):
        hi = x.astype(BF16)
        lo = (x - hi.astype(F32)).astype(BF16)
        if tri_first:
            return jnp.dot(tri, hi, preferred_element_type=F32) + jnp.dot(tri, lo, preferred_element_type=F32)
        return jnp.dot(hi, tri, preferred_element_type=F32) + jnp.dot(lo, tri, preferred_element_type=F32)

    dirs = ((qf_ref, kf_ref, vf_ref, gcf_ref, grf_ref, hf_ref, lower),
            (qb_ref, kb_ref, vb_ref, gcb_ref, grb_ref, hb_ref, upper))

    for t in range(n_steps):
        rows = (slice(t * L, (t + 1) * L), slice((n_steps - 1 - t) * L, (n_steps - t) * L))
        chains = []
        for d, (q_ref, k_ref, v_ref, _, _, h_ref, mask) in enumerate(dirs):
            for hd in range(M_HEADS):
                cols = slice(hd * M_HEAD_DIM, (hd + 1) * M_HEAD_DIM)
                st = d * M_HEADS + hd
                ch = dict(st=st, d=d, hd=hd, mask=mask, h_ref=h_ref, cols=cols,
                          q=q_ref[rows[d], cols], k=k_ref[rows[d], cols],
                          v_ext=jnp.concatenate([v_ref[rows[d], cols], ones], axis=1),
                          m_old=m_ref[st:st + 1, 0:1],
                          c_old=c_ref[st])
                ch["qk"] = lax.dot_general(ch["q"], ch["k"], (((1,), (1,)), ((), ())), preferred_element_type=F32)
                chains.append(ch)
        yield

        gcol = jnp.where(fwd_cols, gcf_ref[rows[0], :], gcb_ref[rows[1], :]) + bc_ref[...]
        grow = jnp.where(fwd_rows, grf_ref[:, rows[0]], grb_ref[:, rows[1]]) + br_ref[...]
        lf_col = jax.nn.log_sigmoid(gcol)
        lf_row = jax.nn.log_sigmoid(grow)
        pre_col = split_dot(lower_b, lf_col, True)
        pre_row = split_dot(upper_b, lf_row, False)
        tot_col = pre_col[L - 1:L, :]
        tot_row = pre_row[:, L - 1:L]
        for ch in chains:
            gi = 2 * ch["d"] * M_HEADS + ch["hd"]
            gf = gi + M_HEADS
            b_row = pre_row[gf:gf + 1, :]
            b_col = pre_col[:, gf:gf + 1]
            if ch["d"] == 1:
                b_row = tot_row[gf:gf + 1, :] - b_row + lf_row[gf:gf + 1, :]
                b_col = tot_col[:, gf:gf + 1] - b_col + lf_col[:, gf:gf + 1]
            ch["b_col"] = b_col
            ch["r_row"] = grow[gi:gi + 1, :] - b_row
            ch["b_tot"] = tot_row[gf:gf + 1, :]
        yield

        for ch in chains:
            ws_row = ch["b_tot"] + ch["r_row"]
            m_new = jnp.maximum(ch["b_tot"] + ch["m_old"], jnp.max(ws_row, axis=1, keepdims=True))
            a = jnp.exp(ch["b_tot"] + ch["m_old"] - m_new)
            w_row = jnp.exp(ws_row - m_new)
            kw = (ch["k"].astype(F32).T * w_row).astype(BF16)
            st = ch["st"]
            c_ref[st] = a * ch["c_old"] + jnp.dot(kw, ch["v_ext"], preferred_element_type=F32)
            m_ref[st:st + 1, :] = jnp.broadcast_to(m_new, (1, LANES))
        yield

        for ch in chains:
            r = jnp.where(ch["mask"], ch["r_row"], -jnp.inf)
            m_col = jnp.maximum(jnp.max(r, axis=1, keepdims=True), ch["m_old"])
            dec = jnp.exp(ch["m_old"] - m_col)
            ch["lhs"] = jnp.concatenate([(jnp.exp(r - m_col) * ch["qk"]).astype(BF16),
                                         (ch["q"].astype(F32) * dec).astype(BF16)], axis=1)
            ch["floor"] = jnp.exp(-(ch["b_col"] + m_col))
        yield

        for ch in chains:
            rhs = jnp.concatenate([ch["v_ext"], ch["c_old"].astype(BF16)], axis=0)
            ext = jnp.dot(ch["lhs"], rhs, preferred_element_type=F32)
            num, den = ext[:, :M_HEAD_DIM], ext[:, M_HEAD_DIM:]
            h = num / jnp.maximum(jnp.abs(den), ch["floor"])
            ch["h_ref"][rows[ch["d"]], ch["cols"]] = h.astype(ch["h_ref"].dtype)
        yield


MLSTM_STAGES_PER_CHUNK = 5


def _mixer_kernel(q_ref, k_ref, vt_ref,
                  qf_ref, kf_ref, vf_ref, qb_ref, kb_ref, vb_ref, gcf_ref, gcb_ref, grf_ref, grb_ref, bc_ref, br_ref,
                  o_ref, hf_ref, hb_ref,
                  w_ref, p0_ref, p1_ref, acc_ref, l_ref, c_ref, m_ref, *, n_chunks, tq, kc, stabilise):
    row = lax.broadcasted_iota(jnp.int32, (LANES, tq), 0)
    low = row < HEAD_DIM
    for j in range(GQA_GROUP):
        qt = q_ref[:, j * LANES:(j + 1) * LANES].astype(F32).T
        w_ref[:, (2 * j) * tq:(2 * j + 1) * tq] = jnp.where(low, qt, 0.0).astype(BF16)
        w_ref[:, (2 * j + 1) * tq:(2 * j + 2) * tq] = jnp.where(low, 0.0, qt).astype(BF16)
    acc_ref[...] = jnp.zeros(acc_ref.shape, F32)
    l_ref[...] = jnp.zeros(l_ref.shape, F32)

    @pl.when(pl.program_id(1) == 0)
    def _():
        c_ref[...] = jnp.zeros(c_ref.shape, F32)
        m_ref[...] = jnp.zeros(m_ref.shape, F32)

    def step(c, ms, src=None, dst=None):
        new_ms = []
        if dst is not None:
            start = (c + 1) * kc
            if not isinstance(start, int):
                start = pl.multiple_of(start, kc)
            kch = k_ref[pl.ds(start, kc), :]
        for slot in range(N_Q_HEADS):
            cols = slice(slot * tq, (slot + 1) * tq)
            if dst is not None:
                s = jnp.dot(kch, w_ref[:, cols], preferred_element_type=F32)
                if stabilise:
                    m = ms[slot][0]
                    mn = jnp.maximum(m, jnp.max(s, axis=0, keepdims=True))
                    new_ms.append((mn, jnp.exp2(m - mn)))
                    s = s - mn
                    l_ref[slot] = l_ref[slot] * new_ms[-1][1]
                pr = jnp.exp2(s)
                dst[:, cols] = pr.astype(BF16)
                l_ref[slot] += jnp.sum(pr.reshape(kc // 8, 8, tq), axis=0)
            if src is not None:
                pv = jnp.dot(vt_ref[slot % N_KV_HEADS, c], src[:, cols], preferred_element_type=F32)
                if stabilise:
                    acc_ref[slot] = acc_ref[slot] * ms[slot][1] + pv
                else:
                    acc_ref[slot] += pv
        return ms if dst is None else tuple(new_ms)

    mlstm_steps = tq // M_CHUNK
    mlstm = _mlstm_stages(qf_ref, kf_ref, vf_ref, qb_ref, kb_ref, vb_ref, gcf_ref, gcb_ref, grf_ref, grb_ref,
                          bc_ref, br_ref, hf_ref, hb_ref, c_ref, m_ref, mlstm_steps)
    if stabilise:
        neg_inf = jnp.full((1, tq), -jnp.inf, F32)
        state = step(-1, tuple((neg_inf, neg_inf) for _ in range(N_Q_HEADS)), dst=p0_ref)
        state = lax.fori_loop(
            0, n_chunks // 2 - 1,
            lambda j, ms: step(2 * j + 1, step(2 * j, ms, src=p0_ref, dst=p1_ref), src=p1_ref, dst=p0_ref), state)
        state = step(n_chunks - 2, state, src=p0_ref, dst=p1_ref)
        step(n_chunks - 1, state, src=p1_ref)
        for _ in mlstm:
            pass
    else:
        bufs = (p0_ref, p1_ref)
        steps = ([(-1, None, p0_ref)]
                 + [(c, bufs[c % 2], bufs[(c + 1) % 2]) for c in range(n_chunks - 1)]
                 + [(n_chunks - 1, bufs[(n_chunks - 1) % 2], None)])
        n_stages = mlstm_steps * MLSTM_STAGES_PER_CHUNK
        done = 0
        for idx, (c, src, dst) in enumerate(steps):
            step(c, (), src=src, dst=dst)
            while done < ((idx + 1) * n_stages) // len(steps):
                next(mlstm)
                done += 1

    outs = [acc_ref[slot] / jnp.sum(l_ref[slot], axis=0, keepdims=True) for slot in range(N_Q_HEADS)]
    o_ref[...] = jnp.concatenate(outs, axis=0).T.astype(BF16)


def _mixers(q, k, vt, m4, gate, gate_t, p, B, S, stabilise):
    T = q.shape[0]
    tq, kc = ATTN_Q_TILE, ATTN_KV_CHUNK
    nq, n_chunks = S // tq, S // kc
    kern = functools.partial(_mixer_kernel, n_chunks=n_chunks, tq=tq, kc=kc, stabilise=stabilise)
    fwd = lambda col: (lambda b, i: (b * nq + i, col))
    bwd = lambda col: (lambda b, i: (b * nq + nq - 1 - i, col))
    blk = lambda f: pl.BlockSpec((tq, M_WIDTH), f)
    return pl.pallas_call(
        kern,
        grid=(B, nq),
        in_specs=[
            pl.BlockSpec((tq, ATTN_WIDTH), fwd(0)),
            pl.BlockSpec((S, KV_WIDTH), lambda b, i: (b, 0)),
            pl.BlockSpec((None, N_KV_HEADS, n_chunks, HEAD_DIM, kc), lambda b, i: (b, 0, 0, 0, 0)),
            blk(fwd(0)), blk(fwd(1)), blk(fwd(2)),
            blk(bwd(0)), blk(bwd(1)), blk(bwd(2)),
            pl.BlockSpec((tq, N_GATE_PRE), fwd(0)),
            pl.BlockSpec((tq, N_GATE_PRE), bwd(0)),
            pl.BlockSpec((N_GATE_PRE, tq), lambda b, i: (0, b * nq + i)),
            pl.BlockSpec((N_GATE_PRE, tq), lambda b, i: (0, b * nq + nq - 1 - i)),
            _const_spec((1, N_GATE_PRE)), _const_spec((N_GATE_PRE, 1)),
        ],
        out_specs=[pl.BlockSpec((tq, ATTN_WIDTH), fwd(0)), blk(fwd(0)), blk(bwd(0))],
        out_shape=[jax.ShapeDtypeStruct((T, ATTN_WIDTH), BF16)] + [jax.ShapeDtypeStruct((T, M_WIDTH), BF16)] * 2,
        scratch_shapes=[
            pltpu.VMEM((LANES, N_Q_HEADS * tq), BF16),
            pltpu.VMEM((kc, N_Q_HEADS * tq), BF16),
            pltpu.VMEM((kc, N_Q_HEADS * tq), BF16),
            pltpu.VMEM((N_Q_HEADS, HEAD_DIM, tq), F32),
            pltpu.VMEM((N_Q_HEADS, 8, tq), F32),
            pltpu.VMEM((2 * M_HEADS, M_HEAD_DIM, 2 * M_HEAD_DIM), F32),
            pltpu.VMEM((2 * M_HEADS, LANES), F32),
        ],
        compiler_params=pltpu.CompilerParams(
            dimension_semantics=("parallel", "arbitrary"), vmem_limit_bytes=VMEM_LIMIT),
        name="mixers",
    )(q, k, vt, m4, m4, m4, m4, m4, m4, gate, gate, gate_t, gate_t, p["b_if_row"], p["b_if_col"])


def _merge_kernel(a_ref, hf_ref, hb_ref, so_ref, sg_ref, x_ref, gm_ref, wao_ref, wmo_ref, wout_ref, gp_ref, o_ref):
    hs = hf_ref[...].astype(F32) + hb_ref[...].astype(F32)
    parts = []
    for hd in range(M_HEADS):
        blk = hs[:, hd * M_HEAD_DIM:(hd + 1) * M_HEAD_DIM]
        ms = jnp.mean(blk * blk, axis=-1, keepdims=True)
        parts.append(blk * lax.rsqrt(ms + EPS))
    hn = jnp.concatenate(parts, axis=1) * gm_ref[...]
    hm = (hn * so_ref[...].astype(F32)).astype(BF16)
    a_out = jnp.dot(a_ref[...], wao_ref[...], preferred_element_type=F32)
    m_out = jnp.dot(hm, wmo_ref[...], preferred_element_type=F32)
    sg = sg_ref[...]
    merged = sg[:, :D_MODEL].astype(F32) * a_out + sg[:, D_MODEL:].astype(F32) * m_out
    y = jnp.dot(merged.astype(BF16), wout_ref[...], preferred_element_type=F32)
    ms = jnp.mean(y * y, axis=-1, keepdims=True)
    o_ref[...] = x_ref[...] + y * lax.rsqrt(ms + EPS) * gp_ref[...]


def _merge(a, hf, hb, m4, sg, x2, p):
    T = x2.shape[0]
    tm = TOKEN_TILE
    row = lambda i: (i, 0)
    return pl.pallas_call(
        _merge_kernel,
        grid=(T // tm,),
        in_specs=[
            pl.BlockSpec((tm, ATTN_WIDTH), row),
            pl.BlockSpec((tm, M_WIDTH), row),
            pl.BlockSpec((tm, M_WIDTH), row),
            pl.BlockSpec((tm, M_WIDTH), lambda i: (i, 3)),
            pl.BlockSpec((tm, 2 * D_MODEL), row),
            pl.BlockSpec((tm, D_MODEL), row),
            _const_spec((1, M_WIDTH)),
            _const_spec((ATTN_WIDTH, D_MODEL)), _const_spec((M_WIDTH, D_MODEL)),
            _const_spec((D_MODEL, D_MODEL)), _const_spec((1, D_MODEL)),
        ],
        out_specs=pl.BlockSpec((tm, D_MODEL), row),
        out_shape=jax.ShapeDtypeStruct((T, D_MODEL), F32),
        compiler_params=pltpu.CompilerParams(
            dimension_semantics=("parallel",), vmem_limit_bytes=VMEM_LIMIT),
        name="merge",
    )(a, hf, hb, m4, sg, x2, p["g_mlstm"], p["wao"], p["wmo"], p["wout"], p["g_mix_post"])


def _ffn_kernel(xp_ref, x_ref, xn_ref, g_ref, wg_ref, wv_ref, cw_ref, cb_ref, wd_ref, gp_ref, o_ref,
                h_ref, ge_ref, *, tiles_per_seq):
    tm = x_ref.shape[0]
    g = g_ref[...]

    def norm(x):
        ms = jnp.mean(x * x, axis=-1, keepdims=True)
        return (x * lax.rsqrt(ms + EPS) * g).astype(BF16)

    h_ref[0:HALO, :] = norm(xp_ref[...])
    h_ref[HALO:HALO + tm, :] = norm(x_ref[...])
    h_ref[HALO + tm:, :] = norm(xn_ref[...])
    pos = pl.program_id(0) % tiles_per_seq
    keep_prev = (pos != 0).astype(F32)
    keep_next = (pos != tiles_per_seq - 1).astype(F32)

    ge_ref[...] = jnp.dot(h_ref[...], wg_ref[...], preferred_element_type=F32)
    ge_ref[HALO - 1:HALO, :] = ge_ref[HALO - 1:HALO, :] * keep_prev
    ge_ref[HALO + tm:HALO + tm + 1, :] = ge_ref[HALO + tm:HALO + tm + 1, :] * keep_next
    half = tm // FFN_ROW_SPLIT
    vals = [jnp.dot(h_ref[HALO + r * half:HALO + (r + 1) * half, :], wv_ref[...], preferred_element_type=F32)
            for r in range(FFN_ROW_SPLIT)]
    cw = cw_ref[...]
    for r in range(FFN_ROW_SPLIT):
        r0 = HALO + r * half
        conv = (ge_ref[r0 - 1:r0 - 1 + half, :] * cw[0:1]
                + ge_ref[r0:r0 + half, :] * cw[1:2]
                + ge_ref[r0 + 1:r0 + 1 + half, :] * cw[2:3]
                + cb_ref[...])
        act = (jax.nn.gelu(conv, approximate=True) * vals[r]).astype(BF16)
        y = jnp.dot(act, wd_ref[...], preferred_element_type=F32)
        ms = jnp.mean(y * y, axis=-1, keepdims=True)
        rows = slice(r * half, (r + 1) * half)
        o_ref[rows, :] = x_ref[rows, :] + y * lax.rsqrt(ms + EPS) * gp_ref[...]


def _ffn(x1, p, S):
    T = x1.shape[0]
    tm = FFN_TOKEN_TILE
    hpt = tm // HALO
    n_halo = T // HALO
    kern = functools.partial(_ffn_kernel, tiles_per_seq=S // tm)
    return pl.pallas_call(
        kern,
        grid=(T // tm,),
        in_specs=[
            pl.BlockSpec((HALO, D_MODEL), lambda i: (jnp.maximum(i * hpt - 1, 0), 0)),
            pl.BlockSpec((tm, D_MODEL), lambda i: (i, 0)),
            pl.BlockSpec((HALO, D_MODEL), lambda i: (jnp.minimum((i + 1) * hpt, n_halo - 1), 0)),
            _const_spec((1, D_MODEL)),
            _const_spec((D_MODEL, D_FF)), _const_spec((D_MODEL, D_FF)),
            _const_spec((8, D_FF)), _const_spec((1, D_FF)),
            _const_spec((D_FF, D_MODEL)), _const_spec((1, D_MODEL)),
        ],
        out_specs=pl.BlockSpec((tm, D_MODEL), lambda i: (i, 0)),
        out_shape=jax.ShapeDtypeStruct((T, D_MODEL), F32),
        scratch_shapes=[
            pltpu.VMEM((tm + 2 * HALO, D_MODEL), BF16),
            pltpu.VMEM((tm + 2 * HALO, D_FF), F32),
        ],
        compiler_params=pltpu.CompilerParams(
            dimension_semantics=("parallel",), vmem_limit_bytes=VMEM_LIMIT),
        name="ffn",
    )(x1, x1, x1, p["g_ffn_pre"], p["wup_g"], p["wup_v"], p["conv_w"], p["conv_b"], p["wdown"], p["g_ffn_post"])


def _rope_tables(S, q_gain, k_gain):
    rows = S // GRID_W
    r = jnp.repeat(jnp.arange(rows, dtype=F32), GRID_W)
    c = jnp.tile(jnp.arange(GRID_W, dtype=F32), rows)
    n_pairs_axis = HEAD_DIM // 4
    freq = ROPE_THETA ** (-jnp.arange(n_pairs_axis, dtype=F32) / n_pairs_axis)
    ang = jnp.concatenate([r[:, None] * freq, c[:, None] * freq], axis=-1)
    cos, sin = jnp.cos(ang), jnp.sin(ang)
    cc = jnp.concatenate([cos, cos], axis=-1)
    ss = jnp.concatenate([-sin, sin], axis=-1)

    def tables(gain, scale):
        g_half = jnp.concatenate([gain[0::2], gain[1::2]])
        g_swap = jnp.concatenate([gain[1::2], gain[0::2]])
        ta = cc * g_half * scale
        tb = ss * g_swap * scale
        return jnp.tile(ta, (1, LANES // HEAD_DIM)), jnp.tile(tb, (1, LANES // HEAD_DIM))

    tqa, tqb = tables(q_gain, LOG2E * HEAD_DIM ** -0.5)
    tka, tkb = tables(k_gain, 1.0)
    return tqa, tqb, tka, tkb


def _prepare(norm_mix_pre, w_in, b_if, q_norm, k_norm, mlstm_norm, w_attn_o, w_mlstm_o, w_out,
             norm_mix_post, norm_ffn_pre, w_up, conv_w, conv_b, w_down, norm_ffn_post, seq_lens):
    half = np.concatenate([np.arange(0, HEAD_DIM, 2), np.arange(1, HEAD_DIM, 2)])
    slots = [h for j in range(GQA_GROUP) for h in (j, j + GQA_GROUP)]
    q_cols = np.concatenate([h * HEAD_DIM + half for h in slots])
    k_cols = ATTN_WIDTH + np.concatenate([h * HEAD_DIM + half for h in range(N_KV_HEADS)])
    v_cols = ATTN_WIDTH + KV_WIDTH + np.arange(KV_WIDTH)
    o_rows = np.concatenate([h * HEAD_DIM + np.arange(HEAD_DIM) for h in slots])
    m0 = ATTN_WIDTH + 2 * KV_WIDTH
    g0 = m0 + 4 * M_WIDTH
    s0 = g0 + N_GATE_PRE

    p = {}
    p["wa"] = w_in[:, np.concatenate([q_cols, k_cols, v_cols])].astype(BF16)
    p["wm"] = w_in[:, m0:g0].astype(BF16)
    p["wg"] = jnp.pad(w_in[:, g0:s0], ((0, 0), (0, LANES - N_GATE_PRE))).astype(BF16)
    p["ws"] = w_in[:, s0:].astype(BF16)
    blk = np.arange(MXU_DIM) // HEAD_DIM
    p["bd"] = jnp.asarray((blk[:, None] == blk[None, :]).astype(np.float32) / HEAD_DIM, BF16)
    p["g_mix_pre"] = norm_mix_pre.reshape(1, D_MODEL)
    p["b_if_row"] = b_if.reshape(1, N_GATE_PRE)
    p["b_if_col"] = b_if.reshape(N_GATE_PRE, 1)
    p["g_mlstm"] = mlstm_norm.reshape(1, M_WIDTH)
    p["wao"] = w_attn_o[o_rows].astype(BF16)
    p["wmo"] = w_mlstm_o.astype(BF16)
    p["wout"] = w_out.astype(BF16)
    p["g_mix_post"] = norm_mix_post.reshape(1, D_MODEL)
    p["g_ffn_pre"] = norm_ffn_pre.reshape(1, D_MODEL)
    p["wup_g"] = w_up[:, :D_FF].astype(BF16)
    p["wup_v"] = w_up[:, D_FF:].astype(BF16)
    p["conv_w"] = jnp.pad(conv_w, ((0, 8 - CONV_W), (0, 0)))
    p["conv_b"] = conv_b.reshape(1, D_FF)
    p["wdown"] = w_down.astype(BF16)
    p["g_ffn_post"] = norm_ffn_post.reshape(1, D_MODEL)
    p["score_bound"] = (1.02 * LOG2E * HEAD_DIM ** 0.5) * jnp.max(jnp.abs(q_norm)) * jnp.max(jnp.abs(k_norm))
    p["tqa"], p["tqb"], p["tka"], p["tkb"] = {}, {}, {}, {}
    for S in seq_lens:
        p["tqa"][S], p["tqb"][S], p["tka"][S], p["tkb"][S] = _rope_tables(S, q_norm, k_norm)
    return p


def _trunk(x, p):
    B, S, _ = x.shape
    T = B * S
    x2 = x.reshape(T, D_MODEL)
    q, k, vt, m4, gate, gate_t, sg = _in_proj(x2, p, S)
    a, hf, hb = lax.cond(p["score_bound"] <= MAX_UNSTABILISED_SCORE,
                         lambda: _mixers(q, k, vt, m4, gate, gate_t, p, B, S, False),
                         lambda: _mixers(q, k, vt, m4, gate, gate_t, p, B, S, True))
    x1 = _merge(a, hf, hb, m4, sg, x2, p)
    y = _ffn(x1, p, S)
    return y.reshape(B, S, D_MODEL)


def kernel(x_prompt, x_sample, norm_mix_pre, w_in, b_if, q_norm, k_norm, mlstm_norm, w_attn_o, w_mlstm_o,
           w_out, norm_mix_post, norm_ffn_pre, w_up, conv_w, conv_b, w_down, norm_ffn_post):
    depth = w_in.shape[0]
    seq_lens = sorted({x_prompt.shape[1], x_sample.shape[1]})
    layers = [
        _prepare(norm_mix_pre[l], w_in[l], b_if[l], q_norm[l], k_norm[l], mlstm_norm[l], w_attn_o[l],
                 w_mlstm_o[l], w_out[l], norm_mix_post[l], norm_ffn_pre[l], w_up[l], conv_w[l], conv_b[l],
                 w_down[l], norm_ffn_post[l], seq_lens)
        for l in range(depth)
    ]

    def trunk(x):
        for p in layers:
            x = _trunk(x, p)
        return x

    return (trunk(x_prompt), trunk(x_sample))
```

```python
import functools

import numpy as np
import jax
import jax.numpy as jnp
from jax import lax
from jax.experimental import pallas as pl
from jax.experimental.pallas import tpu as pltpu

D_MODEL = 1024
GRID_W = 64
N_Q_HEADS = 8
N_KV_HEADS = 2
GQA_GROUP = N_Q_HEADS // N_KV_HEADS
HEAD_DIM = 64
ATTN_WIDTH = N_Q_HEADS * HEAD_DIM
KV_WIDTH = N_KV_HEADS * HEAD_DIM
ROPE_THETA = 10000.0
M_HEADS = 4
M_HEAD_DIM = 128
M_WIDTH = M_HEADS * M_HEAD_DIM
N_GATE_PRE = 4 * M_HEADS
D_FF = 2816
CONV_W = 3
EPS = 1e-6

LANES = 128
MXU_DIM = 256
VMEM_LIMIT = 56 * 1024 * 1024

TOKEN_TILE = 512
ATTN_Q_TILE = 512
ATTN_KV_CHUNK = 256
M_CHUNK = 128
FFN_TOKEN_TILE = 512
FFN_ROW_SPLIT = 2
HALO = 16
LOG2E = 1.4426950408889634
MAX_UNSTABILISED_SCORE = 40.0

BF16 = jnp.bfloat16
F32 = jnp.float32


def _sigmoid(x):
    return 0.5 * jnp.tanh(0.5 * x) + 0.5


def _const_spec(shape):
    n = len(shape)
    return pl.BlockSpec(shape, lambda *_: (0,) * n, pipeline_mode=pl.Buffered(1))


def _in_proj_kernel(x_ref, g_ref, wa_ref, wm_ref, wg_ref, ws_ref, bd_ref,
                    tqa_ref, tqb_ref, tka_ref, tkb_ref,
                    q_ref, k_ref, vt_ref, m_ref, gate_ref, gate_t_ref, sg_ref):
    x = x_ref[...]
    ms = jnp.mean(x * x, axis=-1, keepdims=True)
    h = (x * lax.rsqrt(ms + EPS) * g_ref[...]).astype(BF16)

    za = jnp.dot(h, wa_ref[...], preferred_element_type=F32)
    zm = jnp.dot(h, wm_ref[...], preferred_element_type=F32)
    zg = jnp.dot(h, wg_ref[...], preferred_element_type=F32)
    zs = jnp.dot(h, ws_ref[...], preferred_element_type=F32)
    bd = bd_ref[...]
    lane = lax.broadcasted_iota(jnp.int32, (x.shape[0], LANES), 1)
    first_half = (lane % HEAD_DIM) < (HEAD_DIM // 2)

    def head_norm_rope(z, msq, ta, tb):
        sw = jnp.where(first_half, pltpu.roll(z, LANES - HEAD_DIM // 2, 1), pltpu.roll(z, HEAD_DIM // 2, 1))
        return lax.rsqrt(msq + EPS) * (z * ta + sw * tb)

    tqa, tqb = tqa_ref[...], tqb_ref[...]
    for half in range(ATTN_WIDTH // MXU_DIM):
        zq = za[:, half * MXU_DIM:(half + 1) * MXU_DIM]
        msq = jnp.dot((zq * zq).astype(BF16), bd, preferred_element_type=F32)
        for j in range(MXU_DIM // LANES):
            c0 = half * MXU_DIM + j * LANES
            out = head_norm_rope(zq[:, j * LANES:(j + 1) * LANES], msq[:, j * LANES:(j + 1) * LANES], tqa, tqb)
            q_ref[:, c0:c0 + LANES] = out.astype(BF16)
    zk = za[:, ATTN_WIDTH:ATTN_WIDTH + KV_WIDTH]
    msk = jnp.dot((zk * zk).astype(BF16), bd[:LANES, :LANES], preferred_element_type=F32)
    k_ref[...] = head_norm_rope(zk, msk, tka_ref[...], tkb_ref[...]).astype(BF16)
    v_t = za[:, ATTN_WIDTH + KV_WIDTH:].T
    for kvh in range(N_KV_HEADS):
        for c in range(x.shape[0] // ATTN_KV_CHUNK):
            vt_ref[kvh, c] = v_t[kvh * HEAD_DIM:(kvh + 1) * HEAD_DIM,
                                 c * ATTN_KV_CHUNK:(c + 1) * ATTN_KV_CHUNK].astype(BF16)

    m_ref[:, 0:M_WIDTH] = zm[:, 0:M_WIDTH].astype(BF16)
    m_ref[:, M_WIDTH:2 * M_WIDTH] = (zm[:, M_WIDTH:2 * M_WIDTH] * (M_HEAD_DIM ** -0.5)).astype(BF16)
    m_ref[:, 2 * M_WIDTH:3 * M_WIDTH] = zm[:, 2 * M_WIDTH:3 * M_WIDTH].astype(BF16)
    m_ref[:, 3 * M_WIDTH:] = _sigmoid(zm[:, 3 * M_WIDTH:]).astype(BF16)

    gate_ref[...] = zg[:, :N_GATE_PRE]
    gate_t_ref[...] = zg.T[:N_GATE_PRE, :]

    sg_ref[...] = _sigmoid(zs).astype(BF16)


def _in_proj(x2, p, S):
    T = x2.shape[0]
    tm = TOKEN_TILE
    spt = S // tm
    row = lambda i: (i, 0)
    pos = lambda i: (i % spt, 0)
    tab = pl.BlockSpec((tm, LANES), pos)
    return pl.pallas_call(
        _in_proj_kernel,
        grid=(T // tm,),
        in_specs=[
            pl.BlockSpec((tm, D_MODEL), row),
            _const_spec((1, D_MODEL)),
            _const_spec(p["wa"].shape), _const_spec(p["wm"].shape),
            _const_spec(p["wg"].shape), _const_spec(p["ws"].shape),
            _const_spec((MXU_DIM, MXU_DIM)),
            tab, tab, tab, tab,
        ],
        out_specs=[
            pl.BlockSpec((tm, ATTN_WIDTH), row),
            pl.BlockSpec((tm, KV_WIDTH), row),
            pl.BlockSpec((None, N_KV_HEADS, tm // ATTN_KV_CHUNK, HEAD_DIM, ATTN_KV_CHUNK),
                         lambda i: (i // spt, 0, i % spt, 0, 0)),
            pl.BlockSpec((tm, 4 * M_WIDTH), row),
            pl.BlockSpec((tm, N_GATE_PRE), row),
            pl.BlockSpec((N_GATE_PRE, tm), lambda i: (0, i)),
            pl.BlockSpec((tm, 2 * D_MODEL), row),
        ],
        out_shape=[
            jax.ShapeDtypeStruct((T, ATTN_WIDTH), BF16),
            jax.ShapeDtypeStruct((T, KV_WIDTH), BF16),
            jax.ShapeDtypeStruct((T // S, N_KV_HEADS, S // ATTN_KV_CHUNK, HEAD_DIM, ATTN_KV_CHUNK), BF16),
            jax.ShapeDtypeStruct((T, 4 * M_WIDTH), BF16),
            jax.ShapeDtypeStruct((T, N_GATE_PRE), F32),
            jax.ShapeDtypeStruct((N_GATE_PRE, T), F32),
            jax.ShapeDtypeStruct((T, 2 * D_MODEL), BF16),
        ],
        compiler_params=pltpu.CompilerParams(
            dimension_semantics=("parallel",), vmem_limit_bytes=VMEM_LIMIT),
        name="in_proj",
    )(x2, p["g_mix_pre"], p["wa"], p["wm"], p["wg"], p["ws"], p["bd"],
      p["tqa"][S], p["tqb"][S], p["tka"][S], p["tkb"][S])


def _mlstm_stages(qf_ref, kf_ref, vf_ref, qb_ref, kb_ref, vb_ref, gcf_ref, gcb_ref, grf_ref, grb_ref, bc_ref, br_ref,
                  hf_ref, hb_ref, c_ref, m_ref, n_steps):
    L = M_CHUNK
    ri = lax.broadcasted_iota(jnp.int32, (L, L), 0)
    ci = lax.broadcasted_iota(jnp.int32, (L, L), 1)
    lower = ci <= ri
    upper = ci >= ri
    lower_b = lower.astype(BF16)
    upper_b = upper.astype(BF16)
    ones = jnp.ones((L, M_HEAD_DIM), BF16)
    fwd_cols = lax.broadcasted_iota(jnp.int32, (L, N_GATE_PRE), 1) < N_GATE_PRE // 2
    fwd_rows = lax.broadcasted_iota(jnp.int32, (N_GATE_PRE, L), 0) < N_GATE_PRE // 2

    def split_dot(tri, x, tri_first):
        hi = x.astype(BF16)
        lo = (x - hi.astype(F32)).astype(BF16)
        if tri_first:
            return jnp.dot(tri, hi, preferred_element_type=F32) + jnp.dot(tri, lo, preferred_element_type=F32)
        return jnp.dot(hi, tri, preferred_element_type=F32) + jnp.dot(lo, tri, preferred_element_type=F32)

    dirs = ((qf_ref, kf_ref, vf_ref, gcf_ref, grf_ref, hf_ref, lower),
            (qb_ref, kb_ref, vb_ref, gcb_ref, grb_ref, hb_ref, upper))

    for t in range(n_steps):
        rows = (slice(t * L, (t + 1) * L), slice((n_steps - 1 - t) * L, (n_steps - t) * L))
        chains = []
        for d, (q_ref, k_ref, v_ref, _, _, h_ref, mask) in enumerate(dirs):
            for hd in range(M_HEADS):
                cols = slice(hd * M_HEAD_DIM, (hd + 1) * M_HEAD_DIM)
                st = d * M_HEADS + hd
                ch = dict(st=st, d=d, hd=hd, mask=mask, h_ref=h_ref, cols=cols,
                          q=q_ref[rows[d], cols], k=k_ref[rows[d], cols],
                          v_ext=jnp.concatenate([v_ref[rows[d], cols], ones], axis=1),
                          m_old=m_ref[st:st + 1, 0:1],
                          c_old=c_ref[st])
                ch["qk"] = lax.dot_general(ch["q"], ch["k"], (((1,), (1,)), ((), ())), preferred_element_type=F32)
                chains.append(ch)
        yield

        gcol = jnp.where(fwd_cols, gcf_ref[rows[0], :], gcb_ref[rows[1], :]) + bc_ref[...]
        grow = jnp.where(fwd_rows, grf_ref[:, rows[0]], grb_ref[:, rows[1]]) + br_ref[...]
        lf_col = jax.nn.log_sigmoid(gcol)
        lf_row = jax.nn.log_sigmoid(grow)
        pre_col = split_dot(lower_b, lf_col, True)
        pre_row = split_dot(upper_b, lf_row, False)
        tot_col = pre_col[L - 1:L, :]
        tot_row = pre_row[:, L - 1:L]
        for ch in chains:
            gi = 2 * ch["d"] * M_HEADS + ch["hd"]
            gf = gi + M_HEADS
            b_row = pre_row[gf:gf + 1, :]
            b_col = pre_col[:, gf:gf + 1]
            if ch["d"] == 1:
                b_row = tot_row[gf:gf + 1, :] - b_row + lf_row[gf:gf + 1, :]
                b_col = tot_col[:, gf:gf + 1] - b_col + lf_col[:, gf:gf + 1]
            ch["b_col"] = b_col
            ch["r_row"] = grow[gi:gi + 1, :] - b_row
            ch["b_tot"] = tot_row[gf:gf + 1, :]
        yield

        for ch in chains:
            ws_row = ch["b_tot"] + ch["r_row"]
            m_new = jnp.maximum(ch["b_tot"] + ch["m_old"], jnp.max(ws_row, axis=1, keepdims=True))
            a = jnp.exp(ch["b_tot"] + ch["m_old"] - m_new)
            w_row = jnp.exp(ws_row - m_new)
            kw = (ch["k"].astype(F32).T * w_row).astype(BF16)
            st = ch["st"]
            c_ref[st] = a * ch["c_old"] + jnp.dot(kw, ch["v_ext"], preferred_element_type=F32)
            m_ref[st:st + 1, :] = jnp.broadcast_to(m_new, (1, LANES))
        yield

        for ch in chains:
            r = jnp.where(ch["mask"], ch["r_row"], -jnp.inf)
            m_col = jnp.maximum(jnp.max(r, axis=1, keepdims=True), ch["m_old"])
            dec = jnp.exp(ch["m_old"] - m_col)
            ch["lhs"] = jnp.concatenate([(jnp.exp(r - m_col) * ch["qk"]).astype(BF16),
                                         (ch["q"].astype(F32) * dec).astype(BF16)], axis=1)
            ch["floor"] = jnp.exp(-(ch["b_col"] + m_col))
        yield

        for ch in chains:
            rhs = jnp.concatenate([ch["v_ext"], ch["c_old"].astype(BF16)], axis=0)
            ext = jnp.dot(ch["lhs"], rhs, preferred_element_type=F32)
            num, den = ext[:, :M_HEAD_DIM], ext[:, M_HEAD_DIM:]
            h = num / jnp.maximum(jnp.abs(den), ch["floor"])
            ch["h_ref"][rows[ch["d"]], ch["cols"]] = h.astype(ch["h_ref"].dtype)
        yield


MLSTM_STAGES_PER_CHUNK = 5


def _mixer_kernel(q_ref, k_ref, vt_ref,
                  qf_ref, kf_ref, vf_ref, qb_ref, kb_ref, vb_ref, gcf_ref, gcb_ref, grf_ref, grb_ref, bc_ref, br_ref,
                  o_ref, hf_ref, hb_ref,
                  w_ref, p0_ref, p1_ref, acc_ref, l_ref, c_ref, m_ref, *, n_chunks, tq, kc, stabilise):
    row = lax.broadcasted_iota(jnp.int32, (LANES, tq), 0)
    low = row < HEAD_DIM
    for j in range(GQA_GROUP):
        qt = q_ref[:, j * LANES:(j + 1) * LANES].astype(F32).T
        w_ref[:, (2 * j) * tq:(2 * j + 1) * tq] = jnp.where(low, qt, 0.0).astype(BF16)
        w_ref[:, (2 * j + 1) * tq:(2 * j + 2) * tq] = jnp.where(low, 0.0, qt).astype(BF16)
    acc_ref[...] = jnp.zeros(acc_ref.shape, F32)
    l_ref[...] = jnp.zeros(l_ref.shape, F32)

    @pl.when(pl.program_id(1) == 0)
    def _():
        c_ref[...] = jnp.zeros(c_ref.shape, F32)
        m_ref[...] = jnp.zeros(m_ref.shape, F32)

    def step(c, ms, src=None, dst=None):
        new_ms = []
        if dst is not None:
            start = (c + 1) * kc
            if not isinstance(start, int):
                start = pl.multiple_of(start, kc)
            kch = k_ref[pl.ds(start, kc), :]
        for slot in range(N_Q_HEADS):
            cols = slice(slot * tq, (slot + 1) * tq)
            if dst is not None:
                s = jnp.dot(kch, w_ref[:, cols], preferred_element_type=F32)
                if stabilise:
                    m = ms[slot][0]
                    mn = jnp.maximum(m, jnp.max(s, axis=0, keepdims=True))
                    new_ms.append((mn, jnp.exp2(m - mn)))
                    s = s - mn
                    l_ref[slot] = l_ref[slot] * new_ms[-1][1]
                pr = jnp.exp2(s)
                dst[:, cols] = pr.astype(BF16)
                l_ref[slot] += jnp.sum(pr.reshape(kc // 8, 8, tq), axis=0)
            if src is not None:
                pv = jnp.dot(vt_ref[slot % N_KV_HEADS, c], src[:, cols], preferred_element_type=F32)
                if stabilise:
                    acc_ref[slot] = acc_ref[slot] * ms[slot][1] + pv
                else:
                    acc_ref[slot] += pv
        return ms if dst is None else tuple(new_ms)

    mlstm_steps = tq // M_CHUNK
    mlstm = _mlstm_stages(qf_ref, kf_ref, vf_ref, qb_ref, kb_ref, vb_ref, gcf_ref, gcb_ref, grf_ref, grb_ref,
                          bc_ref, br_ref, hf_ref, hb_ref, c_ref, m_ref, mlstm_steps)
    if stabilise:
        neg_inf = jnp.full((1, tq), -jnp.inf, F32)
        state = step(-1, tuple((neg_inf, neg_inf) for _ in range(N_Q_HEADS)), dst=p0_ref)
        state = lax.fori_loop(
            0, n_chunks // 2 - 1,
            lambda j, ms: step(2 * j + 1, step(2 * j, ms, src=p0_ref, dst=p1_ref), src=p1_ref, dst=p0_ref), state)
        state = step(n_chunks - 2, state, src=p0_ref, dst=p1_ref)
        step(n_chunks - 1, state, src=p1_ref)
        for _ in mlstm:
            pass
    else:
        bufs = (p0_ref, p1_ref)
        steps = ([(-1, None, p0_ref)]
                 + [(c, bufs[c % 2], bufs[(c + 1) % 2]) for c in range(n_chunks - 1)]
                 + [(n_chunks - 1, bufs[(n_chunks - 1) % 2], None)])
        n_stages = mlstm_steps * MLSTM_STAGES_PER_CHUNK
        done = 0
        for idx, (c, src, dst) in enumerate(steps):
            step(c, (), src=src, dst=dst)
            while done < ((idx + 1) * n_stages) // len(steps):
                next(mlstm)
                done += 1

    outs = [acc_ref[slot] / jnp.sum(l_ref[slot], axis=0, keepdims=True) for slot in range(N_Q_HEADS)]
    o_ref[...] = jnp.concatenate(outs, axis=0).T.astype(BF16)


def _mixers(q, k, vt, m4, gate, gate_t, p, B, S, stabilise):
    T = q.shape[0]
    tq, kc = ATTN_Q_TILE, ATTN_KV_CHUNK
    nq, n_chunks = S // tq, S // kc
    kern = functools.partial(_mixer_kernel, n_chunks=n_chunks, tq=tq, kc=kc, stabilise=stabilise)
    fwd = lambda col: (lambda b, i: (b * nq + i, col))
    bwd = lambda col: (lambda b, i: (b * nq + nq - 1 - i, col))
    blk = lambda f: pl.BlockSpec((tq, M_WIDTH), f)
    return pl.pallas_call(
        kern,
        grid=(B, nq),
        in_specs=[
            pl.BlockSpec((tq, ATTN_WIDTH), fwd(0)),
            pl.BlockSpec((S, KV_WIDTH), lambda b, i: (b, 0)),
            pl.BlockSpec((None, N_KV_HEADS, n_chunks, HEAD_DIM, kc), lambda b, i: (b, 0, 0, 0, 0)),
            blk(fwd(0)), blk(fwd(1)), blk(fwd(2)),
            blk(bwd(0)), blk(bwd(1)), blk(bwd(2)),
            pl.BlockSpec((tq, N_GATE_PRE), fwd(0)),
            pl.BlockSpec((tq, N_GATE_PRE), bwd(0)),
            pl.BlockSpec((N_GATE_PRE, tq), lambda b, i: (0, b * nq + i)),
            pl.BlockSpec((N_GATE_PRE, tq), lambda b, i: (0, b * nq + nq - 1 - i)),
            _const_spec((1, N_GATE_PRE)), _const_spec((N_GATE_PRE, 1)),
        ],
        out_specs=[pl.BlockSpec((tq, ATTN_WIDTH), fwd(0)), blk(fwd(0)), blk(bwd(0))],
        out_shape=[jax.ShapeDtypeStruct((T, ATTN_WIDTH), BF16)] + [jax.ShapeDtypeStruct((T, M_WIDTH), BF16)] * 2,
        scratch_shapes=[
            pltpu.VMEM((LANES, N_Q_HEADS * tq), BF16),
            pltpu.VMEM((kc, N_Q_HEADS * tq), BF16),
            pltpu.VMEM((kc, N_Q_HEADS * tq), BF16),
            pltpu.VMEM((N_Q_HEADS, HEAD_DIM, tq), F32),
            pltpu.VMEM((N_Q_HEADS, 8, tq), F32),
            pltpu.VMEM((2 * M_HEADS, M_HEAD_DIM, 2 * M_HEAD_DIM), F32),
            pltpu.VMEM((2 * M_HEADS, LANES), F32),
        ],
        compiler_params=pltpu.CompilerParams(
            dimension_semantics=("parallel", "arbitrary"), vmem_limit_bytes=VMEM_LIMIT),
        name="mixers",
    )(q, k, vt, m4, m4, m4, m4, m4, m4, gate, gate, gate_t, gate_t, p["b_if_row"], p["b_if_col"])


def _merge_kernel(a_ref, hf_ref, hb_ref, so_ref, sg_ref, x_ref, gm_ref, wao_ref, wmo_ref, wout_ref, gp_ref, o_ref):
    hs = hf_ref[...].astype(F32) + hb_ref[...].astype(F32)
    parts = []
    for hd in range(M_HEADS):
        blk = hs[:, hd * M_HEAD_DIM:(hd + 1) * M_HEAD_DIM]
        ms = jnp.mean(blk * blk, axis=-1, keepdims=True)
        parts.append(blk * lax.rsqrt(ms + EPS))
    hn = jnp.concatenate(parts, axis=1) * gm_ref[...]
    hm = (hn * so_ref[...].astype(F32)).astype(BF16)
    a_out = jnp.dot(a_ref[...], wao_ref[...], preferred_element_type=F32)
    m_out = jnp.dot(hm, wmo_ref[...], preferred_element_type=F32)
    sg = sg_ref[...]
    merged = sg[:, :D_MODEL].astype(F32) * a_out + sg[:, D_MODEL:].astype(F32) * m_out
    y = jnp.dot(merged.astype(BF16), wout_ref[...], preferred_element_type=F32)
    ms = jnp.mean(y * y, axis=-1, keepdims=True)
    o_ref[...] = x_ref[...] + y * lax.rsqrt(ms + EPS) * gp_ref[...]


def _merge(a, hf, hb, m4, sg, x2, p):
    T = x2.shape[0]
    tm = TOKEN_TILE
    row = lambda i: (i, 0)
    return pl.pallas_call(
        _merge_kernel,
        grid=(T // tm,),
        in_specs=[
            pl.BlockSpec((tm, ATTN_WIDTH), row),
            pl.BlockSpec((tm, M_WIDTH), row),
            pl.BlockSpec((tm, M_WIDTH), row),
            pl.BlockSpec((tm, M_WIDTH), lambda i: (i, 3)),
            pl.BlockSpec((tm, 2 * D_MODEL), row),
            pl.BlockSpec((tm, D_MODEL), row),
            _const_spec((1, M_WIDTH)),
            _const_spec((ATTN_WIDTH, D_MODEL)), _const_spec((M_WIDTH, D_MODEL)),
            _const_spec((D_MODEL, D_MODEL)), _const_spec((1, D_MODEL)),
        ],
        out_specs=pl.BlockSpec((tm, D_MODEL), row),
        out_shape=jax.ShapeDtypeStruct((T, D_MODEL), F32),
        compiler_params=pltpu.CompilerParams(
            dimension_semantics=("parallel",), vmem_limit_bytes=VMEM_LIMIT),
        name="merge",
    )(a, hf, hb, m4, sg, x2, p["g_mlstm"], p["wao"], p["wmo"], p["wout"], p["g_mix_post"])


def _ffn_kernel(xp_ref, x_ref, xn_ref, g_ref, wg_ref, wv_ref, cw_ref, cb_ref, wd_ref, gp_ref, o_ref,
                h_ref, ge_ref, *, tiles_per_seq):
    tm = x_ref.shape[0]
    g = g_ref[...]

    def norm(x):
        ms = jnp.mean(x * x, axis=-1, keepdims=True)
        return (x * lax.rsqrt(ms + EPS) * g).astype(BF16)

    h_ref[0:HALO, :] = norm(xp_ref[...])
    h_ref[HALO:HALO + tm, :] = norm(x_ref[...])
    h_ref[HALO + tm:, :] = norm(xn_ref[...])
    pos = pl.program_id(0) % tiles_per_seq
    keep_prev = (pos != 0).astype(F32)
    keep_next = (pos != tiles_per_seq - 1).astype(F32)

    ge_ref[...] = jnp.dot(h_ref[...], wg_ref[...], preferred_element_type=F32)
    ge_ref[HALO - 1:HALO, :] = ge_ref[HALO - 1:HALO, :] * keep_prev
    ge_ref[HALO + tm:HALO + tm + 1, :] = ge_ref[HALO + tm:HALO + tm + 1, :] * keep_next
    half = tm // FFN_ROW_SPLIT
    vals = [jnp.dot(h_ref[HALO + r * half:HALO + (r + 1) * half, :], wv_ref[...], preferred_element_type=F32)
            for r in range(FFN_ROW_SPLIT)]
    cw = cw_ref[...]
    for r in range(FFN_ROW_SPLIT):
        r0 = HALO + r * half
        conv = (ge_ref[r0 - 1:r0 - 1 + half, :] * cw[0:1]
                + ge_ref[r0:r0 + half, :] * cw[1:2]
                + ge_ref[r0 + 1:r0 + 1 + half, :] * cw[2:3]
                + cb_ref[...])
        act = (jax.nn.gelu(conv, approximate=True) * vals[r]).astype(BF16)
        y = jnp.dot(act, wd_ref[...], preferred_element_type=F32)
        ms = jnp.mean(y * y, axis=-1, keepdims=True)
        rows = slice(r * half, (r + 1) * half)
        o_ref[rows, :] = x_ref[rows, :] + y * lax.rsqrt(ms + EPS) * gp_ref[...]


def _ffn(x1, p, S):
    T = x1.shape[0]
    tm = FFN_TOKEN_TILE
    hpt = tm // HALO
    n_halo = T // HALO
    kern = functools.partial(_ffn_kernel, tiles_per_seq=S // tm)
    return pl.pallas_call(
        kern,
        grid=(T // tm,),
        in_specs=[
            pl.BlockSpec((HALO, D_MODEL), lambda i: (jnp.maximum(i * hpt - 1, 0), 0)),
            pl.BlockSpec((tm, D_MODEL), lambda i: (i, 0)),
            pl.BlockSpec((HALO, D_MODEL), lambda i: (jnp.minimum((i + 1) * hpt, n_halo - 1), 0)),
            _const_spec((1, D_MODEL)),
            _const_spec((D_MODEL, D_FF)), _const_spec((D_MODEL, D_FF)),
            _const_spec((8, D_FF)), _const_spec((1, D_FF)),
            _const_spec((D_FF, D_MODEL)), _const_spec((1, D_MODEL)),
        ],
        out_specs=pl.BlockSpec((tm, D_MODEL), lambda i: (i, 0)),
        out_shape=jax.ShapeDtypeStruct((T, D_MODEL), F32),
        scratch_shapes=[
            pltpu.VMEM((tm + 2 * HALO, D_MODEL), BF16),
            pltpu.VMEM((tm + 2 * HALO, D_FF), F32),
        ],
        compiler_params=pltpu.CompilerParams(
            dimension_semantics=("parallel",), vmem_limit_bytes=VMEM_LIMIT),
        name="ffn",
    )(x1, x1, x1, p["g_ffn_pre"], p["wup_g"], p["wup_v"], p["conv_w"], p["conv_b"], p["wdown"], p["g_ffn_post"])


def _rope_tables(S, q_gain, k_gain):
    rows = S // GRID_W
    r = jnp.repeat(jnp.arange(rows, dtype=F32), GRID_W)
    c = jnp.tile(jnp.arange(GRID_W, dtype=F32), rows)
    n_pairs_axis = HEAD_DIM // 4
    freq = ROPE_THETA ** (-jnp.arange(n_pairs_axis, dtype=F32) / n_pairs_axis)
    ang = jnp.concatenate([r[:, None] * freq, c[:, None] * freq], axis=-1)
    cos, sin = jnp.cos(ang), jnp.sin(ang)
    cc = jnp.concatenate([cos, cos], axis=-1)
    ss = jnp.concatenate([-sin, sin], axis=-1)

    def tables(gain, scale):
        g_half = jnp.concatenate([gain[0::2], gain[1::2]])
        g_swap = jnp.concatenate([gain[1::2], gain[0::2]])
        ta = cc * g_half * scale
        tb = ss * g_swap * scale
        return jnp.tile(ta, (1, LANES // HEAD_DIM)), jnp.tile(tb, (1, LANES // HEAD_DIM))

    tqa, tqb = tables(q_gain, LOG2E * HEAD_DIM ** -0.5)
    tka, tkb = tables(k_gain, 1.0)
    return tqa, tqb, tka, tkb


def _prepare(norm_mix_pre, w_in, b_if, q_norm, k_norm, mlstm_norm, w_attn_o, w_mlstm_o, w_out,
             norm_mix_post, norm_ffn_pre, w_up, conv_w, conv_b, w_down, norm_ffn_post, seq_lens):
    half = np.concatenate([np.arange(0, HEAD_DIM, 2), np.arange(1, HEAD_DIM, 2)])
    slots = [h for j in range(GQA_GROUP) for h in (j, j + GQA_GROUP)]
    q_cols = np.concatenate([h * HEAD_DIM + half for h in slots])
    k_cols = ATTN_WIDTH + np.concatenate([h * HEAD_DIM + half for h in range(N_KV_HEADS)])
    v_cols = ATTN_WIDTH + KV_WIDTH + np.arange(KV_WIDTH)
    o_rows = np.concatenate([h * HEAD_DIM + np.arange(HEAD_DIM) for h in slots])
    m0 = ATTN_WIDTH + 2 * KV_WIDTH
    g0 = m0 + 4 * M_WIDTH
    s0 = g0 + N_GATE_PRE

    p = {}
    p["wa"] = w_in[:, np.concatenate([q_cols, k_cols, v_cols])].astype(BF16)
    p["wm"] = w_in[:, m0:g0].astype(BF16)
    p["wg"] = jnp.pad(w_in[:, g0:s0], ((0, 0), (0, LANES - N_GATE_PRE))).astype(BF16)
    p["ws"] = w_in[:, s0:].astype(BF16)
    blk = np.arange(MXU_DIM) // HEAD_DIM
    p["bd"] = jnp.asarray((blk[:, None] == blk[None, :]).astype(np.float32) / HEAD_DIM, BF16)
    p["g_mix_pre"] = norm_mix_pre.reshape(1, D_MODEL)
    p["b_if_row"] = b_if.reshape(1, N_GATE_PRE)
    p["b_if_col"] = b_if.reshape(N_GATE_PRE, 1)
    p["g_mlstm"] = mlstm_norm.reshape(1, M_WIDTH)
    p["wao"] = w_attn_o[o_rows].astype(BF16)
    p["wmo"] = w_mlstm_o.astype(BF16)
    p["wout"] = w_out.astype(BF16)
    p["g_mix_post"] = norm_mix_post.reshape(1, D_MODEL)
    p["g_ffn_pre"] = norm_ffn_pre.reshape(1, D_MODEL)
    p["wup_g"] = w_up[:, :D_FF].astype(BF16)
    p["wup_v"] = w_up[:, D_FF:].astype(BF16)
    p["conv_w"] = jnp.pad(conv_w, ((0, 8 - CONV_W), (0, 0)))
    p["conv_b"] = conv_b.reshape(1, D_FF)
    p["wdown"] = w_down.astype(BF16)
    p["g_ffn_post"] = norm_ffn_post.reshape(1, D_MODEL)
    p["score_bound"] = (1.02 * LOG2E * HEAD_DIM ** 0.5) * jnp.max(jnp.abs(q_norm)) * jnp.max(jnp.abs(k_norm))
    p["tqa"], p["tqb"], p["tka"], p["tkb"] = {}, {}, {}, {}
    for S in seq_lens:
        p["tqa"][S], p["tqb"][S], p["tka"][S], p["tkb"][S] = _rope_tables(S, q_norm, k_norm)
    return p


def _check_tiling(S):
    assert S % GRID_W == 0
    assert S % TOKEN_TILE == 0 and S % FFN_TOKEN_TILE == 0 and S % ATTN_Q_TILE == 0
    assert TOKEN_TILE % ATTN_KV_CHUNK == 0 and (S // ATTN_KV_CHUNK) % 2 == 0 and S // ATTN_KV_CHUNK >= 4
    assert ATTN_Q_TILE % M_CHUNK == 0 and ATTN_Q_TILE % LANES == 0 and ATTN_KV_CHUNK % LANES == 0
    assert FFN_TOKEN_TILE % (FFN_ROW_SPLIT * HALO) == 0 and D_FF % LANES == 0


def _trunk(x, p):
    B, S, _ = x.shape
    _check_tiling(S)
    T = B * S
    x2 = x.reshape(T, D_MODEL)
    q, k, vt, m4, gate, gate_t, sg = _in_proj(x2, p, S)
    a, hf, hb = lax.cond(p["score_bound"] <= MAX_UNSTABILISED_SCORE,
                         lambda: _mixers(q, k, vt, m4, gate, gate_t, p, B, S, False),
                         lambda: _mixers(q, k, vt, m4, gate, gate_t, p, B, S, True))
    x1 = _merge(a, hf, hb, m4, sg, x2, p)
    y = _ffn(x1, p, S)
    return y.reshape(B, S, D_MODEL)


def kernel(x_prompt, x_sample, norm_mix_pre, w_in, b_if, q_norm, k_norm, mlstm_norm, w_attn_o, w_mlstm_o,
           w_out, norm_mix_post, norm_ffn_pre, w_up, conv_w, conv_b, w_down, norm_ffn_post):
    depth = w_in.shape[0]
    seq_lens = sorted({x_prompt.shape[1], x_sample.shape[1]})
    layers = [
        _prepare(norm_mix_pre[l], w_in[l], b_if[l], q_norm[l], k_norm[l], mlstm_norm[l], w_attn_o[l],
                 w_mlstm_o[l], w_out[l], norm_mix_post[l], norm_ffn_pre[l], w_up[l], conv_w[l], conv_b[l],
                 w_down[l], norm_ffn_post[l], seq_lens)
        for l in range(depth)
    ]

    def trunk(x):
        for p in layers:
            x = _trunk(x, p)
        return x

    return (trunk(x_prompt), trunk(x_sample))
```
